```python
import math
import jax, jax.numpy as jnp
from jax import lax
import numpy as np

D_MODEL = 2048
BATCH = 4
SEQ = 4096
DEPTH = 2

CHUNK = 64
Q_BLOCK = 128
N_MIXERS = 2
N_MOD = 9
D_FF = 5632
EPS = 1e-5

DA_HEADS = 8
DA_HEAD_DIM = D_MODEL // (2 * DA_HEADS)
DA_V_DIM = 2 * DA_HEAD_DIM
DA_QKV_WIDTH = 2 * 2 * DA_HEADS * DA_HEAD_DIM + DA_HEADS * DA_V_DIM

RET_HEADS = 8
RET_QK_DIM = D_MODEL // RET_HEADS
RET_V_DIM = 2 * RET_QK_DIM
RET_V_WIDTH = RET_HEADS * RET_V_DIM
RET_PROJ_WIDTH = 2 * RET_HEADS * RET_QK_DIM + 2 * RET_V_WIDTH

N_DA_LAYERS = (DEPTH + 1) // 2
N_RET_LAYERS = DEPTH // 2

kernel_name = "hybrid_diffattn_retention_macaron_block"

F32 = jnp.float32


def rms_norm(x, g):
    xf = x.astype(F32)
    y = xf * lax.rsqrt(jnp.mean(xf * xf, axis=-1, keepdims=True) + EPS)
    return (y * g.astype(F32)).astype(x.dtype)


def head_layer_norm(o, g):
    of = o.astype(F32)
    mu = jnp.mean(of, axis=-1, keepdims=True)
    d = of - mu
    y = d * lax.rsqrt(jnp.mean(d * d, axis=-1, keepdims=True) + EPS)
    return (y * g.astype(F32)).astype(o.dtype)


def modulate(h, shift, scale):
    return h * (1.0 + scale[:, None, :]) + shift[:, None, :]


def swiglu(h, w_in, w_out):
    a, b = jnp.split(h @ w_in, 2, axis=-1)
    return (jax.nn.silu(a) * b) @ w_out


def alibi_slopes(n_heads):
    return jnp.asarray([2.0 ** (-8.0 * (h + 1) / n_heads) for h in range(n_heads)], dtype=F32)


def diff_attention(h, w_qkv, lam, subln_g, w_o, lambda_init):
    B, S, _ = h.shape
    H, d = DA_HEADS, DA_HEAD_DIM
    q, k, v = jnp.split(h @ w_qkv, [2 * H * d, 4 * H * d], axis=-1)
    q = q.reshape(B, S, H, 2, d) * (d ** -0.5)
    k = k.reshape(B, S, H, 2, d)
    v = v.reshape(B, S, H, DA_V_DIM)
    lf = lam.astype(F32)
    lam_val = jnp.exp(jnp.sum(lf[0] * lf[1])) - jnp.exp(jnp.sum(lf[2] * lf[3])) + lambda_init
    slopes = alibi_slopes(H)
    pos = jnp.arange(S)
    key_chunk = pos // CHUNK
    nb = S // Q_BLOCK
    qb = q.reshape(B, nb, Q_BLOCK, H, 2, d).transpose(1, 0, 2, 3, 4, 5)

    def block(args):
        q_blk, b_idx = args
        t = b_idx * Q_BLOCK + jnp.arange(Q_BLOCK)
        s = jnp.einsum("bqhnd,bkhnd->bhnqk", q_blk, k).astype(F32)
        dist = jnp.abs(t[:, None] - pos[None, :]).astype(F32)
        s = s - slopes[:, None, None, None] * dist[None, None]
        allowed = key_chunk[None, :] <= (t // CHUNK)[:, None]
        s = jnp.where(allowed, s, -jnp.inf)
        p = jax.nn.softmax(s, axis=-1)
        a = p[:, :, 0] - lam_val * p[:, :, 1]
        return jnp.einsum("bhqk,bkhe->bqhe", a.astype(v.dtype), v)

    o = lax.map(block, (qb, jnp.arange(nb)))
    o = o.transpose(1, 0, 2, 3, 4).reshape(B, S, H, DA_V_DIM)
    o = rms_norm(o, subln_g) * (1.0 - lambda_init)
    return o.reshape(B, S, H * DA_V_DIM) @ w_o


def retention(h, w_qkvg, gn_g, w_o):
    B, S, _ = h.shape
    H, dk, dv, C = RET_HEADS, RET_QK_DIM, RET_V_DIM, CHUNK
    nc = S // C
    q, k, v, g = jnp.split(h @ w_qkvg, [H * dk, 2 * H * dk, 2 * H * dk + H * dv], axis=-1)
    q = q.astype(F32).reshape(B, nc, C, H, dk).transpose(1, 0, 3, 2, 4)
    k = (k.astype(F32) * (dk ** -0.5)).reshape(B, nc, C, H, dk).transpose(1, 0, 3, 2, 4)
    v = v.astype(F32).reshape(B, nc, C, H, dv).transpose(1, 0, 3, 2, 4)
    log_gamma = jnp.log(1.0 - 2.0 ** (-5.0 - jnp.arange(H, dtype=F32)))
    idx = jnp.arange(C, dtype=F32)
    inner_decay = jnp.exp(log_gamma[:, None, None] * jnp.abs(idx[:, None] - idx[None, :]))
    q_decay = jnp.exp(log_gamma[:, None] * idx[None, :])[..., None]
    k_decay = jnp.exp(log_gamma[:, None] * (C - idx)[None, :])[..., None]
    chunk_decay = jnp.exp(log_gamma * C)[:, None, None]

    def step(state, inp):
        qc, kc, vc = inp
        inner = jnp.einsum("bhid,bhjd->bhij", qc, kc) * inner_decay
        o = (jnp.einsum("bhij,bhje->bhie", inner, vc)
             + jnp.einsum("bhid,bhde->bhie", qc * q_decay, state))
        state = state * chunk_decay + jnp.einsum("bhjd,bhje->bhde", kc * k_decay, vc)
        return state, o

    state0 = jnp.zeros((B, H, dk, dv), F32)
    _, o = lax.scan(step, state0, (q, k, v))
    o = o.transpose(1, 0, 3, 2, 4).reshape(B, S, H, dv).astype(h.dtype)
    o = head_layer_norm(o, gn_g).reshape(B, S, H * dv)
    return (jax.nn.silu(g) * o) @ w_o


def setup_inputs(seed: int = 0) -> dict:
    key = jax.random.key(seed)
    ks = jax.random.split(key, 20)
    D, F = D_MODEL, D_FF
    nrm = lambda k, shape, s: jax.random.normal(k, shape, F32) * s
    return {
        "x": nrm(ks[0], (BATCH, SEQ, D), 1.0),
        "c": nrm(ks[1], (BATCH, D), 1.0),
        "ada_w": nrm(ks[2], (DEPTH, D, N_MOD * D), 0.5 * D ** -0.5),
        "ada_b": nrm(ks[3], (DEPTH, N_MOD * D), 0.02),
        "norm_g": 1.0 + nrm(ks[4], (DEPTH, 3, D), 0.02),
        "ffn_w_in": nrm(ks[5], (DEPTH, 2, D, 2 * F), D ** -0.5),
        "ffn_w_out": nrm(ks[6], (DEPTH, 2, F, D), F ** -0.5),
        "da_w_qkv": nrm(ks[7], (N_DA_LAYERS, D, DA_QKV_WIDTH), D ** -0.5),
        "da_lambda": nrm(ks[8], (N_DA_LAYERS, 4, DA_HEAD_DIM), 0.1),
        "da_subln_g": 1.0 + nrm(ks[9], (N_DA_LAYERS, DA_V_DIM), 0.02),
        "da_w_o": nrm(ks[10], (N_DA_LAYERS, DA_HEADS * DA_V_DIM, D), (DA_HEADS * DA_V_DIM) ** -0.5),
        "ret_w_qkvg": nrm(ks[11], (N_RET_LAYERS, D, RET_PROJ_WIDTH), D ** -0.5),
        "ret_gn_g": 1.0 + nrm(ks[12], (N_RET_LAYERS, RET_V_DIM), 0.02),
        "ret_w_o": nrm(ks[13], (N_RET_LAYERS, RET_V_WIDTH, D), RET_V_WIDTH ** -0.5),
        "final_g": 1.0 + nrm(ks[14], (D,), 0.02),
    }


def reference(x, c, ada_w, ada_b, norm_g, ffn_w_in, ffn_w_out, da_w_qkv, da_lambda, da_subln_g,
              da_w_o, ret_w_qkvg, ret_gn_g, ret_w_o, final_g):
    cs = jax.nn.silu(c)
    for i in range(DEPTH):
        mod = cs @ ada_w[i] + ada_b[i]
        sh1, sc1, g1, sh2, sc2, g2, sh3, sc3, g3 = jnp.split(mod, N_MOD, axis=-1)
        h = modulate(rms_norm(x, norm_g[i, 0]), sh1, sc1)
        x = x + 0.5 * g1[:, None, :] * swiglu(h, ffn_w_in[i, 0], ffn_w_out[i, 0])
        h = modulate(rms_norm(x, norm_g[i, 1]), sh2, sc2)
        j = i // N_MIXERS
        if i % N_MIXERS == 0:
            lambda_init = 0.8 - 0.6 * math.exp(-0.3 * i)
            y = diff_attention(h, da_w_qkv[j], da_lambda[j], da_subln_g[j], da_w_o[j], lambda_init)
        else:
            y = retention(h, ret_w_qkvg[j], ret_gn_g[j], ret_w_o[j])
        x = x + g2[:, None, :] * y
        h = modulate(rms_norm(x, norm_g[i, 2]), sh3, sc3)
        x = x + 0.5 * g3[:, None, :] * swiglu(h, ffn_w_in[i, 1], ffn_w_out[i, 1])
    return rms_norm(x, final_g)
```

```python
import functools
import math

import jax
import jax.numpy as jnp
from jax import lax
from jax.experimental import pallas as pl
from jax.experimental.pallas import tpu as pltpu

F32 = jnp.float32
BF16 = jnp.bfloat16

EPS = 1e-5
CHUNK = 64
N_MOD = 9
N_MIXERS = 2
DA_HEADS = 8
RET_HEADS = 8

V7X_VMEM_BYTES = 64 * 1024 * 1024
VMEM_LIMIT_BYTES = V7X_VMEM_BYTES - 8 * 1024 * 1024


def _params(*semantics):
    return pltpu.CompilerParams(dimension_semantics=semantics, vmem_limit_bytes=VMEM_LIMIT_BYTES)


def _dot(a, b):
    return jnp.dot(a, b, preferred_element_type=F32)


def _dot_nt(a, b):
    return lax.dot_general(a, b, (((1,), (1,)), ((), ())), preferred_element_type=F32)


def _dot_tn(a, b):
    return lax.dot_general(a, b, (((0,), (0,)), ((), ())), preferred_element_type=F32)


def _rms(x, g):
    return x * lax.rsqrt(jnp.mean(x * x, axis=-1, keepdims=True) + EPS) * g


def _norm_mod(x, g, shift, scale):
    return _rms(x, g) * (1.0 + scale) + shift


def _ada_kernel(c_ref, w_ref, b_ref, o_ref):
    cs = jax.nn.silu(c_ref[...]).astype(BF16)
    o_ref[...] = _dot(cs, w_ref[...].astype(BF16)) + b_ref[...]


def _ada_mods(c, ada_w, ada_b):
    B, D = c.shape
    L, _, W = ada_w.shape
    rows = 8
    tn = min(1024, W)
    c_pad = jnp.zeros((rows, D), F32).at[:B].set(c)
    out = pl.pallas_call(
        _ada_kernel,
        out_shape=jax.ShapeDtypeStruct((L, rows, W), F32),
        grid=(L, W // tn),
        in_specs=[
            pl.BlockSpec((rows, D), lambda l, n: (0, 0)),
            pl.BlockSpec((None, D, tn), lambda l, n: (l, 0, n)),
            pl.BlockSpec((None, 1, tn), lambda l, n: (l, 0, n)),
        ],
        out_specs=pl.BlockSpec((None, rows, tn), lambda l, n: (l, 0, n)),
        compiler_params=_params("parallel", "parallel"),
        name="ada_mods",
    )(c_pad, ada_w, ada_b.reshape(L, 1, W))
    return out[:, :B].reshape(L, B, N_MOD, D).transpose(0, 2, 1, 3)[:, :, :, None, :]


def _ffn_kernel(x_ref, sh_ref, sc_ref, gt_ref, ng_ref, wg_ref, wu_ref, wo_ref, *rest, final):
    if final:
        fg_ref, o_ref, h_ref, acc_ref = rest
    else:
        o_ref, h_ref, acc_ref = rest
    j = pl.program_id(1)

    @pl.when(j == 0)
    def _():
        h_ref[...] = _norm_mod(x_ref[...], ng_ref[...], sh_ref[...], sc_ref[...]).astype(BF16)
        acc_ref[...] = jnp.zeros_like(acc_ref)

    h = h_ref[...]
    a = _dot(h, wg_ref[...])
    b = _dot(h, wu_ref[...])
    act = (jax.nn.silu(a) * b).astype(BF16)
    acc_ref[...] += _dot(act, wo_ref[...])

    @pl.when(j == pl.num_programs(1) - 1)
    def _():
        y = x_ref[...] + 0.5 * gt_ref[...] * acc_ref[...]
        if final:
            y = _rms(y, fg_ref[...])
        o_ref[...] = y


def _ffn(x, mods, k0, norm_g, w_in, w_out, final_g=None, *, seq):
    M, D = x.shape
    F = w_out.shape[0]
    tm = min(512, seq)
    tf = min(512, F)
    nf = F // tf
    bpb = seq // tm
    vec = lambda k: pl.BlockSpec((None, None, 1, D), lambda i, j: (k, i // bpb, 0, 0))
    in_specs = [
        pl.BlockSpec((tm, D), lambda i, j: (i, 0)),
        vec(k0), vec(k0 + 1), vec(k0 + 2),
        pl.BlockSpec((1, D), lambda i, j: (0, 0)),
        pl.BlockSpec((D, tf), lambda i, j: (0, j)),
        pl.BlockSpec((D, tf), lambda i, j: (0, nf + j)),
        pl.BlockSpec((tf, D), lambda i, j: (j, 0)),
    ]
    args = [x, mods, mods, mods, norm_g.reshape(1, D), w_in, w_in, w_out]
    if final_g is not None:
        in_specs.append(pl.BlockSpec((1, D), lambda i, j: (0, 0)))
        args.append(final_g.reshape(1, D))
    return pl.pallas_call(
        functools.partial(_ffn_kernel, final=final_g is not None),
        out_shape=jax.ShapeDtypeStruct((M, D), F32),
        grid=(M // tm, nf),
        in_specs=in_specs,
        out_specs=pl.BlockSpec((tm, D), lambda i, j: (i, 0)),
        scratch_shapes=[pltpu.VMEM((tm, D), BF16), pltpu.VMEM((tm, D), F32)],
        compiler_params=_params("parallel", "arbitrary"),
        name="ffn",
    )(*args)


def _nm_matmul_kernel(x_ref, sh_ref, sc_ref, ng_ref, w_ref, o_ref, h_ref):
    @pl.when(pl.program_id(1) == 0)
    def _():
        h_ref[...] = _norm_mod(x_ref[...], ng_ref[...], sh_ref[...], sc_ref[...]).astype(BF16)

    o_ref[...] = _dot(h_ref[...], w_ref[...]).astype(o_ref.dtype)


def _norm_mod_matmul(x, mods, k0, norm_g, w, *, seq):
    M, D = x.shape
    N = w.shape[1]
    tm = min(1024, seq)
    tn = min(1024, N)
    bpb = seq // tm
    vec = lambda k: pl.BlockSpec((None, None, 1, D), lambda i, n: (k, i // bpb, 0, 0))
    return pl.pallas_call(
        _nm_matmul_kernel,
        out_shape=jax.ShapeDtypeStruct((M, N), BF16),
        grid=(M // tm, N // tn),
        in_specs=[
            pl.BlockSpec((tm, D), lambda i, n: (i, 0)),
            vec(k0), vec(k0 + 1),
            pl.BlockSpec((1, D), lambda i, n: (0, 0)),
            pl.BlockSpec((D, tn), lambda i, n: (0, n)),
        ],
        out_specs=pl.BlockSpec((tm, tn), lambda i, n: (i, n)),
        scratch_shapes=[pltpu.VMEM((tm, D), BF16)],
        compiler_params=_params("parallel", "arbitrary"),
        name="norm_mod_matmul",
    )(x, mods, mods, norm_g.reshape(1, D), w)


def _mm_res_kernel(a_ref, w_ref, x_ref, gt_ref, o_ref):
    o_ref[...] = x_ref[...] + gt_ref[...] * _dot(a_ref[...], w_ref[...])


def _matmul_residual(a, w, x, mods, k, *, seq):
    M, K = a.shape
    N = w.shape[1]
    tm = min(1024, seq)
    tn = min(512, N)
    bpb = seq // tm
    return pl.pallas_call(
        _mm_res_kernel,
        out_shape=jax.ShapeDtypeStruct((M, N), F32),
        grid=(M // tm, N // tn),
        in_specs=[
            pl.BlockSpec((tm, K), lambda i, n: (i, 0)),
            pl.BlockSpec((K, tn), lambda i, n: (0, n)),
            pl.BlockSpec((tm, tn), lambda i, n: (i, n)),
            pl.BlockSpec((None, None, 1, tn), lambda i, n: (k, i // bpb, 0, n)),
        ],
        out_specs=pl.BlockSpec((tm, tn), lambda i, n: (i, n)),
        compiler_params=_params("parallel", "parallel"),
        name="matmul_residual",
    )(a, w, x, mods)


def _da_attn_kernel(q_ref, k_ref, v_ref, lam_ref, g_ref, o_ref, m_ref, l_ref, acc_ref,
                    *, tq, hd, n_heads, lambda_init):
    head = pl.program_id(1)
    qi = pl.program_id(2)
    tk = tq
    hv = jnp.full((1, tk), head + 1, jnp.int32).astype(F32)
    slope = jnp.exp2(hv * (-8.0 / n_heads))
    col = lax.broadcasted_iota(jnp.int32, (1, tk), 1).astype(F32)

    qs = (q_ref[:, 0:hd], q_ref[:, hd:2 * hd])

    r = lax.broadcasted_iota(jnp.int32, (tq, tk), 0)
    c = lax.broadcasted_iota(jnp.int32, (tq, tk), 1)
    bias_diag = slope * (r - jnp.abs(r - c)).astype(F32)
    allowed = (c // CHUNK) <= (r // CHUNK)
    k0 = pl.multiple_of(qi * tk, tk)
    v_d = v_ref[pl.ds(k0, tk), :]
    for n in range(2):
        k_d = k_ref[pl.ds(k0, tk), n * hd:(n + 1) * hd]
        s = jnp.where(allowed, _dot_nt(qs[n], k_d) + bias_diag, -jnp.inf)
        m = jnp.max(s, axis=-1, keepdims=True)
        p = jnp.exp(s - m)
        m_ref[n] = m
        l_ref[n] = jnp.sum(p, axis=-1, keepdims=True)
        acc_ref[n] = _dot(p.astype(BF16), v_d)

    def body(kb, carry):
        kk = pl.multiple_of(kb * tk, tk)
        bias = slope * (col + (kb - qi).astype(F32) * tk)
        v_b = v_ref[pl.ds(kk, tk), :]
        for n in range(2):
            k_b = k_ref[pl.ds(kk, tk), n * hd:(n + 1) * hd]
            s = _dot_nt(qs[n], k_b) + bias
            m_old = m_ref[n]
            m_new = jnp.maximum(m_old, jnp.max(s, axis=-1, keepdims=True))
            alpha = jnp.exp(m_old - m_new)
            p = jnp.exp(s - m_new)
            m_ref[n] = m_new
            l_ref[n] = alpha * l_ref[n] + jnp.sum(p, axis=-1, keepdims=True)
            acc_ref[n] = alpha * acc_ref[n] + _dot(p.astype(BF16), v_b)
        return carry

    lax.fori_loop(0, qi, body, 0)

    lam = lam_ref[...]
    lam_val = (jnp.exp(jnp.sum(lam[0:1] * lam[1:2], axis=-1, keepdims=True))
               - jnp.exp(jnp.sum(lam[2:3] * lam[3:4], axis=-1, keepdims=True)) + lambda_init)
    o = acc_ref[0] / l_ref[0] - lam_val * (acc_ref[1] / l_ref[1])
    o_ref[...] = (_rms(o, g_ref[...]) * (1.0 - lambda_init)).astype(o_ref.dtype)


def _diff_attention(qkv, lam, subln_g, *, batch, seq, lambda_init):
    M = qkv.shape[0]
    H = DA_HEADS
    hd = qkv.shape[1] // (6 * H)
    vd = 2 * hd
    tq = min(256, seq)
    nq = seq // tq
    return pl.pallas_call(
        functools.partial(_da_attn_kernel, tq=tq, hd=hd, n_heads=H, lambda_init=lambda_init),
        out_shape=jax.ShapeDtypeStruct((M, H * vd), BF16),
        grid=(batch, H, nq),
        in_specs=[
            pl.BlockSpec((tq, vd), lambda b, h, i: (b * nq + i, h)),
            pl.BlockSpec((seq, vd), lambda b, h, i: (b, H + h)),
            pl.BlockSpec((seq, vd), lambda b, h, i: (b, 2 * H + h)),
            pl.BlockSpec((4, hd), lambda b, h, i: (0, 0)),
            pl.BlockSpec((1, vd), lambda b, h, i: (0, 0)),
        ],
        out_specs=pl.BlockSpec((tq, vd), lambda b, h, i: (b * nq + i, h)),
        scratch_shapes=[pltpu.VMEM((2, tq, 1), F32), pltpu.VMEM((2, tq, 1), F32),
                        pltpu.VMEM((2, tq, vd), F32)],
        compiler_params=_params("parallel", "parallel", "arbitrary"),
        name="diff_attention",
    )(qkv, qkv, qkv, lam, subln_g.reshape(1, vd))


def _ret_log_gamma(h):
    return math.log(1.0 - 2.0 ** (-5.0 - h))


def _retention_kernel(q_ref, k_ref, v_ref, g_ref, gn_ref, o_ref, state_ref, dmat_ref, qdec_ref,
                      kdec_ref, *, tb, dk, dv, n_heads):
    t = pl.program_id(1)

    @pl.when((pl.program_id(0) == 0) & (t == 0))
    def _():
        r = lax.broadcasted_iota(jnp.int32, (tb, tb), 0)
        c = lax.broadcasted_iota(jnp.int32, (tb, tb), 1)
        dist = jnp.abs(r - c).astype(F32)
        visible = (c // CHUNK) <= (r // CHUNK)
        pos = lax.broadcasted_iota(jnp.int32, (tb, dk), 0).astype(F32)
        for h in range(n_heads):
            lg = _ret_log_gamma(h)
            dmat_ref[h] = jnp.where(visible, jnp.exp(lg * dist), 0.0)
            qdec_ref[h] = jnp.exp(lg * pos)
            kdec_ref[h] = jnp.exp(lg * (tb - pos))

    @pl.when(t == 0)
    def _():
        state_ref[...] = jnp.zeros_like(state_ref)

    for h in range(n_heads):
        qh = q_ref[:, h * dk:(h + 1) * dk]
        kh = k_ref[:, h * dk:(h + 1) * dk]
        vh = v_ref[:, h * dv:(h + 1) * dv]
        inner = (_dot_nt(qh, kh) * dmat_ref[h]).astype(BF16)
        qd = (qh.astype(F32) * qdec_ref[h]).astype(BF16)
        kd = (kh.astype(F32) * kdec_ref[h]).astype(BF16)
        state = state_ref[h]
        o = _dot(inner, vh) + _dot(qd, state.astype(BF16))
        state_ref[h] = state * math.exp(_ret_log_gamma(h) * tb) + _dot_tn(kd, vh)
        mu = jnp.mean(o, axis=-1, keepdims=True)
        d = o - mu
        y = d * lax.rsqrt(jnp.mean(d * d, axis=-1, keepdims=True) + EPS) * gn_ref[...]
        gate = g_ref[:, h * dv:(h + 1) * dv].astype(F32)
        o_ref[:, h * dv:(h + 1) * dv] = (jax.nn.silu(gate) * y).astype(o_ref.dtype)


def _retention(proj, gn_g, *, batch, seq):
    M, W = proj.shape
    H = RET_HEADS
    dk = W // (6 * H)
    dv = 2 * dk
    tb = min(256, seq)
    nt = seq // tb
    row = lambda b, t: b * nt + t
    return pl.pallas_call(
        functools.partial(_retention_kernel, tb=tb, dk=dk, dv=dv, n_heads=H),
        out_shape=jax.ShapeDtypeStruct((M, H * dv), BF16),
        grid=(batch, nt),
        in_specs=[
            pl.BlockSpec((tb, H * dk), lambda b, t: (row(b, t), 0)),
            pl.BlockSpec((tb, H * dk), lambda b, t: (row(b, t), 1)),
            pl.BlockSpec((tb, H * dv), lambda b, t: (row(b, t), 1)),
            pl.BlockSpec((tb, H * dv), lambda b, t: (row(b, t), 2)),
            pl.BlockSpec((1, dv), lambda b, t: (0, 0)),
        ],
        out_specs=pl.BlockSpec((tb, H * dv), lambda b, t: (row(b, t), 0)),
        scratch_shapes=[pltpu.VMEM((H, dk, dv), F32), pltpu.VMEM((H, tb, tb), F32),
                        pltpu.VMEM((H, tb, dk), F32), pltpu.VMEM((H, tb, dk), F32)],
        compiler_params=_params("arbitrary", "arbitrary"),
        name="retention",
    )(proj, proj, proj, proj, gn_g.reshape(1, dv))


def kernel(x, c, ada_w, ada_b, norm_g, ffn_w_in, ffn_w_out, da_w_qkv, da_lambda, da_subln_g,
           da_w_o, ret_w_qkvg, ret_gn_g, ret_w_o, final_g):
    B, S, D = x.shape
    depth = ada_w.shape[0]
    mods = _ada_mods(c, ada_w, ada_b)

    da_hd = D // (2 * DA_HEADS)
    da_scale = jnp.concatenate([jnp.full((2 * DA_HEADS * da_hd,), da_hd ** -0.5, F32),
                                jnp.ones((da_w_qkv.shape[2] - 2 * DA_HEADS * da_hd,), F32)])
    ret_dk = D // RET_HEADS
    ret_scale = jnp.concatenate([jnp.ones((RET_HEADS * ret_dk,), F32),
                                 jnp.full((RET_HEADS * ret_dk,), ret_dk ** -0.5, F32),
                                 jnp.ones((ret_w_qkvg.shape[2] - 2 * RET_HEADS * ret_dk,), F32)])
    w_in = ffn_w_in.astype(BF16)
    w_out = ffn_w_out.astype(BF16)
    w_qkv = (da_w_qkv * da_scale).astype(BF16)
    w_da_o = da_w_o.astype(BF16)
    w_ret = (ret_w_qkvg * ret_scale).astype(BF16)
    w_ret_o = ret_w_o.astype(BF16)

    xf = x.reshape(B * S, D)
    for i in range(depth):
        m = mods[i]
        xf = _ffn(xf, m, 0, norm_g[i, 0], w_in[i, 0], w_out[i, 0], seq=S)
        j = i // N_MIXERS
        if i % N_MIXERS == 0:
            lambda_init = 0.8 - 0.6 * math.exp(-0.3 * i)
            qkv = _norm_mod_matmul(xf, m, 3, norm_g[i, 1], w_qkv[j], seq=S)
            heads = _diff_attention(qkv, da_lambda[j], da_subln_g[j], batch=B, seq=S,
                                    lambda_init=lambda_init)
            xf = _matmul_residual(heads, w_da_o[j], xf, m, 5, seq=S)
        else:
            proj = _norm_mod_matmul(xf, m, 3, norm_g[i, 1], w_ret[j], seq=S)
            heads = _retention(proj, ret_gn_g[j], batch=B, seq=S)
            xf = _matmul_residual(heads, w_ret_o[j], xf, m, 5, seq=S)
        xf = _ffn(xf, m, 6, norm_g[i, 2], w_in[i, 1], w_out[i, 1],
                  final_g=final_g if i == depth - 1 else None, seq=S)
    return xf.reshape(B, S, D)
```

```python
import functools
import math

import jax
import jax.numpy as jnp
from jax import lax
from jax.experimental import pallas as pl
from jax.experimental.pallas import tpu as pltpu

F32 = jnp.float32
BF16 = jnp.bfloat16

EPS = 1e-5
CHUNK = 64
N_MOD = 9
N_MIXERS = 2
DA_HEADS = 8
RET_HEADS = 8

V7X_VMEM_BYTES = 64 * 1024 * 1024
VMEM_LIMIT_BYTES = V7X_VMEM_BYTES - 8 * 1024 * 1024


def _params(*semantics):
    return pltpu.CompilerParams(dimension_semantics=semantics, vmem_limit_bytes=VMEM_LIMIT_BYTES)


def _dot(a, b):
    return jnp.dot(a, b, preferred_element_type=F32)


def _dot_nt(a, b):
    return lax.dot_general(a, b, (((1,), (1,)), ((), ())), preferred_element_type=F32)


def _dot_tn(a, b):
    return lax.dot_general(a, b, (((0,), (0,)), ((), ())), preferred_element_type=F32)


def _rms(x, g):
    return x * lax.rsqrt(jnp.mean(x * x, axis=-1, keepdims=True) + EPS) * g


def _norm_mod(x, g, shift, scale):
    return _rms(x, g) * (1.0 + scale) + shift


def _ada_kernel(c_ref, w_ref, b_ref, o_ref):
    cs = jax.nn.silu(c_ref[...]).astype(BF16)
    o_ref[...] = _dot(cs, w_ref[...].astype(BF16)) + b_ref[...]


def _ada_mods(c, ada_w, ada_b):
    B, D = c.shape
    L, _, W = ada_w.shape
    rows = 8
    tn = min(1024, W)
    c_pad = jnp.zeros((rows, D), F32).at[:B].set(c)
    out = pl.pallas_call(
        _ada_kernel,
        out_shape=jax.ShapeDtypeStruct((L, rows, W), F32),
        grid=(L, W // tn),
        in_specs=[
            pl.BlockSpec((rows, D), lambda l, n: (0, 0)),
            pl.BlockSpec((None, D, tn), lambda l, n: (l, 0, n)),
            pl.BlockSpec((None, 1, tn), lambda l, n: (l, 0, n)),
        ],
        out_specs=pl.BlockSpec((None, rows, tn), lambda l, n: (l, 0, n)),
        compiler_params=_params("parallel", "parallel"),
        name="ada_mods",
    )(c_pad, ada_w, ada_b.reshape(L, 1, W))
    return out[:, :B].reshape(L, B, N_MOD, D).transpose(0, 2, 1, 3)[:, :, :, None, :]


def _ffn_kernel(x_ref, sh_ref, sc_ref, gt_ref, ng_ref, wg_ref, wu_ref, wo_ref, *rest, final):
    if final:
        fg_ref, o_ref, h_ref, acc_ref = rest
    else:
        o_ref, h_ref, acc_ref = rest
    j = pl.program_id(1)

    @pl.when(j == 0)
    def _():
        h_ref[...] = _norm_mod(x_ref[...], ng_ref[...], sh_ref[...], sc_ref[...]).astype(BF16)
        acc_ref[...] = jnp.zeros_like(acc_ref)

    h = h_ref[...]
    a = _dot(h, wg_ref[...])
    b = _dot(h, wu_ref[...])
    act = (jax.nn.silu(a) * b).astype(BF16)
    acc_ref[...] += _dot(act, wo_ref[...])

    @pl.when(j == pl.num_programs(1) - 1)
    def _():
        y = x_ref[...] + 0.5 * gt_ref[...] * acc_ref[...]
        if final:
            y = _rms(y, fg_ref[...])
        o_ref[...] = y


def _ffn(x, mods, k0, norm_g, w_in, w_out, final_g=None, *, seq):
    M, D = x.shape
    F = w_out.shape[0]
    tm = min(512, seq)
    tf = min(512, F)
    nf = F // tf
    bpb = seq // tm
    vec = lambda k: pl.BlockSpec((None, None, 1, D), lambda i, j: (k, i // bpb, 0, 0))
    in_specs = [
        pl.BlockSpec((tm, D), lambda i, j: (i, 0)),
        vec(k0), vec(k0 + 1), vec(k0 + 2),
        pl.BlockSpec((1, D), lambda i, j: (0, 0)),
        pl.BlockSpec((D, tf), lambda i, j: (0, j)),
        pl.BlockSpec((D, tf), lambda i, j: (0, nf + j)),
        pl.BlockSpec((tf, D), lambda i, j: (j, 0)),
    ]
    args = [x, mods, mods, mods, norm_g.reshape(1, D), w_in, w_in, w_out]
    if final_g is not None:
        in_specs.append(pl.BlockSpec((1, D), lambda i, j: (0, 0)))
        args.append(final_g.reshape(1, D))
    return pl.pallas_call(
        functools.partial(_ffn_kernel, final=final_g is not None),
        out_shape=jax.ShapeDtypeStruct((M, D), F32),
        grid=(M // tm, nf),
        in_specs=in_specs,
        out_specs=pl.BlockSpec((tm, D), lambda i, j: (i, 0)),
        scratch_shapes=[pltpu.VMEM((tm, D), BF16), pltpu.VMEM((tm, D), F32)],
        compiler_params=_params("parallel", "arbitrary"),
        name="ffn",
    )(*args)


def _nm_matmul_kernel(x_ref, sh_ref, sc_ref, ng_ref, w_ref, o_ref, h_ref):
    @pl.when(pl.program_id(1) == 0)
    def _():
        h_ref[...] = _norm_mod(x_ref[...], ng_ref[...], sh_ref[...], sc_ref[...]).astype(BF16)

    o_ref[...] = _dot(h_ref[...], w_ref[...]).astype(o_ref.dtype)


def _norm_mod_matmul(x, mods, k0, norm_g, w, *, seq):
    M, D = x.shape
    N = w.shape[1]
    tm = min(1024, seq)
    tn = min(1024, N)
    bpb = seq // tm
    vec = lambda k: pl.BlockSpec((None, None, 1, D), lambda i, n: (k, i // bpb, 0, 0))
    return pl.pallas_call(
        _nm_matmul_kernel,
        out_shape=jax.ShapeDtypeStruct((M, N), BF16),
        grid=(M // tm, N // tn),
        in_specs=[
            pl.BlockSpec((tm, D), lambda i, n: (i, 0)),
            vec(k0), vec(k0 + 1),
            pl.BlockSpec((1, D), lambda i, n: (0, 0)),
            pl.BlockSpec((D, tn), lambda i, n: (0, n)),
        ],
        out_specs=pl.BlockSpec((tm, tn), lambda i, n: (i, n)),
        scratch_shapes=[pltpu.VMEM((tm, D), BF16)],
        compiler_params=_params("parallel", "arbitrary"),
        name="norm_mod_matmul",
    )(x, mods, mods, norm_g.reshape(1, D), w)


def _mm_res_kernel(a_ref, w_ref, x_ref, gt_ref, o_ref):
    o_ref[...] = x_ref[...] + gt_ref[...] * _dot(a_ref[...], w_ref[...])


def _matmul_residual(a, w, x, mods, k, *, seq):
    M, K = a.shape
    N = w.shape[1]
    tm = min(1024, seq)
    tn = min(512, N)
    bpb = seq // tm
    return pl.pallas_call(
        _mm_res_kernel,
        out_shape=jax.ShapeDtypeStruct((M, N), F32),
        grid=(M // tm, N // tn),
        in_specs=[
            pl.BlockSpec((tm, K), lambda i, n: (i, 0)),
            pl.BlockSpec((K, tn), lambda i, n: (0, n)),
            pl.BlockSpec((tm, tn), lambda i, n: (i, n)),
            pl.BlockSpec((None, None, 1, tn), lambda i, n: (k, i // bpb, 0, n)),
        ],
        out_specs=pl.BlockSpec((tm, tn), lambda i, n: (i, n)),
        compiler_params=_params("parallel", "parallel"),
        name="matmul_residual",
    )(a, w, x, mods)


def _da_attn_kernel(q_ref, k_ref, v_ref, lam_ref, g_ref, o_ref, vt_ref, base_ref, m_ref, l_ref,
                    acc_ref, *, t, hd, n_heads, lambda_init):
    head = pl.program_id(1)
    qi = pl.program_id(2)
    nblk = vt_ref.shape[0]
    hv = jnp.full((1, 1), head + 1, jnp.int32).astype(F32)
    slope = jnp.exp2(hv * (-8.0 / n_heads))
    r = lax.broadcasted_iota(jnp.int32, (t, t), 0)
    c = lax.broadcasted_iota(jnp.int32, (t, t), 1)

    @pl.when(qi == 0)
    def _():
        for b in range(nblk):
            vt_ref[b] = v_ref[b * t:(b + 1) * t, :].astype(F32).T.astype(BF16)
        base_ref[...] = slope * r.astype(F32)

    qs = (q_ref[:, 0:hd], q_ref[:, hd:2 * hd])

    bias_diag = slope * (c - jnp.abs(c - r)).astype(F32)
    allowed = (r // CHUNK) <= (c // CHUNK)
    k0 = pl.multiple_of(qi * t, t)
    vt_d = vt_ref[qi]
    for n in range(2):
        k_d = k_ref[pl.ds(k0, t), n * hd:(n + 1) * hd]
        s = jnp.where(allowed, _dot_nt(k_d, qs[n]) + bias_diag, -jnp.inf)
        m = jnp.max(s, axis=0, keepdims=True)
        p = jnp.exp(s - m)
        m_ref[n] = m
        l_ref[n] = jnp.sum(p, axis=0, keepdims=True)
        acc_ref[n] = _dot(vt_d, p.astype(BF16))

    def body(kb, carry):
        kk = pl.multiple_of(kb * t, t)
        off = slope * ((kb - qi) * t).astype(F32)
        vt_b = vt_ref[kb]
        for n in range(2):
            k_b = k_ref[pl.ds(kk, t), n * hd:(n + 1) * hd]
            s = _dot_nt(k_b, qs[n]) + base_ref[...]
            m_old = m_ref[n]
            m_new = jnp.maximum(m_old, jnp.max(s, axis=0, keepdims=True) + off)
            alpha = jnp.exp(m_old - m_new)
            p = jnp.exp(s - (m_new - off))
            m_ref[n] = m_new
            l_ref[n] = alpha * l_ref[n] + jnp.sum(p, axis=0, keepdims=True)
            acc_ref[n] = alpha * acc_ref[n] + _dot(vt_b, p.astype(BF16))
        return carry

    lax.fori_loop(0, qi, body, 0)

    lam = lam_ref[...]
    lam_val = (jnp.exp(jnp.sum(lam[0:1] * lam[1:2], axis=-1, keepdims=True))
               - jnp.exp(jnp.sum(lam[2:3] * lam[3:4], axis=-1, keepdims=True)) + lambda_init)
    o_t = acc_ref[0] / l_ref[0] - lam_val * (acc_ref[1] / l_ref[1])
    o_t = o_t * lax.rsqrt(jnp.mean(o_t * o_t, axis=0, keepdims=True) + EPS)
    o_ref[...] = (o_t.T * (g_ref[...] * (1.0 - lambda_init))).astype(o_ref.dtype)


def _diff_attention(qkv, lam, subln_g, *, batch, seq, lambda_init):
    M = qkv.shape[0]
    H = DA_HEADS
    hd = qkv.shape[1] // (6 * H)
    vd = 2 * hd
    t = min(512, seq)
    nq = seq // t
    return pl.pallas_call(
        functools.partial(_da_attn_kernel, t=t, hd=hd, n_heads=H, lambda_init=lambda_init),
        out_shape=jax.ShapeDtypeStruct((M, H * vd), BF16),
        grid=(batch, H, nq),
        in_specs=[
            pl.BlockSpec((t, vd), lambda b, h, i: (b * nq + i, h)),
            pl.BlockSpec((seq, vd), lambda b, h, i: (b, H + h)),
            pl.BlockSpec((seq, vd), lambda b, h, i: (b, 2 * H + h)),
            pl.BlockSpec((4, hd), lambda b, h, i: (0, 0)),
            pl.BlockSpec((1, vd), lambda b, h, i: (0, 0)),
        ],
        out_specs=pl.BlockSpec((t, vd), lambda b, h, i: (b * nq + i, h)),
        scratch_shapes=[pltpu.VMEM((nq, vd, t), BF16), pltpu.VMEM((t, t), F32),
                        pltpu.VMEM((2, 1, t), F32), pltpu.VMEM((2, 1, t), F32),
                        pltpu.VMEM((2, vd, t), F32)],
        compiler_params=_params("parallel", "parallel", "arbitrary"),
        name="diff_attention",
    )(qkv, qkv, qkv, lam, subln_g.reshape(1, vd))


def _ret_log_gamma(h):
    return math.log(1.0 - 2.0 ** (-5.0 - h))


def _retention_kernel(q_ref, k_ref, v_ref, g_ref, gn_ref, o_ref, state_ref, dmat_ref, qdec_ref,
                      kdec_ref, *, tb, dk, dv, n_heads):
    t = pl.program_id(1)

    @pl.when((pl.program_id(0) == 0) & (t == 0))
    def _():
        r = lax.broadcasted_iota(jnp.int32, (tb, tb), 0)
        c = lax.broadcasted_iota(jnp.int32, (tb, tb), 1)
        dist = jnp.abs(r - c).astype(F32)
        visible = (c // CHUNK) <= (r // CHUNK)
        pos = lax.broadcasted_iota(jnp.int32, (tb, dk), 0).astype(F32)
        for h in range(n_heads):
            lg = _ret_log_gamma(h)
            dmat_ref[h] = jnp.where(visible, jnp.exp(lg * dist), 0.0)
            qdec_ref[h] = jnp.exp(lg * pos)
            kdec_ref[h] = jnp.exp(lg * (tb - pos))

    @pl.when(t == 0)
    def _():
        state_ref[...] = jnp.zeros_like(state_ref)

    for h in range(n_heads):
        qh = q_ref[:, h * dk:(h + 1) * dk]
        kh = k_ref[:, h * dk:(h + 1) * dk]
        vh = v_ref[:, h * dv:(h + 1) * dv]
        inner = (_dot_nt(qh, kh) * dmat_ref[h]).astype(BF16)
        qd = (qh.astype(F32) * qdec_ref[h]).astype(BF16)
        kd = (kh.astype(F32) * kdec_ref[h]).astype(BF16)
        state = state_ref[h]
        o = _dot(inner, vh) + _dot(qd, state.astype(BF16))
        state_ref[h] = state * math.exp(_ret_log_gamma(h) * tb) + _dot_tn(kd, vh)
        mu = jnp.mean(o, axis=-1, keepdims=True)
        d = o - mu
        y = d * lax.rsqrt(jnp.mean(d * d, axis=-1, keepdims=True) + EPS) * gn_ref[...]
        gate = g_ref[:, h * dv:(h + 1) * dv].astype(F32)
        o_ref[:, h * dv:(h + 1) * dv] = (jax.nn.silu(gate) * y).astype(o_ref.dtype)


def _retention(proj, gn_g, *, batch, seq):
    M, W = proj.shape
    H = RET_HEADS
    dk = W // (6 * H)
    dv = 2 * dk
    tb = min(256, seq)
    nt = seq // tb
    row = lambda b, t: b * nt + t
    return pl.pallas_call(
        functools.partial(_retention_kernel, tb=tb, dk=dk, dv=dv, n_heads=H),
        out_shape=jax.ShapeDtypeStruct((M, H * dv), BF16),
        grid=(batch, nt),
        in_specs=[
            pl.BlockSpec((tb, H * dk), lambda b, t: (row(b, t), 0)),
            pl.BlockSpec((tb, H * dk), lambda b, t: (row(b, t), 1)),
            pl.BlockSpec((tb, H * dv), lambda b, t: (row(b, t), 1)),
            pl.BlockSpec((tb, H * dv), lambda b, t: (row(b, t), 2)),
            pl.BlockSpec((1, dv), lambda b, t: (0, 0)),
        ],
        out_specs=pl.BlockSpec((tb, H * dv), lambda b, t: (row(b, t), 0)),
        scratch_shapes=[pltpu.VMEM((H, dk, dv), F32), pltpu.VMEM((H, tb, tb), F32),
                        pltpu.VMEM((H, tb, dk), F32), pltpu.VMEM((H, tb, dk), F32)],
        compiler_params=_params("arbitrary", "arbitrary"),
        name="retention",
    )(proj, proj, proj, proj, gn_g.reshape(1, dv))


def kernel(x, c, ada_w, ada_b, norm_g, ffn_w_in, ffn_w_out, da_w_qkv, da_lambda, da_subln_g,
           da_w_o, ret_w_qkvg, ret_gn_g, ret_w_o, final_g):
    B, S, D = x.shape
    depth = ada_w.shape[0]
    mods = _ada_mods(c, ada_w, ada_b)

    da_hd = D // (2 * DA_HEADS)
    da_scale = jnp.concatenate([jnp.full((2 * DA_HEADS * da_hd,), da_hd ** -0.5, F32),
                                jnp.ones((da_w_qkv.shape[2] - 2 * DA_HEADS * da_hd,), F32)])
    ret_dk = D // RET_HEADS
    ret_scale = jnp.concatenate([jnp.ones((RET_HEADS * ret_dk,), F32),
                                 jnp.full((RET_HEADS * ret_dk,), ret_dk ** -0.5, F32),
                                 jnp.ones((ret_w_qkvg.shape[2] - 2 * RET_HEADS * ret_dk,), F32)])
    w_in = ffn_w_in.astype(BF16)
    w_out = ffn_w_out.astype(BF16)
    w_qkv = (da_w_qkv * da_scale).astype(BF16)
    w_da_o = da_w_o.astype(BF16)
    w_ret = (ret_w_qkvg * ret_scale).astype(BF16)
    w_ret_o = ret_w_o.astype(BF16)

    xf = x.reshape(B * S, D)
    for i in range(depth):
        m = mods[i]
        xf = _ffn(xf, m, 0, norm_g[i, 0], w_in[i, 0], w_out[i, 0], seq=S)
        j = i // N_MIXERS
        if i % N_MIXERS == 0:
            lambda_init = 0.8 - 0.6 * math.exp(-0.3 * i)
            qkv = _norm_mod_matmul(xf, m, 3, norm_g[i, 1], w_qkv[j], seq=S)
            heads = _diff_attention(qkv, da_lambda[j], da_subln_g[j], batch=B, seq=S,
                                    lambda_init=lambda_init)
            xf = _matmul_residual(heads, w_da_o[j], xf, m, 5, seq=S)
        else:
            proj = _norm_mod_matmul(xf, m, 3, norm_g[i, 1], w_ret[j], seq=S)
            heads = _retention(proj, ret_gn_g[j], batch=B, seq=S)
            xf = _matmul_residual(heads, w_ret_o[j], xf, m, 5, seq=S)
        xf = _ffn(xf, m, 6, norm_g[i, 2], w_in[i, 1], w_out[i, 1],
                  final_g=final_g if i == depth - 1 else None, seq=S)
    return xf.reshape(B, S, D)
```

```python
import functools
import math

import jax
import jax.numpy as jnp
from jax import lax
from jax.experimental import pallas as pl
from jax.experimental.pallas import tpu as pltpu

F32 = jnp.float32
BF16 = jnp.bfloat16

EPS = 1e-5
LOG2E = math.log2(math.e)
CHUNK = 64
N_MOD = 9
N_MIXERS = 2
DA_HEADS = 8
RET_HEADS = 8

V7X_VMEM_BYTES = 64 * 1024 * 1024
VMEM_LIMIT_BYTES = V7X_VMEM_BYTES - 8 * 1024 * 1024


def _params(*semantics):
    return pltpu.CompilerParams(dimension_semantics=semantics, vmem_limit_bytes=VMEM_LIMIT_BYTES)


def _dot(a, b):
    return jnp.dot(a, b, preferred_element_type=F32)


def _dot_nt(a, b):
    return lax.dot_general(a, b, (((1,), (1,)), ((), ())), preferred_element_type=F32)


def _dot_tn(a, b):
    return lax.dot_general(a, b, (((0,), (0,)), ((), ())), preferred_element_type=F32)


def _rms(x, g):
    return x * lax.rsqrt(jnp.mean(x * x, axis=-1, keepdims=True) + EPS) * g


NORM_ROWS = 32


def _norm_mod_into(h_ref, x_ref, g_ref, shift_ref, scale_ref):
    gs = g_ref[...] * (1.0 + scale_ref[...])
    shift = shift_ref[...]

    def body(i, carry):
        r0 = pl.multiple_of(i * NORM_ROWS, NORM_ROWS)
        x = x_ref[pl.ds(r0, NORM_ROWS), :]
        rinv = lax.rsqrt(jnp.mean(x * x, axis=-1, keepdims=True) + EPS)
        h_ref[pl.ds(r0, NORM_ROWS), :] = (x * rinv * gs + shift).astype(BF16)
        return carry

    lax.fori_loop(0, x_ref.shape[0] // NORM_ROWS, body, 0, unroll=4)


def _ada_kernel(c_ref, w_ref, b_ref, o_ref):
    cs = jax.nn.silu(c_ref[...]).astype(BF16)
    o_ref[...] = _dot(cs, w_ref[...].astype(BF16)) + b_ref[...]


def _ada_mods(c, ada_w, ada_b):
    B, D = c.shape
    L, _, W = ada_w.shape
    rows = 8
    tn = min(1024, W)
    c_pad = jnp.zeros((rows, D), F32).at[:B].set(c)
    out = pl.pallas_call(
        _ada_kernel,
        out_shape=jax.ShapeDtypeStruct((L, rows, W), F32),
        grid=(L, W // tn),
        in_specs=[
            pl.BlockSpec((rows, D), lambda l, n: (0, 0)),
            pl.BlockSpec((None, D, tn), lambda l, n: (l, 0, n)),
            pl.BlockSpec((None, 1, tn), lambda l, n: (l, 0, n)),
        ],
        out_specs=pl.BlockSpec((None, rows, tn), lambda l, n: (l, 0, n)),
        compiler_params=_params("parallel", "parallel"),
        name="ada_mods",
    )(c_pad, ada_w, ada_b.reshape(L, 1, W))
    return out[:, :B].reshape(L, B, N_MOD, D).transpose(0, 2, 1, 3)[:, :, :, None, :]


def _ffn_kernel(x_ref, sh_ref, sc_ref, gt_ref, ng_ref, wg_ref, wu_ref, wo_ref, *rest, final):
    if final:
        fg_ref, o_ref, h_ref = rest
    else:
        o_ref, h_ref = rest
    j = pl.program_id(1)

    @pl.when(j == 0)
    def _():
        _norm_mod_into(h_ref, x_ref, ng_ref, sh_ref, sc_ref)
        o_ref[...] = jnp.zeros_like(o_ref)

    h = h_ref[...]
    a = _dot(h, wg_ref[...])
    b = _dot(h, wu_ref[...])
    act = (jax.nn.silu(a) * b).astype(BF16)
    o_ref[...] += _dot(act, wo_ref[...])

    @pl.when(j == pl.num_programs(1) - 1)
    def _():
        half_gate = 0.5 * gt_ref[...]

        def body(i, carry):
            r0 = pl.multiple_of(i * NORM_ROWS, NORM_ROWS)
            y = x_ref[pl.ds(r0, NORM_ROWS), :] + half_gate * o_ref[pl.ds(r0, NORM_ROWS), :]
            if final:
                y = _rms(y, fg_ref[...])
            o_ref[pl.ds(r0, NORM_ROWS), :] = y
            return carry

        lax.fori_loop(0, x_ref.shape[0] // NORM_ROWS, body, 0, unroll=4)


def _ffn(x, mods, k0, norm_g, w_in, w_out, final_g=None, *, seq):
    M, D = x.shape
    F = w_out.shape[0]
    tm = min(1024, seq)
    tf = min(512, F)
    nf = F // tf
    bpb = seq // tm
    vec = lambda k: pl.BlockSpec((None, None, 1, D), lambda i, j: (k, i // bpb, 0, 0))
    in_specs = [
        pl.BlockSpec((tm, D), lambda i, j: (i, 0)),
        vec(k0), vec(k0 + 1), vec(k0 + 2),
        pl.BlockSpec((1, D), lambda i, j: (0, 0)),
        pl.BlockSpec((D, tf), lambda i, j: (0, j)),
        pl.BlockSpec((D, tf), lambda i, j: (0, nf + j)),
        pl.BlockSpec((tf, D), lambda i, j: (j, 0)),
    ]
    args = [x, mods, mods, mods, norm_g.reshape(1, D), w_in, w_in, w_out]
    if final_g is not None:
        in_specs.append(pl.BlockSpec((1, D), lambda i, j: (0, 0)))
        args.append(final_g.reshape(1, D))
    return pl.pallas_call(
        functools.partial(_ffn_kernel, final=final_g is not None),
        out_shape=jax.ShapeDtypeStruct((M, D), F32),
        grid=(M // tm, nf),
        in_specs=in_specs,
        out_specs=pl.BlockSpec((tm, D), lambda i, j: (i, 0)),
        scratch_shapes=[pltpu.VMEM((tm, D), BF16)],
        compiler_params=_params("parallel", "arbitrary"),
        name="ffn",
    )(*args)


def _nm_matmul_kernel(x_ref, sh_ref, sc_ref, ng_ref, w_ref, o_ref, h_ref):
    @pl.when(pl.program_id(1) == 0)
    def _():
        _norm_mod_into(h_ref, x_ref, ng_ref, sh_ref, sc_ref)

    o_ref[...] = _dot(h_ref[...], w_ref[...]).astype(o_ref.dtype)


def _norm_mod_matmul(x, mods, k0, norm_g, w, *, seq):
    M, D = x.shape
    N = w.shape[1]
    tm = min(1024, seq)
    tn = min(1024, N)
    bpb = seq // tm
    vec = lambda k: pl.BlockSpec((None, None, 1, D), lambda i, n: (k, i // bpb, 0, 0))
    return pl.pallas_call(
        _nm_matmul_kernel,
        out_shape=jax.ShapeDtypeStruct((M, N), BF16),
        grid=(M // tm, N // tn),
        in_specs=[
            pl.BlockSpec((tm, D), lambda i, n: (i, 0)),
            vec(k0), vec(k0 + 1),
            pl.BlockSpec((1, D), lambda i, n: (0, 0)),
            pl.BlockSpec((D, tn), lambda i, n: (0, n)),
        ],
        out_specs=pl.BlockSpec((tm, tn), lambda i, n: (i, n)),
        scratch_shapes=[pltpu.VMEM((tm, D), BF16)],
        compiler_params=_params("parallel", "arbitrary"),
        name="norm_mod_matmul",
    )(x, mods, mods, norm_g.reshape(1, D), w)


def _mm_res_kernel(a_ref, w_ref, x_ref, gt_ref, o_ref):
    o_ref[...] = x_ref[...] + gt_ref[...] * _dot(a_ref[...], w_ref[...])


def _matmul_residual(a, w, x, mods, k, *, seq):
    M, K = a.shape
    N = w.shape[1]
    tm = min(1024, seq)
    tn = min(512, N)
    bpb = seq // tm
    return pl.pallas_call(
        _mm_res_kernel,
        out_shape=jax.ShapeDtypeStruct((M, N), F32),
        grid=(M // tm, N // tn),
        in_specs=[
            pl.BlockSpec((tm, K), lambda i, n: (i, 0)),
            pl.BlockSpec((K, tn), lambda i, n: (0, n)),
            pl.BlockSpec((tm, tn), lambda i, n: (i, n)),
            pl.BlockSpec((None, None, 1, tn), lambda i, n: (k, i // bpb, 0, n)),
        ],
        out_specs=pl.BlockSpec((tm, tn), lambda i, n: (i, n)),
        compiler_params=_params("parallel", "parallel"),
        name="matmul_residual",
    )(a, w, x, mods)


def _da_attn_kernel(q_ref, k_ref, v_ref, lam_ref, g_ref, o_ref, vt_ref, base_ref, sa_ref, sb_ref,
                    m_ref, l_ref, acc_ref, *, t, hd, n_heads, lambda_init):
    head = pl.program_id(1)
    qi = pl.program_id(2)
    nblk = vt_ref.shape[0]
    hv = jnp.full((1, 1), head + 1, jnp.int32).astype(F32)
    slope = jnp.exp2(hv * (-8.0 / n_heads)) * LOG2E
    r = lax.broadcasted_iota(jnp.int32, (t, t), 0)
    c = lax.broadcasted_iota(jnp.int32, (t, t), 1)

    @pl.when(qi == 0)
    def _():
        for b in range(nblk):
            vt_ref[b] = v_ref[b * t:(b + 1) * t, :].astype(F32).T.astype(BF16)
        base_ref[...] = slope * r.astype(F32)

    qs = (q_ref[:, 0:hd], q_ref[:, hd:2 * hd])

    def qk(kb, s_ref):
        kk = pl.multiple_of(kb * t, t)
        for n in range(2):
            s_ref[n] = _dot_nt(k_ref[pl.ds(kk, t), n * hd:(n + 1) * hd], qs[n]) + base_ref[...]

    def softmax_pv(kb, s_ref):
        off = slope * ((kb - qi) * t).astype(F32)
        vt_b = vt_ref[kb]
        for n in range(2):
            s = s_ref[n]
            m_old = m_ref[n]
            m_new = jnp.maximum(m_old, jnp.max(s, axis=0, keepdims=True) + off)
            alpha = jnp.exp2(m_old - m_new)
            p = jnp.exp2(s - (m_new - off))
            m_ref[n] = m_new
            l_ref[n] = alpha * l_ref[n] + jnp.sum(p, axis=0, keepdims=True)
            acc_ref[n] = alpha * acc_ref[n] + _dot(vt_b, p.astype(BF16))

    bias_diag = slope * (c - jnp.abs(c - r)).astype(F32)
    allowed = (r // CHUNK) <= (c // CHUNK)
    k0 = pl.multiple_of(qi * t, t)
    vt_d = vt_ref[qi]
    qk(0, sa_ref)
    for n in range(2):
        k_d = k_ref[pl.ds(k0, t), n * hd:(n + 1) * hd]
        s = jnp.where(allowed, _dot_nt(k_d, qs[n]) + bias_diag, -jnp.inf)
        m = jnp.max(s, axis=0, keepdims=True)
        p = jnp.exp2(s - m)
        m_ref[n] = m
        l_ref[n] = jnp.sum(p, axis=0, keepdims=True)
        acc_ref[n] = _dot(vt_d, p.astype(BF16))

    def body(pair, carry):
        kb = 2 * pair
        qk(kb + 1, sb_ref)
        softmax_pv(kb, sa_ref)
        qk(jnp.minimum(kb + 2, qi - 1), sa_ref)
        softmax_pv(kb + 1, sb_ref)
        return carry

    lax.fori_loop(0, qi // 2, body, 0)

    @pl.when(qi % 2 == 1)
    def _():
        softmax_pv(qi - 1, sa_ref)

    lam = lam_ref[...]
    lam_val = (jnp.exp(jnp.sum(lam[0:1] * lam[1:2], axis=-1, keepdims=True))
               - jnp.exp(jnp.sum(lam[2:3] * lam[3:4], axis=-1, keepdims=True)) + lambda_init)
    o_t = acc_ref[0] / l_ref[0] - lam_val * (acc_ref[1] / l_ref[1])
    o_t = o_t * lax.rsqrt(jnp.mean(o_t * o_t, axis=0, keepdims=True) + EPS)
    o_ref[...] = (o_t.T * (g_ref[...] * (1.0 - lambda_init))).astype(o_ref.dtype)


def _diff_attention(qkv, lam, subln_g, *, batch, seq, lambda_init):
    M = qkv.shape[0]
    H = DA_HEADS
    hd = qkv.shape[1] // (6 * H)
    vd = 2 * hd
    t = min(512, seq)
    nq = seq // t
    return pl.pallas_call(
        functools.partial(_da_attn_kernel, t=t, hd=hd, n_heads=H, lambda_init=lambda_init),
        out_shape=jax.ShapeDtypeStruct((M, H * vd), BF16),
        grid=(batch, H, nq),
        in_specs=[
            pl.BlockSpec((t, vd), lambda b, h, i: (b * nq + i, h)),
            pl.BlockSpec((seq, vd), lambda b, h, i: (b, H + h)),
            pl.BlockSpec((seq, vd), lambda b, h, i: (b, 2 * H + h)),
            pl.BlockSpec((4, hd), lambda b, h, i: (0, 0)),
            pl.BlockSpec((1, vd), lambda b, h, i: (0, 0)),
        ],
        out_specs=pl.BlockSpec((t, vd), lambda b, h, i: (b * nq + i, h)),
        scratch_shapes=[pltpu.VMEM((nq, vd, t), BF16), pltpu.VMEM((t, t), F32),
                        pltpu.VMEM((2, t, t), F32), pltpu.VMEM((2, t, t), F32),
                        pltpu.VMEM((2, 1, t), F32), pltpu.VMEM((2, 1, t), F32),
                        pltpu.VMEM((2, vd, t), F32)],
        compiler_params=_params("parallel", "parallel", "arbitrary"),
        name="diff_attention",
    )(qkv, qkv, qkv, lam, subln_g.reshape(1, vd))


def _ret_log_gamma(h):
    return math.log(1.0 - 2.0 ** (-5.0 - h))


def _retention_kernel(q_ref, k_ref, v_ref, g_ref, gn_ref, o_ref, state_ref, dmat_ref, qdec_ref,
                      kdec_ref, *, tb, dk, dv, n_heads):
    t = pl.program_id(1)

    @pl.when((pl.program_id(0) == 0) & (t == 0))
    def _():
        r = lax.broadcasted_iota(jnp.int32, (tb, tb), 0)
        c = lax.broadcasted_iota(jnp.int32, (tb, tb), 1)
        dist = jnp.abs(r - c).astype(F32)
        visible = (c // CHUNK) <= (r // CHUNK)
        pos = lax.broadcasted_iota(jnp.int32, (tb, dk), 0).astype(F32)
        for h in range(n_heads):
            lg = _ret_log_gamma(h)
            dmat_ref[h] = jnp.where(visible, jnp.exp(lg * dist), 0.0)
            qdec_ref[h] = jnp.exp(lg * pos)
            kdec_ref[h] = jnp.exp(lg * (tb - pos))

    @pl.when(t == 0)
    def _():
        state_ref[...] = jnp.zeros_like(state_ref)

    for h in range(n_heads):
        qh = q_ref[:, h * dk:(h + 1) * dk]
        kh = k_ref[:, h * dk:(h + 1) * dk]
        vh = v_ref[:, h * dv:(h + 1) * dv]
        inner = (_dot_nt(qh, kh) * dmat_ref[h]).astype(BF16)
        qd = (qh.astype(F32) * qdec_ref[h]).astype(BF16)
        kd = (kh.astype(F32) * kdec_ref[h]).astype(BF16)
        state = state_ref[h]
        o = _dot(inner, vh) + _dot(qd, state.astype(BF16))
        state_ref[h] = state * math.exp(_ret_log_gamma(h) * tb) + _dot_tn(kd, vh)
        mu = jnp.mean(o, axis=-1, keepdims=True)
        d = o - mu
        y = d * lax.rsqrt(jnp.mean(d * d, axis=-1, keepdims=True) + EPS) * gn_ref[...]
        gate = g_ref[:, h * dv:(h + 1) * dv].astype(F32)
        o_ref[:, h * dv:(h + 1) * dv] = (jax.nn.silu(gate) * y).astype(o_ref.dtype)


def _retention(proj, gn_g, *, batch, seq):
    M, W = proj.shape
    H = RET_HEADS
    dk = W // (6 * H)
    dv = 2 * dk
    tb = min(256, seq)
    nt = seq // tb
    row = lambda b, t: b * nt + t
    return pl.pallas_call(
        functools.partial(_retention_kernel, tb=tb, dk=dk, dv=dv, n_heads=H),
        out_shape=jax.ShapeDtypeStruct((M, H * dv), BF16),
        grid=(batch, nt),
        in_specs=[
            pl.BlockSpec((tb, H * dk), lambda b, t: (row(b, t), 0)),
            pl.BlockSpec((tb, H * dk), lambda b, t: (row(b, t), 1)),
            pl.BlockSpec((tb, H * dv), lambda b, t: (row(b, t), 1)),
            pl.BlockSpec((tb, H * dv), lambda b, t: (row(b, t), 2)),
            pl.BlockSpec((1, dv), lambda b, t: (0, 0)),
        ],
        out_specs=pl.BlockSpec((tb, H * dv), lambda b, t: (row(b, t), 0)),
        scratch_shapes=[pltpu.VMEM((H, dk, dv), F32), pltpu.VMEM((H, tb, tb), F32),
                        pltpu.VMEM((H, tb, dk), F32), pltpu.VMEM((H, tb, dk), F32)],
        compiler_params=_params("arbitrary", "arbitrary"),
        name="retention",
    )(proj, proj, proj, proj, gn_g.reshape(1, dv))


def kernel(x, c, ada_w, ada_b, norm_g, ffn_w_in, ffn_w_out, da_w_qkv, da_lambda, da_subln_g,
           da_w_o, ret_w_qkvg, ret_gn_g, ret_w_o, final_g):
    B, S, D = x.shape
    depth = ada_w.shape[0]
    mods = _ada_mods(c, ada_w, ada_b)

    da_hd = D // (2 * DA_HEADS)
    da_scale = jnp.concatenate([jnp.full((2 * DA_HEADS * da_hd,), da_hd ** -0.5 * LOG2E, F32),
                                jnp.ones((da_w_qkv.shape[2] - 2 * DA_HEADS * da_hd,), F32)])
    ret_dk = D // RET_HEADS
    ret_scale = jnp.concatenate([jnp.ones((RET_HEADS * ret_dk,), F32),
                                 jnp.full((RET_HEADS * ret_dk,), ret_dk ** -0.5, F32),
                                 jnp.ones((ret_w_qkvg.shape[2] - 2 * RET_HEADS * ret_dk,), F32)])
    w_in = ffn_w_in.astype(BF16)
    w_out = ffn_w_out.astype(BF16)
    w_qkv = (da_w_qkv * da_scale).astype(BF16)
    w_da_o = da_w_o.astype(BF16)
    w_ret = (ret_w_qkvg * ret_scale).astype(BF16)
    w_ret_o = ret_w_o.astype(BF16)

    xf = x.reshape(B * S, D)
    for i in range(depth):
        m = mods[i]
        xf = _ffn(xf, m, 0, norm_g[i, 0], w_in[i, 0], w_out[i, 0], seq=S)
        j = i // N_MIXERS
        if i % N_MIXERS == 0:
            lambda_init = 0.8 - 0.6 * math.exp(-0.3 * i)
            qkv = _norm_mod_matmul(xf, m, 3, norm_g[i, 1], w_qkv[j], seq=S)
            heads = _diff_attention(qkv, da_lambda[j], da_subln_g[j], batch=B, seq=S,
                                    lambda_init=lambda_init)
            xf = _matmul_residual(heads, w_da_o[j], xf, m, 5, seq=S)
        else:
            proj = _norm_mod_matmul(xf, m, 3, norm_g[i, 1], w_ret[j], seq=S)
            heads = _retention(proj, ret_gn_g[j], batch=B, seq=S)
            xf = _matmul_residual(heads, w_ret_o[j], xf, m, 5, seq=S)
        xf = _ffn(xf, m, 6, norm_g[i, 2], w_in[i, 1], w_out[i, 1],
                  final_g=final_g if i == depth - 1 else None, seq=S)
    return xf.reshape(B, S, D)
```

```python
import functools
import math

import jax
import jax.numpy as jnp
from jax import lax
from jax.experimental import pallas as pl
from jax.experimental.pallas import tpu as pltpu

F32 = jnp.float32
BF16 = jnp.bfloat16

EPS = 1e-5
LOG2E = math.log2(math.e)
CHUNK = 64
N_MOD = 9
N_MIXERS = 2
DA_HEADS = 8
RET_HEADS = 8

BF16_SUBLANES = 16
V7X_VMEM_BYTES = 64 * 1024 * 1024
VMEM_LIMIT_BYTES = V7X_VMEM_BYTES - 8 * 1024 * 1024


def _params(*semantics):
    return pltpu.CompilerParams(dimension_semantics=semantics, vmem_limit_bytes=VMEM_LIMIT_BYTES)


def _dot(a, b):
    return jnp.dot(a, b, preferred_element_type=F32)


def _dot_nt(a, b):
    return lax.dot_general(a, b, (((1,), (1,)), ((), ())), preferred_element_type=F32)


def _dot_tn(a, b):
    return lax.dot_general(a, b, (((0,), (0,)), ((), ())), preferred_element_type=F32)


def _rms(x, g):
    return x * lax.rsqrt(jnp.mean(x * x, axis=-1, keepdims=True) + EPS) * g


NORM_ROWS = 32


def _norm_mod_into(h_ref, x_ref, g_ref, shift_ref, scale_ref):
    gs = g_ref[...] * (1.0 + scale_ref[...])
    shift = shift_ref[...]

    def body(i, carry):
        r0 = pl.multiple_of(i * NORM_ROWS, NORM_ROWS)
        x = x_ref[pl.ds(r0, NORM_ROWS), :]
        rinv = lax.rsqrt(jnp.mean(x * x, axis=-1, keepdims=True) + EPS)
        h_ref[pl.ds(r0, NORM_ROWS), :] = (x * rinv * gs + shift).astype(BF16)
        return carry

    lax.fori_loop(0, x_ref.shape[0] // NORM_ROWS, body, 0, unroll=4)


def _ada_kernel(c_ref, w_ref, b_ref, o_ref):
    cs = jax.nn.silu(c_ref[...]).astype(BF16)
    o_ref[...] = _dot(cs, w_ref[...].astype(BF16)) + b_ref[...]


def _ada_mods(c, ada_w, ada_b):
    B, D = c.shape
    L, _, W = ada_w.shape
    rows = 8
    tn = min(1024, W)
    c_pad = jnp.zeros((rows, D), F32).at[:B].set(c)
    out = pl.pallas_call(
        _ada_kernel,
        out_shape=jax.ShapeDtypeStruct((L, rows, W), F32),
        grid=(L, W // tn),
        in_specs=[
            pl.BlockSpec((rows, D), lambda l, n: (0, 0)),
            pl.BlockSpec((None, D, tn), lambda l, n: (l, 0, n)),
            pl.BlockSpec((None, 1, tn), lambda l, n: (l, 0, n)),
        ],
        out_specs=pl.BlockSpec((None, rows, tn), lambda l, n: (l, 0, n)),
        compiler_params=_params("parallel", "parallel"),
        name="ada_mods",
    )(c_pad, ada_w, ada_b.reshape(L, 1, W))
    return out[:, :B].reshape(L, B, N_MOD, D).transpose(0, 2, 1, 3)[:, :, :, None, :]


def _ffn_kernel(x_ref, sh_ref, sc_ref, gt_ref, ng_ref, wg_ref, wu_ref, wo_ref, *rest, final):
    if final:
        fg_ref, o_ref, h_ref = rest
    else:
        o_ref, h_ref = rest
    j = pl.program_id(1)

    @pl.when(j == 0)
    def _():
        _norm_mod_into(h_ref, x_ref, ng_ref, sh_ref, sc_ref)
        o_ref[...] = jnp.zeros_like(o_ref)

    h = h_ref[...]
    a = _dot(h, wg_ref[...])
    b = _dot(h, wu_ref[...])
    act = (jax.nn.silu(a) * b).astype(BF16)
    o_ref[...] += _dot(act, wo_ref[...])

    @pl.when(j == pl.num_programs(1) - 1)
    def _():
        half_gate = 0.5 * gt_ref[...]

        def body(i, carry):
            r0 = pl.multiple_of(i * NORM_ROWS, NORM_ROWS)
            y = x_ref[pl.ds(r0, NORM_ROWS), :] + half_gate * o_ref[pl.ds(r0, NORM_ROWS), :]
            if final:
                y = _rms(y, fg_ref[...])
            o_ref[pl.ds(r0, NORM_ROWS), :] = y
            return carry

        lax.fori_loop(0, x_ref.shape[0] // NORM_ROWS, body, 0, unroll=4)


def _ffn(x, mods, k0, norm_g, w_in, w_out, final_g=None, *, seq):
    M, D = x.shape
    F = w_out.shape[0]
    tm = min(1024, seq)
    tf = min(512, F)
    nf = F // tf
    bpb = seq // tm
    vec = lambda k: pl.BlockSpec((None, None, 1, D), lambda i, j: (k, i // bpb, 0, 0))
    in_specs = [
        pl.BlockSpec((tm, D), lambda i, j: (i, 0)),
        vec(k0), vec(k0 + 1), vec(k0 + 2),
        pl.BlockSpec((1, D), lambda i, j: (0, 0)),
        pl.BlockSpec((D, tf), lambda i, j: (0, j)),
        pl.BlockSpec((D, tf), lambda i, j: (0, nf + j)),
        pl.BlockSpec((tf, D), lambda i, j: (j, 0)),
    ]
    args = [x, mods, mods, mods, norm_g.reshape(1, D), w_in, w_in, w_out]
    if final_g is not None:
        in_specs.append(pl.BlockSpec((1, D), lambda i, j: (0, 0)))
        args.append(final_g.reshape(1, D))
    return pl.pallas_call(
        functools.partial(_ffn_kernel, final=final_g is not None),
        out_shape=jax.ShapeDtypeStruct((M, D), F32),
        grid=(M // tm, nf),
        in_specs=in_specs,
        out_specs=pl.BlockSpec((tm, D), lambda i, j: (i, 0)),
        scratch_shapes=[pltpu.VMEM((tm, D), BF16)],
        compiler_params=_params("parallel", "arbitrary"),
        name="ffn",
    )(*args)


def _nm_matmul_kernel(x_ref, sh_ref, sc_ref, ng_ref, w_ref, o_ref, h_ref):
    @pl.when(pl.program_id(1) == 0)
    def _():
        _norm_mod_into(h_ref, x_ref, ng_ref, sh_ref, sc_ref)

    o_ref[...] = _dot(h_ref[...], w_ref[...]).astype(o_ref.dtype)


def _norm_mod_matmul(x, mods, k0, norm_g, w, *, seq):
    M, D = x.shape
    N = w.shape[1]
    tm = min(1024, seq)
    tn = min(1024, N)
    bpb = seq // tm
    vec = lambda k: pl.BlockSpec((None, None, 1, D), lambda i, n: (k, i // bpb, 0, 0))
    return pl.pallas_call(
        _nm_matmul_kernel,
        out_shape=jax.ShapeDtypeStruct((M, N), BF16),
        grid=(M // tm, N // tn),
        in_specs=[
            pl.BlockSpec((tm, D), lambda i, n: (i, 0)),
            vec(k0), vec(k0 + 1),
            pl.BlockSpec((1, D), lambda i, n: (0, 0)),
            pl.BlockSpec((D, tn), lambda i, n: (0, n)),
        ],
        out_specs=pl.BlockSpec((tm, tn), lambda i, n: (i, n)),
        scratch_shapes=[pltpu.VMEM((tm, D), BF16)],
        compiler_params=_params("parallel", "arbitrary"),
        name="norm_mod_matmul",
    )(x, mods, mods, norm_g.reshape(1, D), w)


def _mm_res_kernel(a_ref, w_ref, x_ref, gt_ref, o_ref):
    o_ref[...] = x_ref[...] + gt_ref[...] * _dot(a_ref[...], w_ref[...])


def _matmul_residual(a, w, x, mods, k, *, seq):
    M, K = a.shape
    N = w.shape[1]
    tm = min(1024, seq)
    tn = min(512, N)
    bpb = seq // tm
    return pl.pallas_call(
        _mm_res_kernel,
        out_shape=jax.ShapeDtypeStruct((M, N), F32),
        grid=(M // tm, N // tn),
        in_specs=[
            pl.BlockSpec((tm, K), lambda i, n: (i, 0)),
            pl.BlockSpec((K, tn), lambda i, n: (0, n)),
            pl.BlockSpec((tm, tn), lambda i, n: (i, n)),
            pl.BlockSpec((None, None, 1, tn), lambda i, n: (k, i // bpb, 0, n)),
        ],
        out_specs=pl.BlockSpec((tm, tn), lambda i, n: (i, n)),
        compiler_params=_params("parallel", "parallel"),
        name="matmul_residual",
    )(a, w, x, mods)


def _da_attn_kernel(q_ref, k_ref, v_ref, lam_ref, g_ref, *rest, t, hd, n_heads, lambda_init,
                    cast_scaled):
    n_cast_in = len(cast_scaled) + sum(cast_scaled)
    cast_in = rest[:n_cast_in]
    o_ref = rest[n_cast_in]
    cast_out = rest[n_cast_in + 1:n_cast_in + 1 + len(cast_scaled)]
    vt_ref, bias_ref, sa_ref, sb_ref, m_ref, l_ref, acc_ref = rest[n_cast_in + 1 + len(cast_scaled):]

    head = pl.program_id(1)
    qi = pl.program_id(2)
    nblk = vt_ref.shape[0]
    hv = jnp.full((1, 1), head + 1, jnp.int32).astype(F32)
    slope = jnp.exp2(hv * (-8.0 / n_heads)) * LOG2E

    @pl.when(qi == 0)
    def _():
        for b in range(nblk):
            vt_ref[b] = v_ref[b * t:(b + 1) * t, :].astype(F32).T.astype(BF16)
        r = lax.broadcasted_iota(jnp.int32, (t, t), 0)
        c = lax.broadcasted_iota(jnp.int32, (t, t), 1)
        bias_ref[0] = slope * r.astype(F32)
        bias_ref[1] = jnp.where((r // CHUNK) <= (c // CHUNK),
                                slope * (c - jnp.abs(c - r)).astype(F32), -jnp.inf)

    qts = [q_ref[:, n * hd:(n + 1) * hd].astype(F32).T.astype(BF16) for n in range(2)]
    for n in range(2):
        m_ref[n] = jnp.full((1, t), -jnp.inf, F32)
        l_ref[n] = jnp.zeros((1, t), F32)
        acc_ref[n] = jnp.zeros(acc_ref.shape[1:], F32)

    def qk(kb, s_ref):
        kk = pl.multiple_of(kb * t, t)
        bias = bias_ref[(kb == qi).astype(jnp.int32)]
        for n in range(2):
            s_ref[n] = _dot(k_ref[pl.ds(kk, t), n * hd:(n + 1) * hd], qts[n]) + bias

    def softmax_pv(kb, s_ref):
        off = slope * ((kb - qi) * t).astype(F32)
        vt_b = vt_ref[kb]
        for n in range(2):
            s = s_ref[n]
            m_old = m_ref[n]
            m_new = jnp.maximum(m_old, jnp.max(s, axis=0, keepdims=True) + off)
            alpha = jnp.exp2(m_old - m_new)
            p = jnp.exp2(s - (m_new - off))
            m_ref[n] = m_new
            l_ref[n] = alpha * l_ref[n] + jnp.sum(p, axis=0, keepdims=True)
            acc_ref[n] = alpha * acc_ref[n] + _dot(vt_b, p.astype(BF16))

    qk(0, sa_ref)

    pos = 0
    for dst_ref, has_scale in zip(cast_out, cast_scaled):
        w = cast_in[pos][...]
        if has_scale:
            w = w * cast_in[pos + 1][...]
        dst_ref[...] = w.astype(BF16)
        pos += 2 if has_scale else 1

    def body(pair, carry):
        kb = 2 * pair
        qk(kb + 1, sb_ref)
        softmax_pv(kb, sa_ref)
        qk(jnp.minimum(kb + 2, qi), sa_ref)
        softmax_pv(kb + 1, sb_ref)
        return carry

    lax.fori_loop(0, (qi + 1) // 2, body, 0)

    @pl.when(qi % 2 == 0)
    def _():
        softmax_pv(qi, sa_ref)

    lam = lam_ref[...]
    lam_val = (jnp.exp(jnp.sum(lam[0:1] * lam[1:2], axis=-1, keepdims=True))
               - jnp.exp(jnp.sum(lam[2:3] * lam[3:4], axis=-1, keepdims=True)) + lambda_init)
    o_t = acc_ref[0] / l_ref[0] - lam_val * (acc_ref[1] / l_ref[1])
    o_t = o_t * lax.rsqrt(jnp.mean(o_t * o_t, axis=0, keepdims=True) + EPS)
    o_ref[...] = (o_t.T * (g_ref[...] * (1.0 - lambda_init))).astype(o_ref.dtype)


def _diff_attention(qkv, lam, subln_g, *, batch, seq, lambda_init, cast_jobs=()):
    M = qkv.shape[0]
    H = DA_HEADS
    hd = qkv.shape[1] // (6 * H)
    vd = 2 * hd
    t = min(512, seq)
    nq = seq // t
    n_steps = batch * H * nq
    in_specs = [
        pl.BlockSpec((t, vd), lambda b, h, i: (b * nq + i, h)),
        pl.BlockSpec((seq, vd), lambda b, h, i: (b, H + h)),
        pl.BlockSpec((seq, vd), lambda b, h, i: (b, 2 * H + h)),
        pl.BlockSpec((4, hd), lambda b, h, i: (0, 0)),
        pl.BlockSpec((1, vd), lambda b, h, i: (0, 0)),
    ]
    args = [qkv, qkv, qkv, lam, subln_g.reshape(1, vd)]
    out_shape = [jax.ShapeDtypeStruct((M, H * vd), BF16)]
    out_specs = [pl.BlockSpec((t, vd), lambda b, h, i: (b * nq + i, h))]
    scaled = []
    for arr, lead, col_scale in cast_jobs:
        rows, cols = arr.shape[-2:]
        chunk = BF16_SUBLANES * pl.cdiv(rows, BF16_SUBLANES * n_steps)
        assert rows % chunk == 0, (rows, chunk)
        n_chunks = rows // chunk

        def chunk_of(b, h, i, n_chunks=n_chunks):
            return (((b * H + h) * nq + i) * n_chunks) // n_steps

        in_specs.append(pl.BlockSpec((None,) * len(lead) + (chunk, cols),
                                     lambda b, h, i, lead=lead, f=chunk_of: (*lead, f(b, h, i), 0)))
        args.append(arr)
        if col_scale is not None:
            in_specs.append(pl.BlockSpec((1, cols), lambda b, h, i: (0, 0)))
            args.append(col_scale.reshape(1, cols))
        scaled.append(col_scale is not None)
        out_shape.append(jax.ShapeDtypeStruct((rows, cols), BF16))
        out_specs.append(pl.BlockSpec((chunk, cols), lambda b, h, i, f=chunk_of: (f(b, h, i), 0)))
    outs = pl.pallas_call(
        functools.partial(_da_attn_kernel, t=t, hd=hd, n_heads=H, lambda_init=lambda_init,
                          cast_scaled=tuple(scaled)),
        out_shape=out_shape,
        grid=(batch, H, nq),
        in_specs=in_specs,
        out_specs=out_specs,
        scratch_shapes=[pltpu.VMEM((nq, vd, t), BF16), pltpu.VMEM((2, t, t), F32),
                        pltpu.VMEM((2, t, t), F32), pltpu.VMEM((2, t, t), F32),
                        pltpu.VMEM((2, 1, t), F32), pltpu.VMEM((2, 1, t), F32),
                        pltpu.VMEM((2, vd, t), F32)],
        compiler_params=_params("arbitrary", "arbitrary", "arbitrary"),
        name="diff_attention",
    )(*args)
    return outs[0], outs[1:]


def _ret_log_gamma(h):
    return math.log(1.0 - 2.0 ** (-5.0 - h))


def _retention_kernel(q_ref, k_ref, v_ref, g_ref, gn_ref, o_ref, state_ref, dmat_ref, qdec_ref,
                      kdec_ref, *, tb, dk, dv, n_heads):
    t = pl.program_id(1)

    @pl.when((pl.program_id(0) == 0) & (t == 0))
    def _():
        r = lax.broadcasted_iota(jnp.int32, (tb, tb), 0)
        c = lax.broadcasted_iota(jnp.int32, (tb, tb), 1)
        dist = jnp.abs(r - c).astype(F32)
        visible = (c // CHUNK) <= (r // CHUNK)
        pos = lax.broadcasted_iota(jnp.int32, (tb, dk), 0).astype(F32)
        for h in range(n_heads):
            lg = _ret_log_gamma(h)
            dmat_ref[h] = jnp.where(visible, jnp.exp(lg * dist), 0.0)
            qdec_ref[h] = jnp.exp(lg * pos)
            kdec_ref[h] = jnp.exp(lg * (tb - pos))

    @pl.when(t == 0)
    def _():
        state_ref[...] = jnp.zeros_like(state_ref)

    for h in range(n_heads):
        qh = q_ref[:, h * dk:(h + 1) * dk]
        kh = k_ref[:, h * dk:(h + 1) * dk]
        vh = v_ref[:, h * dv:(h + 1) * dv]
        inner = (_dot_nt(qh, kh) * dmat_ref[h]).astype(BF16)
        qd = (qh.astype(F32) * qdec_ref[h]).astype(BF16)
        kd = (kh.astype(F32) * kdec_ref[h]).astype(BF16)
        state = state_ref[h]
        o = _dot(inner, vh) + _dot(qd, state.astype(BF16))
        state_ref[h] = state * math.exp(_ret_log_gamma(h) * tb) + _dot_tn(kd, vh)
        mu = jnp.mean(o, axis=-1, keepdims=True)
        d = o - mu
        y = d * lax.rsqrt(jnp.mean(d * d, axis=-1, keepdims=True) + EPS) * gn_ref[...]
        gate = g_ref[:, h * dv:(h + 1) * dv].astype(F32)
        o_ref[:, h * dv:(h + 1) * dv] = (jax.nn.silu(gate) * y).astype(o_ref.dtype)


def _retention(proj, gn_g, *, batch, seq):
    M, W = proj.shape
    H = RET_HEADS
    dk = W // (6 * H)
    dv = 2 * dk
    tb = min(256, seq)
    nt = seq // tb
    row = lambda b, t: b * nt + t
    return pl.pallas_call(
        functools.partial(_retention_kernel, tb=tb, dk=dk, dv=dv, n_heads=H),
        out_shape=jax.ShapeDtypeStruct((M, H * dv), BF16),
        grid=(batch, nt),
        in_specs=[
            pl.BlockSpec((tb, H * dk), lambda b, t: (row(b, t), 0)),
            pl.BlockSpec((tb, H * dk), lambda b, t: (row(b, t), 1)),
            pl.BlockSpec((tb, H * dv), lambda b, t: (row(b, t), 1)),
            pl.BlockSpec((tb, H * dv), lambda b, t: (row(b, t), 2)),
            pl.BlockSpec((1, dv), lambda b, t: (0, 0)),
        ],
        out_specs=pl.BlockSpec((tb, H * dv), lambda b, t: (row(b, t), 0)),
        scratch_shapes=[pltpu.VMEM((H, dk, dv), F32), pltpu.VMEM((H, tb, tb), F32),
                        pltpu.VMEM((H, tb, dk), F32), pltpu.VMEM((H, tb, dk), F32)],
        compiler_params=_params("arbitrary", "arbitrary"),
        name="retention",
    )(proj, proj, proj, proj, gn_g.reshape(1, dv))


def kernel(x, c, ada_w, ada_b, norm_g, ffn_w_in, ffn_w_out, da_w_qkv, da_lambda, da_subln_g,
           da_w_o, ret_w_qkvg, ret_gn_g, ret_w_o, final_g):
    B, S, D = x.shape
    depth = ada_w.shape[0]
    mods = _ada_mods(c, ada_w, ada_b)

    da_hd = D // (2 * DA_HEADS)
    da_scale = jnp.concatenate([jnp.full((2 * DA_HEADS * da_hd,), da_hd ** -0.5 * LOG2E, F32),
                                jnp.ones((da_w_qkv.shape[2] - 2 * DA_HEADS * da_hd,), F32)])
    ret_dk = D // RET_HEADS
    ret_scale = jnp.concatenate([jnp.ones((RET_HEADS * ret_dk,), F32),
                                 jnp.full((RET_HEADS * ret_dk,), ret_dk ** -0.5, F32),
                                 jnp.ones((ret_w_qkvg.shape[2] - 2 * RET_HEADS * ret_dk,), F32)])
    wb = {
        ("ffn_in", 0, 0): ffn_w_in[0, 0].astype(BF16),
        ("ffn_out", 0, 0): ffn_w_out[0, 0].astype(BF16),
        ("da_qkv", 0): (da_w_qkv[0] * da_scale).astype(BF16),
    }
    jobs = [(("da_o", 0), da_w_o, (0,), None),
            (("ffn_in", 0, 1), ffn_w_in, (0, 1), None),
            (("ffn_out", 0, 1), ffn_w_out, (0, 1), None)]
    for i in range(1, depth):
        jobs += [(("ffn_in", i, k), ffn_w_in, (i, k), None) for k in range(2)]
        jobs += [(("ffn_out", i, k), ffn_w_out, (i, k), None) for k in range(2)]
        j = i // N_MIXERS
        if i % N_MIXERS == 0:
            jobs += [(("da_qkv", j), da_w_qkv, (j,), da_scale), (("da_o", j), da_w_o, (j,), None)]
        else:
            jobs += [(("ret", j), ret_w_qkvg, (j,), ret_scale), (("ret_o", j), ret_w_o, (j,), None)]

    xf = x.reshape(B * S, D)
    for i in range(depth):
        m = mods[i]
        xf = _ffn(xf, m, 0, norm_g[i, 0], wb["ffn_in", i, 0], wb["ffn_out", i, 0], seq=S)
        j = i // N_MIXERS
        if i % N_MIXERS == 0:
            lambda_init = 0.8 - 0.6 * math.exp(-0.3 * i)
            qkv = _norm_mod_matmul(xf, m, 3, norm_g[i, 1], wb["da_qkv", j], seq=S)
            heads, cast = _diff_attention(qkv, da_lambda[j], da_subln_g[j], batch=B, seq=S,
                                          lambda_init=lambda_init,
                                          cast_jobs=[job[1:] for job in jobs])
            wb.update({job[0]: w for job, w in zip(jobs, cast)})
            jobs = []
            xf = _matmul_residual(heads, wb["da_o", j], xf, m, 5, seq=S)
        else:
            proj = _norm_mod_matmul(xf, m, 3, norm_g[i, 1], wb["ret", j], seq=S)
            heads = _retention(proj, ret_gn_g[j], batch=B, seq=S)
            xf = _matmul_residual(heads, wb["ret_o", j], xf, m, 5, seq=S)
        xf = _ffn(xf, m, 6, norm_g[i, 2], wb["ffn_in", i, 1], wb["ffn_out", i, 1],
                  final_g=final_g if i == depth - 1 else None, seq=S)
    return xf.reshape(B, S, D)
```

```python
import functools
import math

import jax
import jax.numpy as jnp
from jax import lax
from jax.experimental import pallas as pl
from jax.experimental.pallas import tpu as pltpu

F32 = jnp.float32
BF16 = jnp.bfloat16

EPS = 1e-5
LOG2E = math.log2(math.e)
CHUNK = 64
N_MOD = 9
N_MIXERS = 2
DA_HEADS = 8
RET_HEADS = 8

BF16_SUBLANES = 16
V7X_VMEM_BYTES = 64 * 1024 * 1024
VMEM_LIMIT_BYTES = V7X_VMEM_BYTES - 8 * 1024 * 1024


def _params(*semantics):
    return pltpu.CompilerParams(dimension_semantics=semantics, vmem_limit_bytes=VMEM_LIMIT_BYTES)


def _dot(a, b):
    return jnp.dot(a, b, preferred_element_type=F32)


def _dot_nt(a, b):
    return lax.dot_general(a, b, (((1,), (1,)), ((), ())), preferred_element_type=F32)


def _dot_tn(a, b):
    return lax.dot_general(a, b, (((0,), (0,)), ((), ())), preferred_element_type=F32)


def _rms(x, g):
    return x * lax.rsqrt(jnp.mean(x * x, axis=-1, keepdims=True) + EPS) * g


NORM_ROWS = 32


def _norm_mod_into(h_ref, x_ref, g_ref, shift_ref, scale_ref):
    gs = g_ref[...] * (1.0 + scale_ref[...])
    shift = shift_ref[...]

    def body(i, carry):
        r0 = pl.multiple_of(i * NORM_ROWS, NORM_ROWS)
        x = x_ref[pl.ds(r0, NORM_ROWS), :]
        rinv = lax.rsqrt(jnp.mean(x * x, axis=-1, keepdims=True) + EPS)
        h_ref[pl.ds(r0, NORM_ROWS), :] = (x * rinv * gs + shift).astype(BF16)
        return carry

    lax.fori_loop(0, x_ref.shape[0] // NORM_ROWS, body, 0, unroll=4)


def _ada_kernel(c_ref, w_ref, b_ref, o_ref):
    cs = jax.nn.silu(c_ref[...]).astype(BF16)
    o_ref[...] = _dot(cs, w_ref[...].astype(BF16)) + b_ref[...]


def _ada_mods(c, ada_w, ada_b):
    B, D = c.shape
    L, _, W = ada_w.shape
    rows = 8
    tn = min(1024, W)
    c_pad = jnp.zeros((rows, D), F32).at[:B].set(c)
    out = pl.pallas_call(
        _ada_kernel,
        out_shape=jax.ShapeDtypeStruct((L, rows, W), F32),
        grid=(L, W // tn),
        in_specs=[
            pl.BlockSpec((rows, D), lambda l, n: (0, 0)),
            pl.BlockSpec((None, D, tn), lambda l, n: (l, 0, n)),
            pl.BlockSpec((None, 1, tn), lambda l, n: (l, 0, n)),
        ],
        out_specs=pl.BlockSpec((None, rows, tn), lambda l, n: (l, 0, n)),
        compiler_params=_params("parallel", "parallel"),
        name="ada_mods",
    )(c_pad, ada_w, ada_b.reshape(L, 1, W))
    return out[:, :B].reshape(L, B, N_MOD, D).transpose(0, 2, 1, 3)[:, :, :, None, :]


def _ffn_kernel(x_ref, sh_ref, sc_ref, gt_ref, ng_ref, wg_ref, wu_ref, wo_ref, *rest, final):
    if final:
        fg_ref, o_ref, h_ref = rest
    else:
        o_ref, h_ref = rest
    j = pl.program_id(1)

    @pl.when(j == 0)
    def _():
        _norm_mod_into(h_ref, x_ref, ng_ref, sh_ref, sc_ref)
        o_ref[...] = jnp.zeros_like(o_ref)

    h = h_ref[...]
    a = _dot(h, wg_ref[...])
    b = _dot(h, wu_ref[...])
    act = (jax.nn.silu(a) * b).astype(BF16)
    o_ref[...] += _dot(act, wo_ref[...])

    @pl.when(j == pl.num_programs(1) - 1)
    def _():
        half_gate = 0.5 * gt_ref[...]

        def body(i, carry):
            r0 = pl.multiple_of(i * NORM_ROWS, NORM_ROWS)
            y = x_ref[pl.ds(r0, NORM_ROWS), :] + half_gate * o_ref[pl.ds(r0, NORM_ROWS), :]
            if final:
                y = _rms(y, fg_ref[...])
            o_ref[pl.ds(r0, NORM_ROWS), :] = y
            return carry

        lax.fori_loop(0, x_ref.shape[0] // NORM_ROWS, body, 0, unroll=4)


def _ffn(x, mods, k0, norm_g, w_in, w_out, final_g=None, *, seq):
    M, D = x.shape
    F = w_out.shape[0]
    tm = min(1024, seq)
    tf = min(512, F)
    nf = F // tf
    bpb = seq // tm
    vec = lambda k: pl.BlockSpec((None, None, 1, D), lambda i, j: (k, i // bpb, 0, 0))
    in_specs = [
        pl.BlockSpec((tm, D), lambda i, j: (i, 0)),
        vec(k0), vec(k0 + 1), vec(k0 + 2),
        pl.BlockSpec((1, D), lambda i, j: (0, 0)),
        pl.BlockSpec((D, tf), lambda i, j: (0, j)),
        pl.BlockSpec((D, tf), lambda i, j: (0, nf + j)),
        pl.BlockSpec((tf, D), lambda i, j: (j, 0)),
    ]
    args = [x, mods, mods, mods, norm_g.reshape(1, D), w_in, w_in, w_out]
    if final_g is not None:
        in_specs.append(pl.BlockSpec((1, D), lambda i, j: (0, 0)))
        args.append(final_g.reshape(1, D))
    return pl.pallas_call(
        functools.partial(_ffn_kernel, final=final_g is not None),
        out_shape=jax.ShapeDtypeStruct((M, D), F32),
        grid=(M // tm, nf),
        in_specs=in_specs,
        out_specs=pl.BlockSpec((tm, D), lambda i, j: (i, 0)),
        scratch_shapes=[pltpu.VMEM((tm, D), BF16)],
        compiler_params=_params("parallel", "arbitrary"),
        name="ffn",
    )(*args)


def _nm_matmul_kernel(x_ref, sh_ref, sc_ref, ng_ref, w_ref, o_ref, h_ref):
    @pl.when(pl.program_id(1) == 0)
    def _():
        _norm_mod_into(h_ref, x_ref, ng_ref, sh_ref, sc_ref)

    o_ref[...] = _dot(h_ref[...], w_ref[...]).astype(o_ref.dtype)


def _norm_mod_matmul(x, mods, k0, norm_g, w, *, seq):
    M, D = x.shape
    N = w.shape[1]
    tm = min(1024, seq)
    tn = min(2048, N)
    bpb = seq // tm
    vec = lambda k: pl.BlockSpec((None, None, 1, D), lambda i, n: (k, i // bpb, 0, 0))
    return pl.pallas_call(
        _nm_matmul_kernel,
        out_shape=jax.ShapeDtypeStruct((M, N), BF16),
        grid=(M // tm, N // tn),
        in_specs=[
            pl.BlockSpec((tm, D), lambda i, n: (i, 0)),
            vec(k0), vec(k0 + 1),
            pl.BlockSpec((1, D), lambda i, n: (0, 0)),
            pl.BlockSpec((D, tn), lambda i, n: (0, n)),
        ],
        out_specs=pl.BlockSpec((tm, tn), lambda i, n: (i, n)),
        scratch_shapes=[pltpu.VMEM((tm, D), BF16)],
        compiler_params=_params("parallel", "arbitrary"),
        name="norm_mod_matmul",
    )(x, mods, mods, norm_g.reshape(1, D), w)


def _mm_res_kernel(a_ref, w_ref, x_ref, gt_ref, o_ref):
    o_ref[...] = x_ref[...] + gt_ref[...] * _dot(a_ref[...], w_ref[...])


def _matmul_residual(a, w, x, mods, k, *, seq):
    M, K = a.shape
    N = w.shape[1]
    tm = min(1024, seq)
    tn = min(1024, N)
    bpb = seq // tm
    return pl.pallas_call(
        _mm_res_kernel,
        out_shape=jax.ShapeDtypeStruct((M, N), F32),
        grid=(M // tm, N // tn),
        in_specs=[
            pl.BlockSpec((tm, K), lambda i, n: (i, 0)),
            pl.BlockSpec((K, tn), lambda i, n: (0, n)),
            pl.BlockSpec((tm, tn), lambda i, n: (i, n)),
            pl.BlockSpec((None, None, 1, tn), lambda i, n: (k, i // bpb, 0, n)),
        ],
        out_specs=pl.BlockSpec((tm, tn), lambda i, n: (i, n)),
        compiler_params=_params("parallel", "parallel"),
        name="matmul_residual",
    )(a, w, x, mods)


def _da_attn_kernel(q_ref, k_ref, v_ref, lam_ref, g_ref, *rest, t, hd, n_heads, lambda_init,
                    cast_scaled):
    n_cast_in = len(cast_scaled) + sum(cast_scaled)
    cast_in = rest[:n_cast_in]
    o_ref = rest[n_cast_in]
    cast_out = rest[n_cast_in + 1:n_cast_in + 1 + len(cast_scaled)]
    vt_ref, bias_ref, sa_ref, sb_ref, m_ref, acc_ref = rest[n_cast_in + 1 + len(cast_scaled):]

    head = pl.program_id(1)
    qi = pl.program_id(2)
    nblk = vt_ref.shape[0]
    vd = 2 * hd
    hv = jnp.full((1, 1), head + 1, jnp.int32).astype(F32)
    slope = jnp.exp2(hv * (-8.0 / n_heads)) * LOG2E

    @pl.when(qi == 0)
    def _():
        for b in range(nblk):
            vt_ref[b, 0:vd, :] = v_ref[b * t:(b + 1) * t, :].astype(F32).T.astype(BF16)
            vt_ref[b, vd:, :] = jnp.ones((vt_ref.shape[1] - vd, t), BF16)
        r = lax.broadcasted_iota(jnp.int32, (t, t), 0)
        c = lax.broadcasted_iota(jnp.int32, (t, t), 1)
        bias_ref[0] = slope * r.astype(F32)
        bias_ref[1] = jnp.where((r // CHUNK) <= (c // CHUNK),
                                slope * (c - jnp.abs(c - r)).astype(F32), -jnp.inf)

    qts = [q_ref[:, n * hd:(n + 1) * hd].astype(F32).T.astype(BF16) for n in range(2)]
    for n in range(2):
        m_ref[n] = jnp.full((1, t), -jnp.inf, F32)
        acc_ref[n] = jnp.zeros(acc_ref.shape[1:], F32)

    def qk(kb, s_ref):
        kk = pl.multiple_of(kb * t, t)
        bias = bias_ref[(kb == qi).astype(jnp.int32)]
        for n in range(2):
            s_ref[n] = _dot(k_ref[pl.ds(kk, t), n * hd:(n + 1) * hd], qts[n]) + bias

    def softmax_pv(kb, s_ref):
        off = slope * ((kb - qi) * t).astype(F32)
        vt_b = vt_ref[kb]
        for n in range(2):
            s = s_ref[n]
            m_old = m_ref[n]
            m_new = jnp.maximum(m_old, jnp.max(s, axis=0, keepdims=True) + off)
            alpha = jnp.exp2(m_old - m_new)
            p = jnp.exp2(s - (m_new - off))
            m_ref[n] = m_new
            acc_ref[n] = alpha * acc_ref[n] + _dot(vt_b, p.astype(BF16))

    qk(0, sa_ref)

    pos = 0
    for dst_ref, has_scale in zip(cast_out, cast_scaled):
        w = cast_in[pos][...]
        if has_scale:
            w = w * cast_in[pos + 1][...]
        dst_ref[...] = w.astype(BF16)
        pos += 2 if has_scale else 1

    def body(pair, carry):
        kb = 2 * pair
        qk(kb + 1, sb_ref)
        softmax_pv(kb, sa_ref)
        qk(jnp.minimum(kb + 2, qi), sa_ref)
        softmax_pv(kb + 1, sb_ref)
        return carry

    lax.fori_loop(0, (qi + 1) // 2, body, 0)

    @pl.when(qi % 2 == 0)
    def _():
        softmax_pv(qi, sa_ref)

    lam = lam_ref[...]
    lam_val = (jnp.exp(jnp.sum(lam[0:1] * lam[1:2], axis=-1, keepdims=True))
               - jnp.exp(jnp.sum(lam[2:3] * lam[3:4], axis=-1, keepdims=True)) + lambda_init)
    heads_t = [acc_ref[n, 0:vd, :] / acc_ref[n, vd:vd + 1, :] for n in range(2)]
    o_t = heads_t[0] - lam_val * heads_t[1]
    o_t = o_t * lax.rsqrt(jnp.mean(o_t * o_t, axis=0, keepdims=True) + EPS)
    o_ref[...] = (o_t.T * (g_ref[...] * (1.0 - lambda_init))).astype(o_ref.dtype)


def _diff_attention(qkv, lam, subln_g, *, batch, seq, lambda_init, cast_jobs=()):
    M = qkv.shape[0]
    H = DA_HEADS
    hd = qkv.shape[1] // (6 * H)
    vd = 2 * hd
    t = min(512, seq)
    nq = seq // t
    n_steps = batch * H * nq
    in_specs = [
        pl.BlockSpec((t, vd), lambda b, h, i: (b * nq + i, h)),
        pl.BlockSpec((seq, vd), lambda b, h, i: (b, H + h)),
        pl.BlockSpec((seq, vd), lambda b, h, i: (b, 2 * H + h)),
        pl.BlockSpec((4, hd), lambda b, h, i: (0, 0)),
        pl.BlockSpec((1, vd), lambda b, h, i: (0, 0)),
    ]
    args = [qkv, qkv, qkv, lam, subln_g.reshape(1, vd)]
    out_shape = [jax.ShapeDtypeStruct((M, H * vd), BF16)]
    out_specs = [pl.BlockSpec((t, vd), lambda b, h, i: (b * nq + i, h))]
    scaled = []
    for arr, lead, col_scale in cast_jobs:
        rows, cols = arr.shape[-2:]
        chunk = BF16_SUBLANES * pl.cdiv(rows, BF16_SUBLANES * n_steps)
        assert rows % chunk == 0, (rows, chunk)
        n_chunks = rows // chunk

        def chunk_of(b, h, i, n_chunks=n_chunks):
            return (((b * H + h) * nq + i) * n_chunks) // n_steps

        in_specs.append(pl.BlockSpec((None,) * len(lead) + (chunk, cols),
                                     lambda b, h, i, lead=lead, f=chunk_of: (*lead, f(b, h, i), 0)))
        args.append(arr)
        if col_scale is not None:
            in_specs.append(pl.BlockSpec((1, cols), lambda b, h, i: (0, 0)))
            args.append(col_scale.reshape(1, cols))
        scaled.append(col_scale is not None)
        out_shape.append(jax.ShapeDtypeStruct((rows, cols), BF16))
        out_specs.append(pl.BlockSpec((chunk, cols), lambda b, h, i, f=chunk_of: (f(b, h, i), 0)))
    outs = pl.pallas_call(
        functools.partial(_da_attn_kernel, t=t, hd=hd, n_heads=H, lambda_init=lambda_init,
                          cast_scaled=tuple(scaled)),
        out_shape=out_shape,
        grid=(batch, H, nq),
        in_specs=in_specs,
        out_specs=out_specs,
        scratch_shapes=[pltpu.VMEM((nq, vd + BF16_SUBLANES, t), BF16),
                        pltpu.VMEM((2, t, t), F32),
                        pltpu.VMEM((2, t, t), F32), pltpu.VMEM((2, t, t), F32),
                        pltpu.VMEM((2, 1, t), F32),
                        pltpu.VMEM((2, vd + BF16_SUBLANES, t), F32)],
        compiler_params=_params("arbitrary", "arbitrary", "arbitrary"),
        name="diff_attention",
    )(*args)
    return outs[0], outs[1:]


def _ret_log_gamma(h):
    return math.log(1.0 - 2.0 ** (-5.0 - h))


def _retention_kernel(q_ref, k_ref, v_ref, g_ref, gn_ref, o_ref, state_ref, dmat_ref, qdec_ref,
                      kdec_ref, *, tb, dk, dv, n_heads):
    t = pl.program_id(1)

    @pl.when((pl.program_id(0) == 0) & (t == 0))
    def _():
        r = lax.broadcasted_iota(jnp.int32, (tb, tb), 0)
        c = lax.broadcasted_iota(jnp.int32, (tb, tb), 1)
        dist = jnp.abs(r - c).astype(F32)
        visible = (c // CHUNK) <= (r // CHUNK)
        pos = lax.broadcasted_iota(jnp.int32, (tb, dk), 0).astype(F32)
        for h in range(n_heads):
            lg = _ret_log_gamma(h)
            dmat_ref[h] = jnp.where(visible, jnp.exp(lg * dist), 0.0)
            qdec_ref[h] = jnp.exp(lg * pos)
            kdec_ref[h] = jnp.exp(lg * (tb - pos))

    @pl.when(t == 0)
    def _():
        state_ref[...] = jnp.zeros_like(state_ref)

    for h in range(n_heads):
        qh = q_ref[:, h * dk:(h + 1) * dk]
        kh = k_ref[:, h * dk:(h + 1) * dk]
        vh = v_ref[:, h * dv:(h + 1) * dv]
        inner = (_dot_nt(qh, kh) * dmat_ref[h]).astype(BF16)
        qd = (qh.astype(F32) * qdec_ref[h]).astype(BF16)
        kd = (kh.astype(F32) * kdec_ref[h]).astype(BF16)
        state = state_ref[h]
        o = _dot(inner, vh) + _dot(qd, state.astype(BF16))
        state_ref[h] = state * math.exp(_ret_log_gamma(h) * tb) + _dot_tn(kd, vh)
        mu = jnp.mean(o, axis=-1, keepdims=True)
        d = o - mu
        y = d * lax.rsqrt(jnp.mean(d * d, axis=-1, keepdims=True) + EPS) * gn_ref[...]
        gate = g_ref[:, h * dv:(h + 1) * dv].astype(F32)
        o_ref[:, h * dv:(h + 1) * dv] = (jax.nn.silu(gate) * y).astype(o_ref.dtype)


def _retention(proj, gn_g, *, batch, seq):
    M, W = proj.shape
    H = RET_HEADS
    dk = W // (6 * H)
    dv = 2 * dk
    tb = min(256, seq)
    nt = seq // tb
    row = lambda b, t: b * nt + t
    return pl.pallas_call(
        functools.partial(_retention_kernel, tb=tb, dk=dk, dv=dv, n_heads=H),
        out_shape=jax.ShapeDtypeStruct((M, H * dv), BF16),
        grid=(batch, nt),
        in_specs=[
            pl.BlockSpec((tb, H * dk), lambda b, t: (row(b, t), 0)),
            pl.BlockSpec((tb, H * dk), lambda b, t: (row(b, t), 1)),
            pl.BlockSpec((tb, H * dv), lambda b, t: (row(b, t), 1)),
            pl.BlockSpec((tb, H * dv), lambda b, t: (row(b, t), 2)),
            pl.BlockSpec((1, dv), lambda b, t: (0, 0)),
        ],
        out_specs=pl.BlockSpec((tb, H * dv), lambda b, t: (row(b, t), 0)),
        scratch_shapes=[pltpu.VMEM((H, dk, dv), F32), pltpu.VMEM((H, tb, tb), F32),
                        pltpu.VMEM((H, tb, dk), F32), pltpu.VMEM((H, tb, dk), F32)],
        compiler_params=_params("arbitrary", "arbitrary"),
        name="retention",
    )(proj, proj, proj, proj, gn_g.reshape(1, dv))


def kernel(x, c, ada_w, ada_b, norm_g, ffn_w_in, ffn_w_out, da_w_qkv, da_lambda, da_subln_g,
           da_w_o, ret_w_qkvg, ret_gn_g, ret_w_o, final_g):
    B, S, D = x.shape
    depth = ada_w.shape[0]
    mods = _ada_mods(c, ada_w, ada_b)

    da_hd = D // (2 * DA_HEADS)
    da_scale = jnp.concatenate([jnp.full((2 * DA_HEADS * da_hd,), da_hd ** -0.5 * LOG2E, F32),
                                jnp.ones((da_w_qkv.shape[2] - 2 * DA_HEADS * da_hd,), F32)])
    ret_dk = D // RET_HEADS
    ret_scale = jnp.concatenate([jnp.ones((RET_HEADS * ret_dk,), F32),
                                 jnp.full((RET_HEADS * ret_dk,), ret_dk ** -0.5, F32),
                                 jnp.ones((ret_w_qkvg.shape[2] - 2 * RET_HEADS * ret_dk,), F32)])
    wb = {
        ("ffn_in", 0, 0): ffn_w_in[0, 0].astype(BF16),
        ("ffn_out", 0, 0): ffn_w_out[0, 0].astype(BF16),
        ("da_qkv", 0): (da_w_qkv[0] * da_scale).astype(BF16),
    }
    jobs = [(("da_o", 0), da_w_o, (0,), None),
            (("ffn_in", 0, 1), ffn_w_in, (0, 1), None),
            (("ffn_out", 0, 1), ffn_w_out, (0, 1), None)]
    for i in range(1, depth):
        jobs += [(("ffn_in", i, k), ffn_w_in, (i, k), None) for k in range(2)]
        jobs += [(("ffn_out", i, k), ffn_w_out, (i, k), None) for k in range(2)]
        j = i // N_MIXERS
        if i % N_MIXERS == 0:
            jobs += [(("da_qkv", j), da_w_qkv, (j,), da_scale), (("da_o", j), da_w_o, (j,), None)]
        else:
            jobs += [(("ret", j), ret_w_qkvg, (j,), ret_scale), (("ret_o", j), ret_w_o, (j,), None)]

    xf = x.reshape(B * S, D)
    for i in range(depth):
        m = mods[i]
        xf = _ffn(xf, m, 0, norm_g[i, 0], wb["ffn_in", i, 0], wb["ffn_out", i, 0], seq=S)
        j = i // N_MIXERS
        if i % N_MIXERS == 0:
            lambda_init = 0.8 - 0.6 * math.exp(-0.3 * i)
            qkv = _norm_mod_matmul(xf, m, 3, norm_g[i, 1], wb["da_qkv", j], seq=S)
            heads, cast = _diff_attention(qkv, da_lambda[j], da_subln_g[j], batch=B, seq=S,
                                          lambda_init=lambda_init,
                                          cast_jobs=[job[1:] for job in jobs])
            wb.update({job[0]: w for job, w in zip(jobs, cast)})
            jobs = []
            xf = _matmul_residual(heads, wb["da_o", j], xf, m, 5, seq=S)
        else:
            proj = _norm_mod_matmul(xf, m, 3, norm_g[i, 1], wb["ret", j], seq=S)
            heads = _retention(proj, ret_gn_g[j], batch=B, seq=S)
            xf = _matmul_residual(heads, wb["ret_o", j], xf, m, 5, seq=S)
        xf = _ffn(xf, m, 6, norm_g[i, 2], wb["ffn_in", i, 1], wb["ffn_out", i, 1],
                  final_g=final_g if i == depth - 1 else None, seq=S)
    return xf.reshape(B, S, D)
```

```python
import functools
import math

import jax
import jax.numpy as jnp
from jax import lax
from jax.experimental import pallas as pl
from jax.experimental.pallas import tpu as pltpu

F32 = jnp.float32
BF16 = jnp.bfloat16

EPS = 1e-5
LOG2E = math.log2(math.e)
CHUNK = 64
N_MOD = 9
N_MIXERS = 2
DA_HEADS = 8
RET_HEADS = 8

BF16_SUBLANES = 16
V7X_VMEM_BYTES = 64 * 1024 * 1024
VMEM_LIMIT_BYTES = V7X_VMEM_BYTES - 8 * 1024 * 1024


def _params(*semantics):
    return pltpu.CompilerParams(dimension_semantics=semantics, vmem_limit_bytes=VMEM_LIMIT_BYTES)


def _dot(a, b):
    return jnp.dot(a, b, preferred_element_type=F32)


def _dot_nt(a, b):
    return lax.dot_general(a, b, (((1,), (1,)), ((), ())), preferred_element_type=F32)


def _dot_tn(a, b):
    return lax.dot_general(a, b, (((0,), (0,)), ((), ())), preferred_element_type=F32)


def _rms(x, g):
    return x * lax.rsqrt(jnp.mean(x * x, axis=-1, keepdims=True) + EPS) * g


NORM_ROWS = 32


def _norm_mod_into(h_ref, x_ref, g_ref, shift_ref, scale_ref):
    gs = g_ref[...] * (1.0 + scale_ref[...])
    shift = shift_ref[...]

    def body(i, carry):
        r0 = pl.multiple_of(i * NORM_ROWS, NORM_ROWS)
        x = x_ref[pl.ds(r0, NORM_ROWS), :]
        rinv = lax.rsqrt(jnp.mean(x * x, axis=-1, keepdims=True) + EPS)
        h_ref[pl.ds(r0, NORM_ROWS), :] = (x * rinv * gs + shift).astype(BF16)
        return carry

    lax.fori_loop(0, x_ref.shape[0] // NORM_ROWS, body, 0, unroll=4)


def _ada_kernel(c_ref, w_ref, b_ref, o_ref):
    cs = jax.nn.silu(c_ref[...]).astype(BF16)
    o_ref[...] = _dot(cs, w_ref[...].astype(BF16)) + b_ref[...]


def _ada_mods(c, ada_w, ada_b):
    B, D = c.shape
    L, _, W = ada_w.shape
    rows = 8
    tn = min(1024, W)
    c_pad = jnp.zeros((rows, D), F32).at[:B].set(c)
    out = pl.pallas_call(
        _ada_kernel,
        out_shape=jax.ShapeDtypeStruct((L, rows, W), F32),
        grid=(L, W // tn),
        in_specs=[
            pl.BlockSpec((rows, D), lambda l, n: (0, 0)),
            pl.BlockSpec((None, D, tn), lambda l, n: (l, 0, n)),
            pl.BlockSpec((None, 1, tn), lambda l, n: (l, 0, n)),
        ],
        out_specs=pl.BlockSpec((None, rows, tn), lambda l, n: (l, 0, n)),
        compiler_params=_params("parallel", "parallel"),
        name="ada_mods",
    )(c_pad, ada_w, ada_b.reshape(L, 1, W))
    return out[:, :B].reshape(L, B, N_MOD, D).transpose(0, 2, 1, 3)[:, :, :, None, :]


def _ffn_kernel(x_ref, sh_ref, sc_ref, gt_ref, ng_ref, wg_ref, wu_ref, wo_ref, *rest, final, nf):
    if final:
        fg_ref, o_ref, h_ref = rest
    else:
        o_ref, h_ref = rest
    j = pl.program_id(1)
    assert nf >= 2, "the first and last hidden-dimension steps are distinct code paths"

    @pl.when(j == 0)
    def _():
        _norm_mod_into(h_ref, x_ref, ng_ref, sh_ref, sc_ref)

    def contribution():
        h = h_ref[...]
        a = _dot(h, wg_ref[...])
        b = _dot(h, wu_ref[...])
        act = (jax.nn.silu(a) * b).astype(BF16)
        return _dot(act, wo_ref[...])

    @pl.when(j == 0)
    def _():
        o_ref[...] = contribution()

    @pl.when((j > 0) & (j < nf - 1))
    def _():
        o_ref[...] += contribution()

    @pl.when(j == nf - 1)
    def _():
        o_ref[...] = x_ref[...] + (0.5 * gt_ref[...]) * (o_ref[...] + contribution())
        if final:
            for r0 in range(0, x_ref.shape[0], NORM_ROWS):
                o_ref[r0:r0 + NORM_ROWS, :] = _rms(o_ref[r0:r0 + NORM_ROWS, :], fg_ref[...])


def _ffn(x, mods, k0, norm_g, w_in, w_out, final_g=None, *, seq):
    M, D = x.shape
    F = w_out.shape[0]
    tm = min(1024, seq)
    tf = min(512, F)
    nf = F // tf
    bpb = seq // tm
    vec = lambda k: pl.BlockSpec((None, None, 1, D), lambda i, j: (k, i // bpb, 0, 0))
    in_specs = [
        pl.BlockSpec((tm, D), lambda i, j: (i, 0)),
        vec(k0), vec(k0 + 1), vec(k0 + 2),
        pl.BlockSpec((1, D), lambda i, j: (0, 0)),
        pl.BlockSpec((D, tf), lambda i, j: (0, j)),
        pl.BlockSpec((D, tf), lambda i, j: (0, nf + j)),
        pl.BlockSpec((tf, D), lambda i, j: (j, 0)),
    ]
    args = [x, mods, mods, mods, norm_g.reshape(1, D), w_in, w_in, w_out]
    if final_g is not None:
        in_specs.append(pl.BlockSpec((1, D), lambda i, j: (0, 0)))
        args.append(final_g.reshape(1, D))
    return pl.pallas_call(
        functools.partial(_ffn_kernel, final=final_g is not None, nf=nf),
        out_shape=jax.ShapeDtypeStruct((M, D), F32),
        grid=(M // tm, nf),
        in_specs=in_specs,
        out_specs=pl.BlockSpec((tm, D), lambda i, j: (i, 0)),
        scratch_shapes=[pltpu.VMEM((tm, D), BF16)],
        compiler_params=_params("parallel", "arbitrary"),
        name="ffn",
    )(*args)


def _nm_matmul_kernel(x_ref, sh_ref, sc_ref, ng_ref, w_ref, o_ref, h_ref):
    @pl.when(pl.program_id(1) == 0)
    def _():
        _norm_mod_into(h_ref, x_ref, ng_ref, sh_ref, sc_ref)

    o_ref[...] = _dot(h_ref[...], w_ref[...]).astype(o_ref.dtype)


def _norm_mod_matmul(x, mods, k0, norm_g, w, *, seq):
    M, D = x.shape
    N = w.shape[1]
    tm = min(1024, seq)
    tn = min(2048, N)
    bpb = seq // tm
    vec = lambda k: pl.BlockSpec((None, None, 1, D), lambda i, n: (k, i // bpb, 0, 0))
    return pl.pallas_call(
        _nm_matmul_kernel,
        out_shape=jax.ShapeDtypeStruct((M, N), BF16),
        grid=(M // tm, N // tn),
        in_specs=[
            pl.BlockSpec((tm, D), lambda i, n: (i, 0)),
            vec(k0), vec(k0 + 1),
            pl.BlockSpec((1, D), lambda i, n: (0, 0)),
            pl.BlockSpec((D, tn), lambda i, n: (0, n)),
        ],
        out_specs=pl.BlockSpec((tm, tn), lambda i, n: (i, n)),
        scratch_shapes=[pltpu.VMEM((tm, D), BF16)],
        compiler_params=_params("parallel", "arbitrary"),
        name="norm_mod_matmul",
    )(x, mods, mods, norm_g.reshape(1, D), w)


def _mm_res_kernel(a_ref, w_ref, x_ref, gt_ref, o_ref):
    o_ref[...] = x_ref[...] + gt_ref[...] * _dot(a_ref[...], w_ref[...])


def _matmul_residual(a, w, x, mods, k, *, seq):
    M, K = a.shape
    N = w.shape[1]
    tm = min(1024, seq)
    tn = min(1024, N)
    bpb = seq // tm
    return pl.pallas_call(
        _mm_res_kernel,
        out_shape=jax.ShapeDtypeStruct((M, N), F32),
        grid=(M // tm, N // tn),
        in_specs=[
            pl.BlockSpec((tm, K), lambda i, n: (i, 0)),
            pl.BlockSpec((K, tn), lambda i, n: (0, n)),
            pl.BlockSpec((tm, tn), lambda i, n: (i, n)),
            pl.BlockSpec((None, None, 1, tn), lambda i, n: (k, i // bpb, 0, n)),
        ],
        out_specs=pl.BlockSpec((tm, tn), lambda i, n: (i, n)),
        compiler_params=_params("parallel", "parallel"),
        name="matmul_residual",
    )(a, w, x, mods)


def _da_attn_kernel(q_ref, k_ref, v_ref, lam_ref, g_ref, *rest, t, hd, n_heads, lambda_init,
                    cast_scaled):
    n_cast_in = len(cast_scaled) + sum(cast_scaled)
    cast_in = rest[:n_cast_in]
    o_ref = rest[n_cast_in]
    cast_out = rest[n_cast_in + 1:n_cast_in + 1 + len(cast_scaled)]
    vt_ref, bias_ref, sa_ref, sb_ref, m_ref, acc_ref = rest[n_cast_in + 1 + len(cast_scaled):]

    head = pl.program_id(1)
    qi = pl.program_id(2)
    nblk = vt_ref.shape[0]
    vd = 2 * hd
    hv = jnp.full((1, 1), head + 1, jnp.int32).astype(F32)
    slope = jnp.exp2(hv * (-8.0 / n_heads)) * LOG2E

    @pl.when(qi == 0)
    def _():
        for b in range(nblk):
            vt_ref[b, 0:vd, :] = v_ref[b * t:(b + 1) * t, :].astype(F32).T.astype(BF16)
            vt_ref[b, vd:, :] = jnp.ones((vt_ref.shape[1] - vd, t), BF16)
        r = lax.broadcasted_iota(jnp.int32, (t, t), 0)
        c = lax.broadcasted_iota(jnp.int32, (t, t), 1)
        bias_ref[0] = slope * r.astype(F32)
        bias_ref[1] = jnp.where((r // CHUNK) <= (c // CHUNK),
                                slope * (c - jnp.abs(c - r)).astype(F32), -jnp.inf)

    qts = [q_ref[:, n * hd:(n + 1) * hd].astype(F32).T.astype(BF16) for n in range(2)]
    for n in range(2):
        m_ref[n] = jnp.full((1, t), -jnp.inf, F32)
        acc_ref[n] = jnp.zeros(acc_ref.shape[1:], F32)

    def qk(kb, s_ref):
        kk = pl.multiple_of(kb * t, t)
        bias = bias_ref[(kb == qi).astype(jnp.int32)]
        for n in range(2):
            s_ref[n] = _dot(k_ref[pl.ds(kk, t), n * hd:(n + 1) * hd], qts[n]) + bias

    def softmax_pv(kb, s_ref):
        off = slope * ((kb - qi) * t).astype(F32)
        vt_b = vt_ref[kb]
        for n in range(2):
            s = s_ref[n]
            m_old = m_ref[n]
            m_new = jnp.maximum(m_old, jnp.max(s, axis=0, keepdims=True) + off)
            alpha = jnp.exp2(m_old - m_new)
            p = jnp.exp2(s - (m_new - off))
            m_ref[n] = m_new
            acc_ref[n] = alpha * acc_ref[n] + _dot(vt_b, p.astype(BF16))

    qk(0, sa_ref)

    pos = 0
    for dst_ref, has_scale in zip(cast_out, cast_scaled):
        w = cast_in[pos][...]
        if has_scale:
            w = w * cast_in[pos + 1][...]
        dst_ref[...] = w.astype(BF16)
        pos += 2 if has_scale else 1

    def body(pair, carry):
        kb = 2 * pair
        qk(kb + 1, sb_ref)
        softmax_pv(kb, sa_ref)
        qk(jnp.minimum(kb + 2, qi), sa_ref)
        softmax_pv(kb + 1, sb_ref)
        return carry

    lax.fori_loop(0, (qi + 1) // 2, body, 0)

    @pl.when(qi % 2 == 0)
    def _():
        softmax_pv(qi, sa_ref)

    lam = lam_ref[...]
    lam_val = (jnp.exp(jnp.sum(lam[0:1] * lam[1:2], axis=-1, keepdims=True))
               - jnp.exp(jnp.sum(lam[2:3] * lam[3:4], axis=-1, keepdims=True)) + lambda_init)
    heads_t = [acc_ref[n, 0:vd, :] / acc_ref[n, vd:vd + 1, :] for n in range(2)]
    o_t = heads_t[0] - lam_val * heads_t[1]
    o_t = o_t * lax.rsqrt(jnp.mean(o_t * o_t, axis=0, keepdims=True) + EPS)
    o_ref[...] = (o_t.T * (g_ref[...] * (1.0 - lambda_init))).astype(o_ref.dtype)


def _diff_attention(qkv, lam, subln_g, *, batch, seq, lambda_init, cast_jobs=()):
    M = qkv.shape[0]
    H = DA_HEADS
    hd = qkv.shape[1] // (6 * H)
    vd = 2 * hd
    t = min(512, seq)
    nq = seq // t
    n_steps = batch * H * nq
    in_specs = [
        pl.BlockSpec((t, vd), lambda b, h, i: (b * nq + i, h)),
        pl.BlockSpec((seq, vd), lambda b, h, i: (b, H + h)),
        pl.BlockSpec((seq, vd), lambda b, h, i: (b, 2 * H + h)),
        pl.BlockSpec((4, hd), lambda b, h, i: (0, 0)),
        pl.BlockSpec((1, vd), lambda b, h, i: (0, 0)),
    ]
    args = [qkv, qkv, qkv, lam, subln_g.reshape(1, vd)]
    out_shape = [jax.ShapeDtypeStruct((M, H * vd), BF16)]
    out_specs = [pl.BlockSpec((t, vd), lambda b, h, i: (b * nq + i, h))]
    scaled = []
    for arr, lead, col_scale in cast_jobs:
        rows, cols = arr.shape[-2:]
        chunk = BF16_SUBLANES * pl.cdiv(rows, BF16_SUBLANES * n_steps)
        assert rows % chunk == 0, (rows, chunk)
        n_chunks = rows // chunk

        def chunk_of(b, h, i, n_chunks=n_chunks):
            return (((b * H + h) * nq + i) * n_chunks) // n_steps

        in_specs.append(pl.BlockSpec((None,) * len(lead) + (chunk, cols),
                                     lambda b, h, i, lead=lead, f=chunk_of: (*lead, f(b, h, i), 0)))
        args.append(arr)
        if col_scale is not None:
            in_specs.append(pl.BlockSpec((1, cols), lambda b, h, i: (0, 0)))
            args.append(col_scale.reshape(1, cols))
        scaled.append(col_scale is not None)
        out_shape.append(jax.ShapeDtypeStruct((rows, cols), BF16))
        out_specs.append(pl.BlockSpec((chunk, cols), lambda b, h, i, f=chunk_of: (f(b, h, i), 0)))
    outs = pl.pallas_call(
        functools.partial(_da_attn_kernel, t=t, hd=hd, n_heads=H, lambda_init=lambda_init,
                          cast_scaled=tuple(scaled)),
        out_shape=out_shape,
        grid=(batch, H, nq),
        in_specs=in_specs,
        out_specs=out_specs,
        scratch_shapes=[pltpu.VMEM((nq, vd + BF16_SUBLANES, t), BF16),
                        pltpu.VMEM((2, t, t), F32),
                        pltpu.VMEM((2, t, t), F32), pltpu.VMEM((2, t, t), F32),
                        pltpu.VMEM((2, 1, t), F32),
                        pltpu.VMEM((2, vd + BF16_SUBLANES, t), F32)],
        compiler_params=_params("arbitrary", "arbitrary", "arbitrary"),
        name="diff_attention",
    )(*args)
    return outs[0], outs[1:]


def _ret_log_gamma(h):
    return math.log(1.0 - 2.0 ** (-5.0 - h))


def _retention_kernel(q_ref, k_ref, v_ref, g_ref, gn_ref, o_ref, state_ref, dmat_ref, qdec_ref,
                      kdec_ref, *, tb, dk, dv, n_heads):
    t = pl.program_id(1)

    @pl.when((pl.program_id(0) == 0) & (t == 0))
    def _():
        r = lax.broadcasted_iota(jnp.int32, (tb, tb), 0)
        c = lax.broadcasted_iota(jnp.int32, (tb, tb), 1)
        dist = jnp.abs(r - c).astype(F32)
        visible = (c // CHUNK) <= (r // CHUNK)
        pos = lax.broadcasted_iota(jnp.int32, (tb, dk), 0).astype(F32)
        for h in range(n_heads):
            lg = _ret_log_gamma(h)
            dmat_ref[h] = jnp.where(visible, jnp.exp(lg * dist), 0.0)
            qdec_ref[h] = jnp.exp(lg * pos)
            kdec_ref[h] = jnp.exp(lg * (tb - pos))

    @pl.when(t == 0)
    def _():
        state_ref[...] = jnp.zeros_like(state_ref)

    for h in range(n_heads):
        qh = q_ref[:, h * dk:(h + 1) * dk]
        kh = k_ref[:, h * dk:(h + 1) * dk]
        vh = v_ref[:, h * dv:(h + 1) * dv]
        inner = (_dot_nt(qh, kh) * dmat_ref[h]).astype(BF16)
        qd = (qh.astype(F32) * qdec_ref[h]).astype(BF16)
        kd = (kh.astype(F32) * kdec_ref[h]).astype(BF16)
        state = state_ref[h]
        o = _dot(inner, vh) + _dot(qd, state.astype(BF16))
        state_ref[h] = state * math.exp(_ret_log_gamma(h) * tb) + _dot_tn(kd, vh)
        mu = jnp.mean(o, axis=-1, keepdims=True)
        d = o - mu
        y = d * lax.rsqrt(jnp.mean(d * d, axis=-1, keepdims=True) + EPS) * gn_ref[...]
        gate = g_ref[:, h * dv:(h + 1) * dv].astype(F32)
        o_ref[:, h * dv:(h + 1) * dv] = (jax.nn.silu(gate) * y).astype(o_ref.dtype)


def _retention(proj, gn_g, *, batch, seq):
    M, W = proj.shape
    H = RET_HEADS
    dk = W // (6 * H)
    dv = 2 * dk
    tb = min(256, seq)
    nt = seq // tb
    row = lambda b, t: b * nt + t
    return pl.pallas_call(
        functools.partial(_retention_kernel, tb=tb, dk=dk, dv=dv, n_heads=H),
        out_shape=jax.ShapeDtypeStruct((M, H * dv), BF16),
        grid=(batch, nt),
        in_specs=[
            pl.BlockSpec((tb, H * dk), lambda b, t: (row(b, t), 0)),
            pl.BlockSpec((tb, H * dk), lambda b, t: (row(b, t), 1)),
            pl.BlockSpec((tb, H * dv), lambda b, t: (row(b, t), 1)),
            pl.BlockSpec((tb, H * dv), lambda b, t: (row(b, t), 2)),
            pl.BlockSpec((1, dv), lambda b, t: (0, 0)),
        ],
        out_specs=pl.BlockSpec((tb, H * dv), lambda b, t: (row(b, t), 0)),
        scratch_shapes=[pltpu.VMEM((H, dk, dv), F32), pltpu.VMEM((H, tb, tb), F32),
                        pltpu.VMEM((H, tb, dk), F32), pltpu.VMEM((H, tb, dk), F32)],
        compiler_params=_params("arbitrary", "arbitrary"),
        name="retention",
    )(proj, proj, proj, proj, gn_g.reshape(1, dv))


def kernel(x, c, ada_w, ada_b, norm_g, ffn_w_in, ffn_w_out, da_w_qkv, da_lambda, da_subln_g,
           da_w_o, ret_w_qkvg, ret_gn_g, ret_w_o, final_g):
    B, S, D = x.shape
    depth = ada_w.shape[0]
    mods = _ada_mods(c, ada_w, ada_b)

    da_hd = D // (2 * DA_HEADS)
    da_scale = jnp.concatenate([jnp.full((2 * DA_HEADS * da_hd,), da_hd ** -0.5 * LOG2E, F32),
                                jnp.ones((da_w_qkv.shape[2] - 2 * DA_HEADS * da_hd,), F32)])
    ret_dk = D // RET_HEADS
    ret_scale = jnp.concatenate([jnp.ones((RET_HEADS * ret_dk,), F32),
                                 jnp.full((RET_HEADS * ret_dk,), ret_dk ** -0.5, F32),
                                 jnp.ones((ret_w_qkvg.shape[2] - 2 * RET_HEADS * ret_dk,), F32)])
    wb = {
        ("ffn_in", 0, 0): ffn_w_in[0, 0].astype(BF16),
        ("ffn_out", 0, 0): ffn_w_out[0, 0].astype(BF16),
        ("da_qkv", 0): (da_w_qkv[0] * da_scale).astype(BF16),
    }
    jobs = [(("da_o", 0), da_w_o, (0,), None),
            (("ffn_in", 0, 1), ffn_w_in, (0, 1), None),
            (("ffn_out", 0, 1), ffn_w_out, (0, 1), None)]
    for i in range(1, depth):
        jobs += [(("ffn_in", i, k), ffn_w_in, (i, k), None) for k in range(2)]
        jobs += [(("ffn_out", i, k), ffn_w_out, (i, k), None) for k in range(2)]
        j = i // N_MIXERS
        if i % N_MIXERS == 0:
            jobs += [(("da_qkv", j), da_w_qkv, (j,), da_scale), (("da_o", j), da_w_o, (j,), None)]
        else:
            jobs += [(("ret", j), ret_w_qkvg, (j,), ret_scale), (("ret_o", j), ret_w_o, (j,), None)]

    xf = x.reshape(B * S, D)
    for i in range(depth):
        m = mods[i]
        xf = _ffn(xf, m, 0, norm_g[i, 0], wb["ffn_in", i, 0], wb["ffn_out", i, 0], seq=S)
        j = i // N_MIXERS
        if i % N_MIXERS == 0:
            lambda_init = 0.8 - 0.6 * math.exp(-0.3 * i)
            qkv = _norm_mod_matmul(xf, m, 3, norm_g[i, 1], wb["da_qkv", j], seq=S)
            heads, cast = _diff_attention(qkv, da_lambda[j], da_subln_g[j], batch=B, seq=S,
                                          lambda_init=lambda_init,
                                          cast_jobs=[job[1:] for job in jobs])
            wb.update({job[0]: w for job, w in zip(jobs, cast)})
            jobs = []
            xf = _matmul_residual(heads, wb["da_o", j], xf, m, 5, seq=S)
        else:
            proj = _norm_mod_matmul(xf, m, 3, norm_g[i, 1], wb["ret", j], seq=S)
            heads = _retention(proj, ret_gn_g[j], batch=B, seq=S)
            xf = _matmul_residual(heads, wb["ret_o", j], xf, m, 5, seq=S)
        xf = _ffn(xf, m, 6, norm_g[i, 2], wb["ffn_in", i, 1], wb["ffn_out", i, 1],
                  final_g=final_g if i == depth - 1 else None, seq=S)
    return xf.reshape(B, S, D)
```

```python
import functools
import math

import jax
import jax.numpy as jnp
from jax import lax
from jax.experimental import pallas as pl
from jax.experimental.pallas import tpu as pltpu

F32 = jnp.float32
BF16 = jnp.bfloat16

EPS = 1e-5
LOG2E = math.log2(math.e)
CHUNK = 64
N_MOD = 9
N_MIXERS = 2
DA_HEADS = 8
RET_HEADS = 8

BF16_SUBLANES = 16
V7X_VMEM_BYTES = 64 * 1024 * 1024
VMEM_LIMIT_BYTES = V7X_VMEM_BYTES - 8 * 1024 * 1024


def _params(*semantics):
    return pltpu.CompilerParams(dimension_semantics=semantics, vmem_limit_bytes=VMEM_LIMIT_BYTES)


def _dot(a, b):
    return jnp.dot(a, b, preferred_element_type=F32)


def _dot_nt(a, b):
    return lax.dot_general(a, b, (((1,), (1,)), ((), ())), preferred_element_type=F32)


def _dot_tn(a, b):
    return lax.dot_general(a, b, (((0,), (0,)), ((), ())), preferred_element_type=F32)


def _rms(x, g):
    return x * lax.rsqrt(jnp.mean(x * x, axis=-1, keepdims=True) + EPS) * g


NORM_ROWS = 32
FIRST_STEP_PARTS = 4


def _norm_mod_into(h_ref, x_ref, g_ref, shift_ref, scale_ref, rows=None, unrolled=False):
    gs = g_ref[...] * (1.0 + scale_ref[...])
    shift = shift_ref[...]
    start, stop = rows if rows is not None else (0, x_ref.shape[0])

    def chunk(r0):
        x = x_ref[pl.ds(r0, NORM_ROWS), :]
        rinv = lax.rsqrt(jnp.mean(x * x, axis=-1, keepdims=True) + EPS)
        h_ref[pl.ds(r0, NORM_ROWS), :] = (x * rinv * gs + shift).astype(BF16)

    if unrolled:
        for r0 in range(start, stop, NORM_ROWS):
            chunk(r0)
    else:
        def body(i, carry):
            chunk(pl.multiple_of(start + i * NORM_ROWS, NORM_ROWS))
            return carry

        lax.fori_loop(0, (stop - start) // NORM_ROWS, body, 0, unroll=4)


def _ada_kernel(c_ref, w_ref, b_ref, o_ref):
    cs = jax.nn.silu(c_ref[...]).astype(BF16)
    o_ref[...] = _dot(cs, w_ref[...].astype(BF16)) + b_ref[...]


def _ada_mods(c, ada_w, ada_b):
    B, D = c.shape
    L, _, W = ada_w.shape
    rows = 8
    tn = min(1024, W)
    c_pad = jnp.zeros((rows, D), F32).at[:B].set(c)
    out = pl.pallas_call(
        _ada_kernel,
        out_shape=jax.ShapeDtypeStruct((L, rows, W), F32),
        grid=(L, W // tn),
        in_specs=[
            pl.BlockSpec((rows, D), lambda l, n: (0, 0)),
            pl.BlockSpec((None, D, tn), lambda l, n: (l, 0, n)),
            pl.BlockSpec((None, 1, tn), lambda l, n: (l, 0, n)),
        ],
        out_specs=pl.BlockSpec((None, rows, tn), lambda l, n: (l, 0, n)),
        compiler_params=_params("parallel", "parallel"),
        name="ada_mods",
    )(c_pad, ada_w, ada_b.reshape(L, 1, W))
    return out[:, :B].reshape(L, B, N_MOD, D).transpose(0, 2, 1, 3)[:, :, :, None, :]


def _ffn_kernel(x_ref, sh_ref, sc_ref, gt_ref, ng_ref, wg_ref, wu_ref, wo_ref, *rest, final, nf):
    if final:
        fg_ref, o_ref, h_ref = rest
    else:
        o_ref, h_ref = rest
    j = pl.program_id(1)
    assert nf >= 2, "the first and last hidden-dimension steps are distinct code paths"

    def contribution(rows=slice(None)):
        h = h_ref[rows, :]
        a = _dot(h, wg_ref[...])
        b = _dot(h, wu_ref[...])
        act = (jax.nn.silu(a) * b).astype(BF16)
        return _dot(act, wo_ref[...])

    @pl.when(j == 0)
    def _():
        part = x_ref.shape[0] // FIRST_STEP_PARTS
        _norm_mod_into(h_ref, x_ref, ng_ref, sh_ref, sc_ref, rows=(0, part))
        for q in range(FIRST_STEP_PARTS):
            if q + 1 < FIRST_STEP_PARTS:
                _norm_mod_into(h_ref, x_ref, ng_ref, sh_ref, sc_ref,
                               rows=((q + 1) * part, (q + 2) * part), unrolled=True)
            o_ref[q * part:(q + 1) * part, :] = contribution(slice(q * part, (q + 1) * part))

    @pl.when((j > 0) & (j < nf - 1))
    def _():
        o_ref[...] += contribution()

    @pl.when(j == nf - 1)
    def _():
        o_ref[...] = x_ref[...] + (0.5 * gt_ref[...]) * (o_ref[...] + contribution())
        if final:
            for r0 in range(0, x_ref.shape[0], NORM_ROWS):
                o_ref[r0:r0 + NORM_ROWS, :] = _rms(o_ref[r0:r0 + NORM_ROWS, :], fg_ref[...])


def _ffn(x, mods, k0, norm_g, w_in, w_out, final_g=None, *, seq):
    M, D = x.shape
    F = w_out.shape[0]
    tm = min(1024, seq)
    tf = min(512, F)
    nf = F // tf
    bpb = seq // tm
    vec = lambda k: pl.BlockSpec((None, None, 1, D), lambda i, j: (k, i // bpb, 0, 0))
    in_specs = [
        pl.BlockSpec((tm, D), lambda i, j: (i, 0)),
        vec(k0), vec(k0 + 1), vec(k0 + 2),
        pl.BlockSpec((1, D), lambda i, j: (0, 0)),
        pl.BlockSpec((D, tf), lambda i, j: (0, j)),
        pl.BlockSpec((D, tf), lambda i, j: (0, nf + j)),
        pl.BlockSpec((tf, D), lambda i, j: (j, 0)),
    ]
    args = [x, mods, mods, mods, norm_g.reshape(1, D), w_in, w_in, w_out]
    if final_g is not None:
        in_specs.append(pl.BlockSpec((1, D), lambda i, j: (0, 0)))
        args.append(final_g.reshape(1, D))
    return pl.pallas_call(
        functools.partial(_ffn_kernel, final=final_g is not None, nf=nf),
        out_shape=jax.ShapeDtypeStruct((M, D), F32),
        grid=(M // tm, nf),
        in_specs=in_specs,
        out_specs=pl.BlockSpec((tm, D), lambda i, j: (i, 0)),
        scratch_shapes=[pltpu.VMEM((tm, D), BF16)],
        compiler_params=_params("parallel", "arbitrary"),
        name="ffn",
    )(*args)


def _nm_matmul_kernel(x_ref, sh_ref, sc_ref, ng_ref, w_ref, o_ref, h_ref):
    n = pl.program_id(1)

    @pl.when(n == 0)
    def _():
        part = x_ref.shape[0] // FIRST_STEP_PARTS
        _norm_mod_into(h_ref, x_ref, ng_ref, sh_ref, sc_ref, rows=(0, part))
        for q in range(FIRST_STEP_PARTS):
            if q + 1 < FIRST_STEP_PARTS:
                _norm_mod_into(h_ref, x_ref, ng_ref, sh_ref, sc_ref,
                               rows=((q + 1) * part, (q + 2) * part), unrolled=True)
            rows = slice(q * part, (q + 1) * part)
            o_ref[rows, :] = _dot(h_ref[rows, :], w_ref[...]).astype(o_ref.dtype)

    @pl.when(n > 0)
    def _():
        o_ref[...] = _dot(h_ref[...], w_ref[...]).astype(o_ref.dtype)


def _norm_mod_matmul(x, mods, k0, norm_g, w, *, seq):
    M, D = x.shape
    N = w.shape[1]
    tm = min(1024, seq)
    tn = min(2048, N)
    bpb = seq // tm
    vec = lambda k: pl.BlockSpec((None, None, 1, D), lambda i, n: (k, i // bpb, 0, 0))
    return pl.pallas_call(
        _nm_matmul_kernel,
        out_shape=jax.ShapeDtypeStruct((M, N), BF16),
        grid=(M // tm, N // tn),
        in_specs=[
            pl.BlockSpec((tm, D), lambda i, n: (i, 0)),
            vec(k0), vec(k0 + 1),
            pl.BlockSpec((1, D), lambda i, n: (0, 0)),
            pl.BlockSpec((D, tn), lambda i, n: (0, n)),
        ],
        out_specs=pl.BlockSpec((tm, tn), lambda i, n: (i, n)),
        scratch_shapes=[pltpu.VMEM((tm, D), BF16)],
        compiler_params=_params("parallel", "arbitrary"),
        name="norm_mod_matmul",
    )(x, mods, mods, norm_g.reshape(1, D), w)


def _mm_res_kernel(a_ref, w_ref, x_ref, gt_ref, o_ref):
    o_ref[...] = x_ref[...] + gt_ref[...] * _dot(a_ref[...], w_ref[...])


def _matmul_residual(a, w, x, mods, k, *, seq):
    M, K = a.shape
    N = w.shape[1]
    tm = min(1024, seq)
    tn = min(1024, N)
    bpb = seq // tm
    return pl.pallas_call(
        _mm_res_kernel,
        out_shape=jax.ShapeDtypeStruct((M, N), F32),
        grid=(M // tm, N // tn),
        in_specs=[
            pl.BlockSpec((tm, K), lambda i, n: (i, 0)),
            pl.BlockSpec((K, tn), lambda i, n: (0, n)),
            pl.BlockSpec((tm, tn), lambda i, n: (i, n)),
            pl.BlockSpec((None, None, 1, tn), lambda i, n: (k, i // bpb, 0, n)),
        ],
        out_specs=pl.BlockSpec((tm, tn), lambda i, n: (i, n)),
        compiler_params=_params("parallel", "parallel"),
        name="matmul_residual",
    )(a, w, x, mods)


def _da_attn_kernel(q_ref, k_ref, v_ref, lam_ref, g_ref, *rest, t, hd, n_heads, lambda_init,
                    cast_scaled, cast_chunks):
    n_cast_in = len(cast_scaled) + sum(cast_scaled)
    cast_in = rest[:n_cast_in]
    o_ref = rest[n_cast_in]
    cast_out = rest[n_cast_in + 1:n_cast_in + 1 + len(cast_scaled)]
    vt_ref, bias_ref, sa_ref, sb_ref, m_ref, acc_ref = rest[n_cast_in + 1 + len(cast_scaled):]

    head = pl.program_id(1)
    qi = pl.program_id(2)
    nblk = vt_ref.shape[0]
    vd = 2 * hd
    hv = jnp.full((1, 1), head + 1, jnp.int32).astype(F32)
    slope = jnp.exp2(hv * (-8.0 / n_heads)) * LOG2E

    @pl.when(qi == 0)
    def _():
        for b in range(nblk):
            vt_ref[b, 0:vd, :] = v_ref[b * t:(b + 1) * t, :].astype(F32).T.astype(BF16)
            vt_ref[b, vd:, :] = jnp.ones((vt_ref.shape[1] - vd, t), BF16)
        r = lax.broadcasted_iota(jnp.int32, (t, t), 0)
        c = lax.broadcasted_iota(jnp.int32, (t, t), 1)
        bias_ref[0] = slope * r.astype(F32)
        bias_ref[1] = jnp.where((r // CHUNK) <= (c // CHUNK),
                                slope * (c - jnp.abs(c - r)).astype(F32), -jnp.inf)

    qts = [q_ref[:, n * hd:(n + 1) * hd].astype(F32).T.astype(BF16) for n in range(2)]
    for n in range(2):
        m_ref[n] = jnp.full((1, t), -jnp.inf, F32)
        acc_ref[n] = jnp.zeros(acc_ref.shape[1:], F32)

    def qk(kb, s_ref):
        kk = pl.multiple_of(kb * t, t)
        bias = bias_ref[(kb == qi).astype(jnp.int32)]
        for n in range(2):
            s_ref[n] = _dot(k_ref[pl.ds(kk, t), n * hd:(n + 1) * hd], qts[n]) + bias

    def softmax_pv(kb, s_ref):
        off = slope * ((kb - qi) * t).astype(F32)
        vt_b = vt_ref[kb]
        for n in range(2):
            s = s_ref[n]
            m_old = m_ref[n]
            m_new = jnp.maximum(m_old, jnp.max(s, axis=0, keepdims=True) + off)
            alpha = jnp.exp2(m_old - m_new)
            p = jnp.exp2(s - (m_new - off))
            m_ref[n] = m_new
            acc_ref[n] = alpha * acc_ref[n] + _dot(vt_b, p.astype(BF16))

    qk(0, sa_ref)

    step = (pl.program_id(0) * pl.num_programs(1) + head) * pl.num_programs(2) + qi
    n_steps = pl.num_programs(0) * pl.num_programs(1) * pl.num_programs(2)
    pos = 0
    for dst_ref, has_scale, n_chunks in zip(cast_out, cast_scaled, cast_chunks):
        src_ref = cast_in[pos]
        scale_ref = cast_in[pos + 1] if has_scale else None
        pos += 2 if has_scale else 1
        moved = (step == 0) | ((step * n_chunks) // n_steps != ((step - 1) * n_chunks) // n_steps)

        @pl.when(moved)
        def _(dst_ref=dst_ref, src_ref=src_ref, scale_ref=scale_ref):
            w = src_ref[...]
            if scale_ref is not None:
                w = w * scale_ref[...]
            dst_ref[...] = w.astype(BF16)

    def body(pair, carry):
        kb = 2 * pair
        qk(kb + 1, sb_ref)
        softmax_pv(kb, sa_ref)
        qk(jnp.minimum(kb + 2, qi), sa_ref)
        softmax_pv(kb + 1, sb_ref)
        return carry

    lax.fori_loop(0, (qi + 1) // 2, body, 0)

    @pl.when(qi % 2 == 0)
    def _():
        softmax_pv(qi, sa_ref)

    lam = lam_ref[...]
    lam_val = (jnp.exp(jnp.sum(lam[0:1] * lam[1:2], axis=-1, keepdims=True))
               - jnp.exp(jnp.sum(lam[2:3] * lam[3:4], axis=-1, keepdims=True)) + lambda_init)
    heads_t = [acc_ref[n, 0:vd, :] / acc_ref[n, vd:vd + 1, :] for n in range(2)]
    o_t = heads_t[0] - lam_val * heads_t[1]
    o_t = o_t * lax.rsqrt(jnp.mean(o_t * o_t, axis=0, keepdims=True) + EPS)
    o_ref[...] = (o_t.T * (g_ref[...] * (1.0 - lambda_init))).astype(o_ref.dtype)


def _diff_attention(qkv, lam, subln_g, *, batch, seq, lambda_init, cast_jobs=()):
    M = qkv.shape[0]
    H = DA_HEADS
    hd = qkv.shape[1] // (6 * H)
    vd = 2 * hd
    t = min(512, seq)
    nq = seq // t
    n_steps = batch * H * nq
    in_specs = [
        pl.BlockSpec((t, vd), lambda b, h, i: (b * nq + i, h)),
        pl.BlockSpec((seq, vd), lambda b, h, i: (b, H + h)),
        pl.BlockSpec((seq, vd), lambda b, h, i: (b, 2 * H + h)),
        pl.BlockSpec((4, hd), lambda b, h, i: (0, 0)),
        pl.BlockSpec((1, vd), lambda b, h, i: (0, 0)),
    ]
    args = [qkv, qkv, qkv, lam, subln_g.reshape(1, vd)]
    out_shape = [jax.ShapeDtypeStruct((M, H * vd), BF16)]
    out_specs = [pl.BlockSpec((t, vd), lambda b, h, i: (b * nq + i, h))]
    scaled, chunks = [], []
    for arr, lead, col_scale in cast_jobs:
        rows, cols = arr.shape[-2:]
        chunk = BF16_SUBLANES * pl.cdiv(rows, BF16_SUBLANES * n_steps)
        assert rows % chunk == 0, (rows, chunk)
        n_chunks = rows // chunk

        def chunk_of(b, h, i, n_chunks=n_chunks):
            return (((b * H + h) * nq + i) * n_chunks) // n_steps

        in_specs.append(pl.BlockSpec((None,) * len(lead) + (chunk, cols),
                                     lambda b, h, i, lead=lead, f=chunk_of: (*lead, f(b, h, i), 0)))
        args.append(arr)
        if col_scale is not None:
            in_specs.append(pl.BlockSpec((1, cols), lambda b, h, i: (0, 0)))
            args.append(col_scale.reshape(1, cols))
        scaled.append(col_scale is not None)
        chunks.append(n_chunks)
        out_shape.append(jax.ShapeDtypeStruct((rows, cols), BF16))
        out_specs.append(pl.BlockSpec((chunk, cols), lambda b, h, i, f=chunk_of: (f(b, h, i), 0)))
    outs = pl.pallas_call(
        functools.partial(_da_attn_kernel, t=t, hd=hd, n_heads=H, lambda_init=lambda_init,
                          cast_scaled=tuple(scaled), cast_chunks=tuple(chunks)),
        out_shape=out_shape,
        grid=(batch, H, nq),
        in_specs=in_specs,
        out_specs=out_specs,
        scratch_shapes=[pltpu.VMEM((nq, vd + BF16_SUBLANES, t), BF16),
                        pltpu.VMEM((2, t, t), F32),
                        pltpu.VMEM((2, t, t), F32), pltpu.VMEM((2, t, t), F32),
                        pltpu.VMEM((2, 1, t), F32),
                        pltpu.VMEM((2, vd + BF16_SUBLANES, t), F32)],
        compiler_params=_params("arbitrary", "arbitrary", "arbitrary"),
        name="diff_attention",
    )(*args)
    return outs[0], outs[1:]


def _ret_log_gamma(h):
    return math.log(1.0 - 2.0 ** (-5.0 - h))


def _retention_kernel(q_ref, k_ref, v_ref, g_ref, gn_ref, o_ref, state_ref, dmat_ref, qdec_ref,
                      kdec_ref, *, tb, dk, dv, n_heads):
    t = pl.program_id(1)

    @pl.when((pl.program_id(0) == 0) & (t == 0))
    def _():
        r = lax.broadcasted_iota(jnp.int32, (tb, tb), 0)
        c = lax.broadcasted_iota(jnp.int32, (tb, tb), 1)
        dist = jnp.abs(r - c).astype(F32)
        visible = (c // CHUNK) <= (r // CHUNK)
        pos = lax.broadcasted_iota(jnp.int32, (tb, dk), 0).astype(F32)
        for h in range(n_heads):
            lg = _ret_log_gamma(h)
            dmat_ref[h] = jnp.where(visible, jnp.exp(lg * dist), 0.0)
            qdec_ref[h] = jnp.exp(lg * pos)
            kdec_ref[h] = jnp.exp(lg * (tb - pos))

    @pl.when(t == 0)
    def _():
        state_ref[...] = jnp.zeros_like(state_ref)

    for h in range(n_heads):
        qh = q_ref[:, h * dk:(h + 1) * dk]
        kh = k_ref[:, h * dk:(h + 1) * dk]
        vh = v_ref[:, h * dv:(h + 1) * dv]
        inner = (_dot_nt(qh, kh) * dmat_ref[h]).astype(BF16)
        qd = (qh.astype(F32) * qdec_ref[h]).astype(BF16)
        kd = (kh.astype(F32) * kdec_ref[h]).astype(BF16)
        state = state_ref[h]
        o = _dot(inner, vh) + _dot(qd, state.astype(BF16))
        state_ref[h] = state * math.exp(_ret_log_gamma(h) * tb) + _dot_tn(kd, vh)
        mu = jnp.mean(o, axis=-1, keepdims=True)
        d = o - mu
        y = d * lax.rsqrt(jnp.mean(d * d, axis=-1, keepdims=True) + EPS) * gn_ref[...]
        gate = g_ref[:, h * dv:(h + 1) * dv].astype(F32)
        o_ref[:, h * dv:(h + 1) * dv] = (jax.nn.silu(gate) * y).astype(o_ref.dtype)


def _retention(proj, gn_g, *, batch, seq):
    M, W = proj.shape
    H = RET_HEADS
    dk = W // (6 * H)
    dv = 2 * dk
    tb = min(256, seq)
    nt = seq // tb
    row = lambda b, t: b * nt + t
    return pl.pallas_call(
        functools.partial(_retention_kernel, tb=tb, dk=dk, dv=dv, n_heads=H),
        out_shape=jax.ShapeDtypeStruct((M, H * dv), BF16),
        grid=(batch, nt),
        in_specs=[
            pl.BlockSpec((tb, H * dk), lambda b, t: (row(b, t), 0)),
            pl.BlockSpec((tb, H * dk), lambda b, t: (row(b, t), 1)),
            pl.BlockSpec((tb, H * dv), lambda b, t: (row(b, t), 1)),
            pl.BlockSpec((tb, H * dv), lambda b, t: (row(b, t), 2)),
            pl.BlockSpec((1, dv), lambda b, t: (0, 0)),
        ],
        out_specs=pl.BlockSpec((tb, H * dv), lambda b, t: (row(b, t), 0)),
        scratch_shapes=[pltpu.VMEM((H, dk, dv), F32), pltpu.VMEM((H, tb, tb), F32),
                        pltpu.VMEM((H, tb, dk), F32), pltpu.VMEM((H, tb, dk), F32)],
        compiler_params=_params("arbitrary", "arbitrary"),
        name="retention",
    )(proj, proj, proj, proj, gn_g.reshape(1, dv))


def kernel(x, c, ada_w, ada_b, norm_g, ffn_w_in, ffn_w_out, da_w_qkv, da_lambda, da_subln_g,
           da_w_o, ret_w_qkvg, ret_gn_g, ret_w_o, final_g):
    B, S, D = x.shape
    depth = ada_w.shape[0]
    mods = _ada_mods(c, ada_w, ada_b)

    da_hd = D // (2 * DA_HEADS)
    da_scale = jnp.concatenate([jnp.full((2 * DA_HEADS * da_hd,), da_hd ** -0.5 * LOG2E, F32),
                                jnp.ones((da_w_qkv.shape[2] - 2 * DA_HEADS * da_hd,), F32)])
    ret_dk = D // RET_HEADS
    ret_scale = jnp.concatenate([jnp.ones((RET_HEADS * ret_dk,), F32),
                                 jnp.full((RET_HEADS * ret_dk,), ret_dk ** -0.5, F32),
                                 jnp.ones((ret_w_qkvg.shape[2] - 2 * RET_HEADS * ret_dk,), F32)])
    wb = {
        ("ffn_in", 0, 0): ffn_w_in[0, 0].astype(BF16),
        ("ffn_out", 0, 0): ffn_w_out[0, 0].astype(BF16),
        ("da_qkv", 0): (da_w_qkv[0] * da_scale).astype(BF16),
    }
    jobs = [(("da_o", 0), da_w_o, (0,), None),
            (("ffn_in", 0, 1), ffn_w_in, (0, 1), None),
            (("ffn_out", 0, 1), ffn_w_out, (0, 1), None)]
    for i in range(1, depth):
        jobs += [(("ffn_in", i, k), ffn_w_in, (i, k), None) for k in range(2)]
        jobs += [(("ffn_out", i, k), ffn_w_out, (i, k), None) for k in range(2)]
        j = i // N_MIXERS
        if i % N_MIXERS == 0:
            jobs += [(("da_qkv", j), da_w_qkv, (j,), da_scale), (("da_o", j), da_w_o, (j,), None)]
        else:
            jobs += [(("ret", j), ret_w_qkvg, (j,), ret_scale), (("ret_o", j), ret_w_o, (j,), None)]

    xf = x.reshape(B * S, D)
    for i in range(depth):
        m = mods[i]
        xf = _ffn(xf, m, 0, norm_g[i, 0], wb["ffn_in", i, 0], wb["ffn_out", i, 0], seq=S)
        j = i // N_MIXERS
        if i % N_MIXERS == 0:
            lambda_init = 0.8 - 0.6 * math.exp(-0.3 * i)
            qkv = _norm_mod_matmul(xf, m, 3, norm_g[i, 1], wb["da_qkv", j], seq=S)
            heads, cast = _diff_attention(qkv, da_lambda[j], da_subln_g[j], batch=B, seq=S,
                                          lambda_init=lambda_init,
                                          cast_jobs=[job[1:] for job in jobs])
            wb.update({job[0]: w for job, w in zip(jobs, cast)})
            jobs = []
            xf = _matmul_residual(heads, wb["da_o", j], xf, m, 5, seq=S)
        else:
            proj = _norm_mod_matmul(xf, m, 3, norm_g[i, 1], wb["ret", j], seq=S)
            heads = _retention(proj, ret_gn_g[j], batch=B, seq=S)
            xf = _matmul_residual(heads, wb["ret_o", j], xf, m, 5, seq=S)
        xf = _ffn(xf, m, 6, norm_g[i, 2], wb["ffn_in", i, 1], wb["ffn_out", i, 1],
                  final_g=final_g if i == depth - 1 else None, seq=S)
    return xf.reshape(B, S, D)
```

```python
import functools
import math

import jax
import jax.numpy as jnp
from jax import lax
from jax.experimental import pallas as pl
from jax.experimental.pallas import tpu as pltpu

F32 = jnp.float32
BF16 = jnp.bfloat16

EPS = 1e-5
LOG2E = math.log2(math.e)
CHUNK = 64
N_MOD = 9
N_MIXERS = 2
DA_HEADS = 8
RET_HEADS = 8

BF16_SUBLANES = 16
V7X_VMEM_BYTES = 64 * 1024 * 1024
VMEM_LIMIT_BYTES = V7X_VMEM_BYTES - 8 * 1024 * 1024


def _params(*semantics):
    return pltpu.CompilerParams(dimension_semantics=semantics, vmem_limit_bytes=VMEM_LIMIT_BYTES)


def _dot(a, b):
    return jnp.dot(a, b, preferred_element_type=F32)


def _dot_nt(a, b):
    return lax.dot_general(a, b, (((1,), (1,)), ((), ())), preferred_element_type=F32)


def _dot_tn(a, b):
    return lax.dot_general(a, b, (((0,), (0,)), ((), ())), preferred_element_type=F32)


def _rms(x, g):
    return x * lax.rsqrt(jnp.mean(x * x, axis=-1, keepdims=True) + EPS) * g


NORM_ROWS = 32
FIRST_STEP_PARTS = 4


def _norm_mod_into(h_ref, x_ref, g_ref, shift_ref, scale_ref, rows=None, unrolled=False):
    gs = g_ref[...] * (1.0 + scale_ref[...])
    shift = shift_ref[...]
    start, stop = rows if rows is not None else (0, x_ref.shape[0])

    def chunk(r0):
        x = x_ref[pl.ds(r0, NORM_ROWS), :]
        rinv = lax.rsqrt(jnp.mean(x * x, axis=-1, keepdims=True) + EPS)
        h_ref[pl.ds(r0, NORM_ROWS), :] = (x * rinv * gs + shift).astype(BF16)

    if unrolled:
        for r0 in range(start, stop, NORM_ROWS):
            chunk(r0)
    else:
        def body(i, carry):
            chunk(pl.multiple_of(start + i * NORM_ROWS, NORM_ROWS))
            return carry

        lax.fori_loop(0, (stop - start) // NORM_ROWS, body, 0, unroll=4)


def _ada_kernel(c_ref, w_ref, b_ref, o_ref):
    cs = jax.nn.silu(c_ref[...]).astype(BF16)
    o_ref[...] = _dot(cs, w_ref[...].astype(BF16)) + b_ref[...]


def _ada_mods(c, ada_w, ada_b):
    B, D = c.shape
    L, _, W = ada_w.shape
    rows = 8
    tn = min(1024, W)
    c_pad = jnp.zeros((rows, D), F32).at[:B].set(c)
    out = pl.pallas_call(
        _ada_kernel,
        out_shape=jax.ShapeDtypeStruct((L, rows, W), F32),
        grid=(L, W // tn),
        in_specs=[
            pl.BlockSpec((rows, D), lambda l, n: (0, 0)),
            pl.BlockSpec((None, D, tn), lambda l, n: (l, 0, n)),
            pl.BlockSpec((None, 1, tn), lambda l, n: (l, 0, n)),
        ],
        out_specs=pl.BlockSpec((None, rows, tn), lambda l, n: (l, 0, n)),
        compiler_params=_params("parallel", "parallel"),
        name="ada_mods",
    )(c_pad, ada_w, ada_b.reshape(L, 1, W))
    return out[:, :B].reshape(L, B, N_MOD, D).transpose(0, 2, 1, 3)[:, :, :, None, :]


def _ffn_kernel(x_ref, sh_ref, sc_ref, gt_ref, ng_ref, wg_ref, wu_ref, wo_ref, *rest, final, nf):
    if final:
        fg_ref, o_ref, h_ref = rest
    else:
        o_ref, h_ref = rest
    j = pl.program_id(1)
    assert nf >= 2, "the first and last hidden-dimension steps are distinct code paths"

    def contribution(rows=slice(None)):
        h = h_ref[rows, :]
        a = _dot(h, wg_ref[...])
        b = _dot(h, wu_ref[...])
        act = (jax.nn.silu(a) * b).astype(BF16)
        return _dot(act, wo_ref[...])

    @pl.when(j == 0)
    def _():
        part = x_ref.shape[0] // FIRST_STEP_PARTS
        _norm_mod_into(h_ref, x_ref, ng_ref, sh_ref, sc_ref, rows=(0, part))
        for q in range(FIRST_STEP_PARTS):
            if q + 1 < FIRST_STEP_PARTS:
                _norm_mod_into(h_ref, x_ref, ng_ref, sh_ref, sc_ref,
                               rows=((q + 1) * part, (q + 2) * part), unrolled=True)
            o_ref[q * part:(q + 1) * part, :] = contribution(slice(q * part, (q + 1) * part))

    @pl.when((j > 0) & (j < nf - 1))
    def _():
        o_ref[...] += contribution()

    @pl.when(j == nf - 1)
    def _():
        o_ref[...] = x_ref[...] + (0.5 * gt_ref[...]) * (o_ref[...] + contribution())
        if final:
            for r0 in range(0, x_ref.shape[0], NORM_ROWS):
                o_ref[r0:r0 + NORM_ROWS, :] = _rms(o_ref[r0:r0 + NORM_ROWS, :], fg_ref[...])


def _ffn(x, mods, k0, norm_g, w_in, w_out, final_g=None, *, seq):
    M, D = x.shape
    F = w_out.shape[0]
    tm = min(1024, seq)
    tf = min(512, F)
    nf = F // tf
    bpb = seq // tm
    vec = lambda k: pl.BlockSpec((None, None, 1, D), lambda i, j: (k, i // bpb, 0, 0))
    in_specs = [
        pl.BlockSpec((tm, D), lambda i, j: (i, 0)),
        vec(k0), vec(k0 + 1), vec(k0 + 2),
        pl.BlockSpec((1, D), lambda i, j: (0, 0)),
        pl.BlockSpec((D, tf), lambda i, j: (0, j)),
        pl.BlockSpec((D, tf), lambda i, j: (0, nf + j)),
        pl.BlockSpec((tf, D), lambda i, j: (j, 0)),
    ]
    args = [x, mods, mods, mods, norm_g.reshape(1, D), w_in, w_in, w_out]
    if final_g is not None:
        in_specs.append(pl.BlockSpec((1, D), lambda i, j: (0, 0)))
        args.append(final_g.reshape(1, D))
    return pl.pallas_call(
        functools.partial(_ffn_kernel, final=final_g is not None, nf=nf),
        out_shape=jax.ShapeDtypeStruct((M, D), F32),
        grid=(M // tm, nf),
        in_specs=in_specs,
        out_specs=pl.BlockSpec((tm, D), lambda i, j: (i, 0)),
        scratch_shapes=[pltpu.VMEM((tm, D), BF16)],
        compiler_params=_params("parallel", "arbitrary"),
        name="ffn",
    )(*args)


def _nm_matmul_kernel(x_ref, sh_ref, sc_ref, ng_ref, w_ref, o_ref, h_ref):
    n = pl.program_id(1)

    @pl.when(n == 0)
    def _():
        part = x_ref.shape[0] // FIRST_STEP_PARTS
        _norm_mod_into(h_ref, x_ref, ng_ref, sh_ref, sc_ref, rows=(0, part))
        for q in range(FIRST_STEP_PARTS):
            if q + 1 < FIRST_STEP_PARTS:
                _norm_mod_into(h_ref, x_ref, ng_ref, sh_ref, sc_ref,
                               rows=((q + 1) * part, (q + 2) * part), unrolled=True)
            rows = slice(q * part, (q + 1) * part)
            o_ref[rows, :] = _dot(h_ref[rows, :], w_ref[...]).astype(o_ref.dtype)

    @pl.when(n > 0)
    def _():
        o_ref[...] = _dot(h_ref[...], w_ref[...]).astype(o_ref.dtype)


def _norm_mod_matmul(x, mods, k0, norm_g, w, *, seq):
    M, D = x.shape
    N = w.shape[1]
    tm = min(1024, seq)
    tn = min(2048, N)
    bpb = seq // tm
    vec = lambda k: pl.BlockSpec((None, None, 1, D), lambda i, n: (k, i // bpb, 0, 0))
    return pl.pallas_call(
        _nm_matmul_kernel,
        out_shape=jax.ShapeDtypeStruct((M, N), BF16),
        grid=(M // tm, N // tn),
        in_specs=[
            pl.BlockSpec((tm, D), lambda i, n: (i, 0)),
            vec(k0), vec(k0 + 1),
            pl.BlockSpec((1, D), lambda i, n: (0, 0)),
            pl.BlockSpec((D, tn), lambda i, n: (0, n)),
        ],
        out_specs=pl.BlockSpec((tm, tn), lambda i, n: (i, n)),
        scratch_shapes=[pltpu.VMEM((tm, D), BF16)],
        compiler_params=_params("parallel", "arbitrary"),
        name="norm_mod_matmul",
    )(x, mods, mods, norm_g.reshape(1, D), w)


def _mm_res_kernel(a_ref, w_ref, x_ref, gt_ref, o_ref):
    o_ref[...] = x_ref[...] + gt_ref[...] * _dot(a_ref[...], w_ref[...])


def _matmul_residual(a, w, x, mods, k, *, seq):
    M, K = a.shape
    N = w.shape[1]
    if 2 * K * N * w.dtype.itemsize <= V7X_VMEM_BYTES // 4:
        tm, tn = min(512, seq), N
    else:
        tm, tn = min(1024, seq), min(1024, N)
    bpb = seq // tm
    return pl.pallas_call(
        _mm_res_kernel,
        out_shape=jax.ShapeDtypeStruct((M, N), F32),
        grid=(M // tm, N // tn),
        in_specs=[
            pl.BlockSpec((tm, K), lambda i, n: (i, 0)),
            pl.BlockSpec((K, tn), lambda i, n: (0, n)),
            pl.BlockSpec((tm, tn), lambda i, n: (i, n)),
            pl.BlockSpec((None, None, 1, tn), lambda i, n: (k, i // bpb, 0, n)),
        ],
        out_specs=pl.BlockSpec((tm, tn), lambda i, n: (i, n)),
        compiler_params=_params("parallel", "parallel"),
        name="matmul_residual",
    )(a, w, x, mods)


def _da_attn_kernel(q_ref, k_ref, v_ref, lam_ref, g_ref, *rest, t, hd, n_heads, lambda_init,
                    cast_scaled, cast_chunks):
    n_cast_in = len(cast_scaled) + sum(cast_scaled)
    cast_in = rest[:n_cast_in]
    o_ref = rest[n_cast_in]
    cast_out = rest[n_cast_in + 1:n_cast_in + 1 + len(cast_scaled)]
    vt_ref, bias_ref, sa_ref, sb_ref, m_ref, acc_ref = rest[n_cast_in + 1 + len(cast_scaled):]

    head = pl.program_id(1)
    qi = pl.program_id(2)
    nblk = vt_ref.shape[0]
    vd = 2 * hd
    hv = jnp.full((1, 1), head + 1, jnp.int32).astype(F32)
    slope = jnp.exp2(hv * (-8.0 / n_heads)) * LOG2E

    @pl.when(qi == 0)
    def _():
        for b in range(nblk):
            vt_ref[b, 0:vd, :] = v_ref[b * t:(b + 1) * t, :].astype(F32).T.astype(BF16)
            vt_ref[b, vd:, :] = jnp.ones((vt_ref.shape[1] - vd, t), BF16)
        r = lax.broadcasted_iota(jnp.int32, (t, t), 0)
        c = lax.broadcasted_iota(jnp.int32, (t, t), 1)
        bias_ref[0] = slope * r.astype(F32)
        bias_ref[1] = jnp.where((r // CHUNK) <= (c // CHUNK),
                                slope * (c - jnp.abs(c - r)).astype(F32), -jnp.inf)

    qts = [q_ref[:, n * hd:(n + 1) * hd].astype(F32).T.astype(BF16) for n in range(2)]
    for n in range(2):
        m_ref[n] = jnp.full((1, t), -jnp.inf, F32)
        acc_ref[n] = jnp.zeros(acc_ref.shape[1:], F32)

    def qk(kb, s_ref):
        kk = pl.multiple_of(kb * t, t)
        bias = bias_ref[(kb == qi).astype(jnp.int32)]
        for n in range(2):
            s_ref[n] = _dot(k_ref[pl.ds(kk, t), n * hd:(n + 1) * hd], qts[n]) + bias

    def softmax_pv(kb, s_ref):
        off = slope * ((kb - qi) * t).astype(F32)
        vt_b = vt_ref[kb]
        for n in range(2):
            s = s_ref[n]
            m_old = m_ref[n]
            m_new = jnp.maximum(m_old, jnp.max(s, axis=0, keepdims=True) + off)
            alpha = jnp.exp2(m_old - m_new)
            p = jnp.exp2(s - (m_new - off))
            m_ref[n] = m_new
            acc_ref[n] = alpha * acc_ref[n] + _dot(vt_b, p.astype(BF16))

    qk(0, sa_ref)

    step = (pl.program_id(0) * pl.num_programs(1) + head) * pl.num_programs(2) + qi
    n_steps = pl.num_programs(0) * pl.num_programs(1) * pl.num_programs(2)
    pos = 0
    for dst_ref, has_scale, n_chunks in zip(cast_out, cast_scaled, cast_chunks):
        src_ref = cast_in[pos]
        scale_ref = cast_in[pos + 1] if has_scale else None
        pos += 2 if has_scale else 1
        moved = (step == 0) | ((step * n_chunks) // n_steps != ((step - 1) * n_chunks) // n_steps)

        @pl.when(moved)
        def _(dst_ref=dst_ref, src_ref=src_ref, scale_ref=scale_ref):
            w = src_ref[...]
            if scale_ref is not None:
                w = w * scale_ref[...]
            dst_ref[...] = w.astype(BF16)

    def body(pair, carry):
        kb = 2 * pair
        qk(kb + 1, sb_ref)
        softmax_pv(kb, sa_ref)
        qk(jnp.minimum(kb + 2, qi), sa_ref)
        softmax_pv(kb + 1, sb_ref)
        return carry

    lax.fori_loop(0, (qi + 1) // 2, body, 0)

    @pl.when(qi % 2 == 0)
    def _():
        softmax_pv(qi, sa_ref)

    lam = lam_ref[...]
    lam_val = (jnp.exp(jnp.sum(lam[0:1] * lam[1:2], axis=-1, keepdims=True))
               - jnp.exp(jnp.sum(lam[2:3] * lam[3:4], axis=-1, keepdims=True)) + lambda_init)
    heads_t = [acc_ref[n, 0:vd, :] / acc_ref[n, vd:vd + 1, :] for n in range(2)]
    o_t = heads_t[0] - lam_val * heads_t[1]
    o_t = o_t * lax.rsqrt(jnp.mean(o_t * o_t, axis=0, keepdims=True) + EPS)
    o_ref[...] = (o_t.T * (g_ref[...] * (1.0 - lambda_init))).astype(o_ref.dtype)


def _diff_attention(qkv, lam, subln_g, *, batch, seq, lambda_init, cast_jobs=()):
    M = qkv.shape[0]
    H = DA_HEADS
    hd = qkv.shape[1] // (6 * H)
    vd = 2 * hd
    t = min(512, seq)
    nq = seq // t
    n_steps = batch * H * nq
    in_specs = [
        pl.BlockSpec((t, vd), lambda b, h, i: (b * nq + i, h)),
        pl.BlockSpec((seq, vd), lambda b, h, i: (b, H + h)),
        pl.BlockSpec((seq, vd), lambda b, h, i: (b, 2 * H + h)),
        pl.BlockSpec((4, hd), lambda b, h, i: (0, 0)),
        pl.BlockSpec((1, vd), lambda b, h, i: (0, 0)),
    ]
    args = [qkv, qkv, qkv, lam, subln_g.reshape(1, vd)]
    out_shape = [jax.ShapeDtypeStruct((M, H * vd), BF16)]
    out_specs = [pl.BlockSpec((t, vd), lambda b, h, i: (b * nq + i, h))]
    scaled, chunks = [], []
    for arr, lead, col_scale in cast_jobs:
        rows, cols = arr.shape[-2:]
        chunk = BF16_SUBLANES * pl.cdiv(rows, BF16_SUBLANES * n_steps)
        assert rows % chunk == 0, (rows, chunk)
        n_chunks = rows // chunk

        def chunk_of(b, h, i, n_chunks=n_chunks):
            return (((b * H + h) * nq + i) * n_chunks) // n_steps

        in_specs.append(pl.BlockSpec((None,) * len(lead) + (chunk, cols),
                                     lambda b, h, i, lead=lead, f=chunk_of: (*lead, f(b, h, i), 0)))
        args.append(arr)
        if col_scale is not None:
            in_specs.append(pl.BlockSpec((1, cols), lambda b, h, i: (0, 0)))
            args.append(col_scale.reshape(1, cols))
        scaled.append(col_scale is not None)
        chunks.append(n_chunks)
        out_shape.append(jax.ShapeDtypeStruct((rows, cols), BF16))
        out_specs.append(pl.BlockSpec((chunk, cols), lambda b, h, i, f=chunk_of: (f(b, h, i), 0)))
    outs = pl.pallas_call(
        functools.partial(_da_attn_kernel, t=t, hd=hd, n_heads=H, lambda_init=lambda_init,
                          cast_scaled=tuple(scaled), cast_chunks=tuple(chunks)),
        out_shape=out_shape,
        grid=(batch, H, nq),
        in_specs=in_specs,
        out_specs=out_specs,
        scratch_shapes=[pltpu.VMEM((nq, vd + BF16_SUBLANES, t), BF16),
                        pltpu.VMEM((2, t, t), F32),
                        pltpu.VMEM((2, t, t), F32), pltpu.VMEM((2, t, t), F32),
                        pltpu.VMEM((2, 1, t), F32),
                        pltpu.VMEM((2, vd + BF16_SUBLANES, t), F32)],
        compiler_params=_params("arbitrary", "arbitrary", "arbitrary"),
        name="diff_attention",
    )(*args)
    return outs[0], outs[1:]


def _ret_log_gamma(h):
    return math.log(1.0 - 2.0 ** (-5.0 - h))


def _retention_kernel(q_ref, k_ref, v_ref, g_ref, gn_ref, o_ref, state_ref, dmat_ref, qdec_ref,
                      kdec_ref, *, tb, dk, dv, n_heads):
    t = pl.program_id(1)

    @pl.when((pl.program_id(0) == 0) & (t == 0))
    def _():
        r = lax.broadcasted_iota(jnp.int32, (tb, tb), 0)
        c = lax.broadcasted_iota(jnp.int32, (tb, tb), 1)
        dist = jnp.abs(r - c).astype(F32)
        visible = (c // CHUNK) <= (r // CHUNK)
        pos = lax.broadcasted_iota(jnp.int32, (tb, dk), 0).astype(F32)
        for h in range(n_heads):
            lg = _ret_log_gamma(h)
            dmat_ref[h] = jnp.where(visible, jnp.exp(lg * dist), 0.0)
            qdec_ref[h] = jnp.exp(lg * pos)
            kdec_ref[h] = jnp.exp(lg * (tb - pos))

    @pl.when(t == 0)
    def _():
        state_ref[...] = jnp.zeros_like(state_ref)

    for h in range(n_heads):
        qh = q_ref[:, h * dk:(h + 1) * dk]
        kh = k_ref[:, h * dk:(h + 1) * dk]
        vh = v_ref[:, h * dv:(h + 1) * dv]
        inner = (_dot_nt(qh, kh) * dmat_ref[h]).astype(BF16)
        qd = (qh.astype(F32) * qdec_ref[h]).astype(BF16)
        kd = (kh.astype(F32) * kdec_ref[h]).astype(BF16)
        state = state_ref[h]
        o = _dot(inner, vh) + _dot(qd, state.astype(BF16))
        state_ref[h] = state * math.exp(_ret_log_gamma(h) * tb) + _dot_tn(kd, vh)
        mu = jnp.mean(o, axis=-1, keepdims=True)
        d = o - mu
        y = d * lax.rsqrt(jnp.mean(d * d, axis=-1, keepdims=True) + EPS) * gn_ref[...]
        gate = g_ref[:, h * dv:(h + 1) * dv].astype(F32)
        o_ref[:, h * dv:(h + 1) * dv] = (jax.nn.silu(gate) * y).astype(o_ref.dtype)


def _retention(proj, gn_g, *, batch, seq):
    M, W = proj.shape
    H = RET_HEADS
    dk = W // (6 * H)
    dv = 2 * dk
    tb = min(256, seq)
    nt = seq // tb
    row = lambda b, t: b * nt + t
    return pl.pallas_call(
        functools.partial(_retention_kernel, tb=tb, dk=dk, dv=dv, n_heads=H),
        out_shape=jax.ShapeDtypeStruct((M, H * dv), BF16),
        grid=(batch, nt),
        in_specs=[
            pl.BlockSpec((tb, H * dk), lambda b, t: (row(b, t), 0)),
            pl.BlockSpec((tb, H * dk), lambda b, t: (row(b, t), 1)),
            pl.BlockSpec((tb, H * dv), lambda b, t: (row(b, t), 1)),
            pl.BlockSpec((tb, H * dv), lambda b, t: (row(b, t), 2)),
            pl.BlockSpec((1, dv), lambda b, t: (0, 0)),
        ],
        out_specs=pl.BlockSpec((tb, H * dv), lambda b, t: (row(b, t), 0)),
        scratch_shapes=[pltpu.VMEM((H, dk, dv), F32), pltpu.VMEM((H, tb, tb), F32),
                        pltpu.VMEM((H, tb, dk), F32), pltpu.VMEM((H, tb, dk), F32)],
        compiler_params=_params("arbitrary", "arbitrary"),
        name="retention",
    )(proj, proj, proj, proj, gn_g.reshape(1, dv))


def kernel(x, c, ada_w, ada_b, norm_g, ffn_w_in, ffn_w_out, da_w_qkv, da_lambda, da_subln_g,
           da_w_o, ret_w_qkvg, ret_gn_g, ret_w_o, final_g):
    B, S, D = x.shape
    depth = ada_w.shape[0]
    mods = _ada_mods(c, ada_w, ada_b)

    da_hd = D // (2 * DA_HEADS)
    da_scale = jnp.concatenate([jnp.full((2 * DA_HEADS * da_hd,), da_hd ** -0.5 * LOG2E, F32),
                                jnp.ones((da_w_qkv.shape[2] - 2 * DA_HEADS * da_hd,), F32)])
    ret_dk = D // RET_HEADS
    ret_scale = jnp.concatenate([jnp.ones((RET_HEADS * ret_dk,), F32),
                                 jnp.full((RET_HEADS * ret_dk,), ret_dk ** -0.5, F32),
                                 jnp.ones((ret_w_qkvg.shape[2] - 2 * RET_HEADS * ret_dk,), F32)])
    wb = {
        ("ffn_in", 0, 0): ffn_w_in[0, 0].astype(BF16),
        ("ffn_out", 0, 0): ffn_w_out[0, 0].astype(BF16),
        ("da_qkv", 0): (da_w_qkv[0] * da_scale).astype(BF16),
    }
    jobs = [(("da_o", 0), da_w_o, (0,), None),
            (("ffn_in", 0, 1), ffn_w_in, (0, 1), None),
            (("ffn_out", 0, 1), ffn_w_out, (0, 1), None)]
    for i in range(1, depth):
        jobs += [(("ffn_in", i, k), ffn_w_in, (i, k), None) for k in range(2)]
        jobs += [(("ffn_out", i, k), ffn_w_out, (i, k), None) for k in range(2)]
        j = i // N_MIXERS
        if i % N_MIXERS == 0:
            jobs += [(("da_qkv", j), da_w_qkv, (j,), da_scale), (("da_o", j), da_w_o, (j,), None)]
        else:
            jobs += [(("ret", j), ret_w_qkvg, (j,), ret_scale), (("ret_o", j), ret_w_o, (j,), None)]

    xf = x.reshape(B * S, D)
    for i in range(depth):
        m = mods[i]
        xf = _ffn(xf, m, 0, norm_g[i, 0], wb["ffn_in", i, 0], wb["ffn_out", i, 0], seq=S)
        j = i // N_MIXERS
        if i % N_MIXERS == 0:
            lambda_init = 0.8 - 0.6 * math.exp(-0.3 * i)
            qkv = _norm_mod_matmul(xf, m, 3, norm_g[i, 1], wb["da_qkv", j], seq=S)
            heads, cast = _diff_attention(qkv, da_lambda[j], da_subln_g[j], batch=B, seq=S,
                                          lambda_init=lambda_init,
                                          cast_jobs=[job[1:] for job in jobs])
            wb.update({job[0]: w for job, w in zip(jobs, cast)})
            jobs = []
            xf = _matmul_residual(heads, wb["da_o", j], xf, m, 5, seq=S)
        else:
            proj = _norm_mod_matmul(xf, m, 3, norm_g[i, 1], wb["ret", j], seq=S)
            heads = _retention(proj, ret_gn_g[j], batch=B, seq=S)
            xf = _matmul_residual(heads, wb["ret_o", j], xf, m, 5, seq=S)
        xf = _ffn(xf, m, 6, norm_g[i, 2], wb["ffn_in", i, 1], wb["ffn_out", i, 1],
                  final_g=final_g if i == depth - 1 else None, seq=S)
    return xf.reshape(B, S, D)
```

```python
import functools
import math

import jax
import jax.numpy as jnp
from jax import lax
from jax.experimental import pallas as pl
from jax.experimental.pallas import tpu as pltpu

F32 = jnp.float32
BF16 = jnp.bfloat16

EPS = 1e-5
LOG2E = math.log2(math.e)
CHUNK = 64
N_MOD = 9
N_MIXERS = 2
DA_HEADS = 8
HEADS_PER_STEP = 2
RET_HEADS = 8

BF16_SUBLANES = 16
V7X_VMEM_BYTES = 64 * 1024 * 1024
VMEM_LIMIT_BYTES = V7X_VMEM_BYTES - 8 * 1024 * 1024


def _params(*semantics):
    return pltpu.CompilerParams(dimension_semantics=semantics, vmem_limit_bytes=VMEM_LIMIT_BYTES)


def _dot(a, b):
    return jnp.dot(a, b, preferred_element_type=F32)


def _dot_nt(a, b):
    return lax.dot_general(a, b, (((1,), (1,)), ((), ())), preferred_element_type=F32)


def _dot_tn(a, b):
    return lax.dot_general(a, b, (((0,), (0,)), ((), ())), preferred_element_type=F32)


def _rms(x, g):
    return x * lax.rsqrt(jnp.mean(x * x, axis=-1, keepdims=True) + EPS) * g


NORM_ROWS = 32
FIRST_STEP_PARTS = 4


def _norm_mod_into(h_ref, x_ref, g_ref, shift_ref, scale_ref, rows=None, unrolled=False):
    gs = g_ref[...] * (1.0 + scale_ref[...])
    shift = shift_ref[...]
    start, stop = rows if rows is not None else (0, x_ref.shape[0])

    def chunk(r0):
        x = x_ref[pl.ds(r0, NORM_ROWS), :]
        rinv = lax.rsqrt(jnp.mean(x * x, axis=-1, keepdims=True) + EPS)
        h_ref[pl.ds(r0, NORM_ROWS), :] = (x * rinv * gs + shift).astype(BF16)

    if unrolled:
        for r0 in range(start, stop, NORM_ROWS):
            chunk(r0)
    else:
        def body(i, carry):
            chunk(pl.multiple_of(start + i * NORM_ROWS, NORM_ROWS))
            return carry

        lax.fori_loop(0, (stop - start) // NORM_ROWS, body, 0, unroll=4)


def _ada_kernel(c_ref, w_ref, b_ref, o_ref):
    cs = jax.nn.silu(c_ref[...]).astype(BF16)
    o_ref[...] = _dot(cs, w_ref[...].astype(BF16)) + b_ref[...]


def _ada_mods(c, ada_w, ada_b):
    B, D = c.shape
    L, _, W = ada_w.shape
    rows = 8
    tn = min(2048, W)
    c_pad = jnp.zeros((rows, D), F32).at[:B].set(c)
    out = pl.pallas_call(
        _ada_kernel,
        out_shape=jax.ShapeDtypeStruct((L, rows, W), F32),
        grid=(L, W // tn),
        in_specs=[
            pl.BlockSpec((rows, D), lambda l, n: (0, 0)),
            pl.BlockSpec((None, D, tn), lambda l, n: (l, 0, n)),
            pl.BlockSpec((None, 1, tn), lambda l, n: (l, 0, n)),
        ],
        out_specs=pl.BlockSpec((None, rows, tn), lambda l, n: (l, 0, n)),
        compiler_params=_params("parallel", "parallel"),
        name="ada_mods",
    )(c_pad, ada_w, ada_b.reshape(L, 1, W))
    return out[:, :B].reshape(L, B, N_MOD, D).transpose(0, 2, 1, 3)[:, :, :, None, :]


def _ffn_kernel(x_ref, sh_ref, sc_ref, gt_ref, ng_ref, wg_ref, wu_ref, wo_ref, *rest, final, nf):
    if final:
        fg_ref, o_ref, h_ref = rest
    else:
        o_ref, h_ref = rest
    j = pl.program_id(1)
    assert nf >= 2, "the first and last hidden-dimension steps are distinct code paths"

    def contribution(rows=slice(None)):
        h = h_ref[rows, :]
        a = _dot(h, wg_ref[...])
        b = _dot(h, wu_ref[...])
        act = (jax.nn.silu(a) * b).astype(BF16)
        return _dot(act, wo_ref[...])

    @pl.when(j == 0)
    def _():
        part = x_ref.shape[0] // FIRST_STEP_PARTS
        _norm_mod_into(h_ref, x_ref, ng_ref, sh_ref, sc_ref, rows=(0, part))
        for q in range(FIRST_STEP_PARTS):
            if q + 1 < FIRST_STEP_PARTS:
                _norm_mod_into(h_ref, x_ref, ng_ref, sh_ref, sc_ref,
                               rows=((q + 1) * part, (q + 2) * part), unrolled=True)
            o_ref[q * part:(q + 1) * part, :] = contribution(slice(q * part, (q + 1) * part))

    @pl.when((j > 0) & (j < nf - 1))
    def _():
        o_ref[...] += contribution()

    @pl.when(j == nf - 1)
    def _():
        o_ref[...] = x_ref[...] + (0.5 * gt_ref[...]) * (o_ref[...] + contribution())
        if final:
            for r0 in range(0, x_ref.shape[0], NORM_ROWS):
                o_ref[r0:r0 + NORM_ROWS, :] = _rms(o_ref[r0:r0 + NORM_ROWS, :], fg_ref[...])


def _ffn(x, mods, k0, norm_g, w_in, w_out, final_g=None, *, seq):
    M, D = x.shape
    F = w_out.shape[0]
    tm = min(1024, seq)
    tf = min(512, F)
    nf = F // tf
    bpb = seq // tm
    vec = lambda k: pl.BlockSpec((None, None, 1, D), lambda i, j: (k, i // bpb, 0, 0))
    in_specs = [
        pl.BlockSpec((tm, D), lambda i, j: (i, 0)),
        vec(k0), vec(k0 + 1), vec(k0 + 2),
        pl.BlockSpec((1, D), lambda i, j: (0, 0)),
        pl.BlockSpec((D, tf), lambda i, j: (0, j)),
        pl.BlockSpec((D, tf), lambda i, j: (0, nf + j)),
        pl.BlockSpec((tf, D), lambda i, j: (j, 0)),
    ]
    args = [x, mods, mods, mods, norm_g.reshape(1, D), w_in, w_in, w_out]
    if final_g is not None:
        in_specs.append(pl.BlockSpec((1, D), lambda i, j: (0, 0)))
        args.append(final_g.reshape(1, D))
    return pl.pallas_call(
        functools.partial(_ffn_kernel, final=final_g is not None, nf=nf),
        out_shape=jax.ShapeDtypeStruct((M, D), F32),
        grid=(M // tm, nf),
        in_specs=in_specs,
        out_specs=pl.BlockSpec((tm, D), lambda i, j: (i, 0)),
        scratch_shapes=[pltpu.VMEM((tm, D), BF16)],
        compiler_params=_params("parallel", "arbitrary"),
        name="ffn",
    )(*args)


def _nm_matmul_kernel(x_ref, sh_ref, sc_ref, ng_ref, w_ref, o_ref, h_ref):
    n = pl.program_id(1)

    @pl.when(n == 0)
    def _():
        part = x_ref.shape[0] // FIRST_STEP_PARTS
        _norm_mod_into(h_ref, x_ref, ng_ref, sh_ref, sc_ref, rows=(0, part))
        for q in range(FIRST_STEP_PARTS):
            if q + 1 < FIRST_STEP_PARTS:
                _norm_mod_into(h_ref, x_ref, ng_ref, sh_ref, sc_ref,
                               rows=((q + 1) * part, (q + 2) * part), unrolled=True)
            rows = slice(q * part, (q + 1) * part)
            o_ref[rows, :] = _dot(h_ref[rows, :], w_ref[...]).astype(o_ref.dtype)

    @pl.when(n > 0)
    def _():
        o_ref[...] = _dot(h_ref[...], w_ref[...]).astype(o_ref.dtype)


def _norm_mod_matmul(x, mods, k0, norm_g, w, *, seq):
    M, D = x.shape
    N = w.shape[1]
    tm = min(1024, seq)
    tn = min(2048, N)
    bpb = seq // tm
    vec = lambda k: pl.BlockSpec((None, None, 1, D), lambda i, n: (k, i // bpb, 0, 0))
    return pl.pallas_call(
        _nm_matmul_kernel,
        out_shape=jax.ShapeDtypeStruct((M, N), BF16),
        grid=(M // tm, N // tn),
        in_specs=[
            pl.BlockSpec((tm, D), lambda i, n: (i, 0)),
            vec(k0), vec(k0 + 1),
            pl.BlockSpec((1, D), lambda i, n: (0, 0)),
            pl.BlockSpec((D, tn), lambda i, n: (0, n)),
        ],
        out_specs=pl.BlockSpec((tm, tn), lambda i, n: (i, n)),
        scratch_shapes=[pltpu.VMEM((tm, D), BF16)],
        compiler_params=_params("parallel", "arbitrary"),
        name="norm_mod_matmul",
    )(x, mods, mods, norm_g.reshape(1, D), w)


def _mm_res_kernel(a_ref, w_ref, x_ref, gt_ref, o_ref):
    o_ref[...] = x_ref[...] + gt_ref[...] * _dot(a_ref[...], w_ref[...])


def _matmul_residual(a, w, x, mods, k, *, seq):
    M, K = a.shape
    N = w.shape[1]
    if 2 * K * N * w.dtype.itemsize <= V7X_VMEM_BYTES // 4:
        tm, tn = min(512, seq), N
    else:
        tm, tn = min(1024, seq), min(1024, N)
    bpb = seq // tm
    return pl.pallas_call(
        _mm_res_kernel,
        out_shape=jax.ShapeDtypeStruct((M, N), F32),
        grid=(M // tm, N // tn),
        in_specs=[
            pl.BlockSpec((tm, K), lambda i, n: (i, 0)),
            pl.BlockSpec((K, tn), lambda i, n: (0, n)),
            pl.BlockSpec((tm, tn), lambda i, n: (i, n)),
            pl.BlockSpec((None, None, 1, tn), lambda i, n: (k, i // bpb, 0, n)),
        ],
        out_specs=pl.BlockSpec((tm, tn), lambda i, n: (i, n)),
        compiler_params=_params("parallel", "parallel"),
        name="matmul_residual",
    )(a, w, x, mods)


def _da_attn_kernel(q_ref, k_ref, v_ref, lam_ref, g_ref, *rest, t, hd, n_heads, lambda_init,
                    cast_scaled, cast_chunks):
    n_cast_in = len(cast_scaled) + sum(cast_scaled)
    cast_in = rest[:n_cast_in]
    o_ref = rest[n_cast_in]
    cast_out = rest[n_cast_in + 1:n_cast_in + 1 + len(cast_scaled)]
    vt_ref, bias_ref, sa_ref, sb_ref, m_ref, acc_ref = rest[n_cast_in + 1 + len(cast_scaled):]

    qi = pl.program_id(2)
    nblk = vt_ref.shape[1]
    vd = 2 * hd
    hps = vt_ref.shape[0]

    step = (pl.program_id(0) * pl.num_programs(1) + pl.program_id(1)) * pl.num_programs(2) + qi
    n_steps = pl.num_programs(0) * pl.num_programs(1) * pl.num_programs(2)
    pos = 0
    for dst_ref, has_scale, n_chunks in zip(cast_out, cast_scaled, cast_chunks):
        src_ref = cast_in[pos]
        scale_ref = cast_in[pos + 1] if has_scale else None
        pos += 2 if has_scale else 1
        moved = (step == 0) | ((step * n_chunks) // n_steps != ((step - 1) * n_chunks) // n_steps)

        @pl.when(moved)
        def _(dst_ref=dst_ref, src_ref=src_ref, scale_ref=scale_ref):
            w = src_ref[...]
            if scale_ref is not None:
                w = w * scale_ref[...]
            dst_ref[...] = w.astype(BF16)

    lam = lam_ref[...]
    lam_val = (jnp.exp(jnp.sum(lam[0:1] * lam[1:2], axis=-1, keepdims=True))
               - jnp.exp(jnp.sum(lam[2:3] * lam[3:4], axis=-1, keepdims=True)) + lambda_init)

    for hh in range(hps):
        c0 = hh * vd
        head = pl.program_id(1) * hps + hh
        hv = jnp.full((1, 1), head + 1, jnp.int32).astype(F32)
        slope = jnp.exp2(hv * (-8.0 / n_heads)) * LOG2E

        @pl.when(qi == 0)
        def _(hh=hh, c0=c0, slope=slope):
            for b in range(nblk):
                vt_ref[hh, b, 0:vd, :] = v_ref[b * t:(b + 1) * t, c0:c0 + vd].astype(F32).T.astype(BF16)
                vt_ref[hh, b, vd:, :] = jnp.ones((vt_ref.shape[2] - vd, t), BF16)
            r = lax.broadcasted_iota(jnp.int32, (t, t), 0)
            c = lax.broadcasted_iota(jnp.int32, (t, t), 1)
            bias_ref[hh, 0] = slope * r.astype(F32)
            bias_ref[hh, 1] = jnp.where((r // CHUNK) <= (c // CHUNK),
                                        slope * (c - jnp.abs(c - r)).astype(F32), -jnp.inf)

        qts = [q_ref[:, c0 + n * hd:c0 + (n + 1) * hd].astype(F32).T.astype(BF16) for n in range(2)]
        for n in range(2):
            m_ref[n] = jnp.full((1, t), -jnp.inf, F32)
            acc_ref[n] = jnp.zeros(acc_ref.shape[1:], F32)

        def qk(kb, s_ref, hh=hh, c0=c0, qts=qts):
            kk = pl.multiple_of(kb * t, t)
            bias = bias_ref[hh, (kb == qi).astype(jnp.int32)]
            for n in range(2):
                s_ref[n] = _dot(k_ref[pl.ds(kk, t), c0 + n * hd:c0 + (n + 1) * hd], qts[n]) + bias

        def softmax_pv(kb, s_ref, hh=hh, slope=slope):
            off = slope * ((kb - qi) * t).astype(F32)
            vt_b = vt_ref[hh, kb]
            for n in range(2):
                s = s_ref[n]
                m_old = m_ref[n]
                m_new = jnp.maximum(m_old, jnp.max(s, axis=0, keepdims=True) + off)
                alpha = jnp.exp2(m_old - m_new)
                p = jnp.exp2(s - (m_new - off))
                m_ref[n] = m_new
                acc_ref[n] = alpha * acc_ref[n] + _dot(vt_b, p.astype(BF16))

        qk(0, sa_ref)

        def body(pair, carry, qk=qk, softmax_pv=softmax_pv):
            kb = 2 * pair
            qk(kb + 1, sb_ref)
            softmax_pv(kb, sa_ref)
            qk(jnp.minimum(kb + 2, qi), sa_ref)
            softmax_pv(kb + 1, sb_ref)
            return carry

        lax.fori_loop(0, (qi + 1) // 2, body, 0)

        @pl.when(qi % 2 == 0)
        def _(softmax_pv=softmax_pv):
            softmax_pv(qi, sa_ref)

        heads_t = [acc_ref[n, 0:vd, :] / acc_ref[n, vd:vd + 1, :] for n in range(2)]
        o_t = heads_t[0] - lam_val * heads_t[1]
        o_t = o_t * lax.rsqrt(jnp.mean(o_t * o_t, axis=0, keepdims=True) + EPS)
        o_ref[:, c0:c0 + vd] = (o_t.T * (g_ref[...] * (1.0 - lambda_init))).astype(o_ref.dtype)


def _diff_attention(qkv, lam, subln_g, *, batch, seq, lambda_init, cast_jobs=()):
    M = qkv.shape[0]
    H = DA_HEADS
    hd = qkv.shape[1] // (6 * H)
    vd = 2 * hd
    t = min(512, seq)
    nq = seq // t
    hps = HEADS_PER_STEP
    hg = H // hps
    n_steps = batch * hg * nq
    wd = hps * vd
    in_specs = [
        pl.BlockSpec((t, wd), lambda b, h, i: (b * nq + i, h)),
        pl.BlockSpec((seq, wd), lambda b, h, i: (b, hg + h)),
        pl.BlockSpec((seq, wd), lambda b, h, i: (b, 2 * hg + h)),
        pl.BlockSpec((4, hd), lambda b, h, i: (0, 0)),
        pl.BlockSpec((1, vd), lambda b, h, i: (0, 0)),
    ]
    args = [qkv, qkv, qkv, lam, subln_g.reshape(1, vd)]
    out_shape = [jax.ShapeDtypeStruct((M, H * vd), BF16)]
    out_specs = [pl.BlockSpec((t, wd), lambda b, h, i: (b * nq + i, h))]
    scaled, chunks = [], []
    for arr, lead, col_scale in cast_jobs:
        rows, cols = arr.shape[-2:]
        chunk = next(c for c in range(BF16_SUBLANES, rows + 1, BF16_SUBLANES)
                     if rows % c == 0 and rows // c <= n_steps)
        n_chunks = rows // chunk

        def chunk_of(b, h, i, n_chunks=n_chunks):
            return (((b * hg + h) * nq + i) * n_chunks) // n_steps

        in_specs.append(pl.BlockSpec((None,) * len(lead) + (chunk, cols),
                                     lambda b, h, i, lead=lead, f=chunk_of: (*lead, f(b, h, i), 0)))
        args.append(arr)
        if col_scale is not None:
            in_specs.append(pl.BlockSpec((1, cols), lambda b, h, i: (0, 0)))
            args.append(col_scale.reshape(1, cols))
        scaled.append(col_scale is not None)
        chunks.append(n_chunks)
        out_shape.append(jax.ShapeDtypeStruct((rows, cols), BF16))
        out_specs.append(pl.BlockSpec((chunk, cols), lambda b, h, i, f=chunk_of: (f(b, h, i), 0)))
    outs = pl.pallas_call(
        functools.partial(_da_attn_kernel, t=t, hd=hd, n_heads=H, lambda_init=lambda_init,
                          cast_scaled=tuple(scaled), cast_chunks=tuple(chunks)),
        out_shape=out_shape,
        grid=(batch, hg, nq),
        in_specs=in_specs,
        out_specs=out_specs,
        scratch_shapes=[pltpu.VMEM((hps, nq, vd + BF16_SUBLANES, t), BF16),
                        pltpu.VMEM((hps, 2, t, t), F32),
                        pltpu.VMEM((2, t, t), F32), pltpu.VMEM((2, t, t), F32),
                        pltpu.VMEM((2, 1, t), F32),
                        pltpu.VMEM((2, vd + BF16_SUBLANES, t), F32)],
        compiler_params=_params("arbitrary", "arbitrary", "arbitrary"),
        name="diff_attention",
    )(*args)
    return outs[0], outs[1:]


def _ret_log_gamma(h):
    return math.log(1.0 - 2.0 ** (-5.0 - h))


def _retention_kernel(q_ref, k_ref, v_ref, g_ref, gn_ref, o_ref, state_ref, dmat_ref, qdec_ref,
                      kdec_ref, *, tb, dk, dv, n_heads):
    t = pl.program_id(1)

    @pl.when((pl.program_id(0) == 0) & (t == 0))
    def _():
        r = lax.broadcasted_iota(jnp.int32, (tb, tb), 0)
        c = lax.broadcasted_iota(jnp.int32, (tb, tb), 1)
        dist = jnp.abs(r - c).astype(F32)
        visible = (c // CHUNK) <= (r // CHUNK)
        pos = lax.broadcasted_iota(jnp.int32, (tb, dk), 0).astype(F32)
        for h in range(n_heads):
            lg = _ret_log_gamma(h)
            dmat_ref[h] = jnp.where(visible, jnp.exp(lg * dist), 0.0)
            qdec_ref[h] = jnp.exp(lg * pos)
            kdec_ref[h] = jnp.exp(lg * (tb - pos))

    @pl.when(t == 0)
    def _():
        state_ref[...] = jnp.zeros_like(state_ref)

    for h in range(n_heads):
        qh = q_ref[:, h * dk:(h + 1) * dk]
        kh = k_ref[:, h * dk:(h + 1) * dk]
        vh = v_ref[:, h * dv:(h + 1) * dv]
        inner = (_dot_nt(qh, kh) * dmat_ref[h]).astype(BF16)
        qd = (qh.astype(F32) * qdec_ref[h]).astype(BF16)
        kd = (kh.astype(F32) * kdec_ref[h]).astype(BF16)
        state = state_ref[h]
        o = _dot(inner, vh) + _dot(qd, state.astype(BF16))
        state_ref[h] = state * math.exp(_ret_log_gamma(h) * tb) + _dot_tn(kd, vh)
        mu = jnp.mean(o, axis=-1, keepdims=True)
        d = o - mu
        y = d * lax.rsqrt(jnp.mean(d * d, axis=-1, keepdims=True) + EPS) * gn_ref[...]
        gate = g_ref[:, h * dv:(h + 1) * dv].astype(F32)
        o_ref[:, h * dv:(h + 1) * dv] = (jax.nn.silu(gate) * y).astype(o_ref.dtype)


def _retention(proj, gn_g, *, batch, seq):
    M, W = proj.shape
    H = RET_HEADS
    dk = W // (6 * H)
    dv = 2 * dk
    tb = min(256, seq)
    nt = seq // tb
    row = lambda b, t: b * nt + t
    return pl.pallas_call(
        functools.partial(_retention_kernel, tb=tb, dk=dk, dv=dv, n_heads=H),
        out_shape=jax.ShapeDtypeStruct((M, H * dv), BF16),
        grid=(batch, nt),
        in_specs=[
            pl.BlockSpec((tb, H * dk), lambda b, t: (row(b, t), 0)),
            pl.BlockSpec((tb, H * dk), lambda b, t: (row(b, t), 1)),
            pl.BlockSpec((tb, H * dv), lambda b, t: (row(b, t), 1)),
            pl.BlockSpec((tb, H * dv), lambda b, t: (row(b, t), 2)),
            pl.BlockSpec((1, dv), lambda b, t: (0, 0)),
        ],
        out_specs=pl.BlockSpec((tb, H * dv), lambda b, t: (row(b, t), 0)),
        scratch_shapes=[pltpu.VMEM((H, dk, dv), F32), pltpu.VMEM((H, tb, tb), F32),
                        pltpu.VMEM((H, tb, dk), F32), pltpu.VMEM((H, tb, dk), F32)],
        compiler_params=_params("arbitrary", "arbitrary"),
        name="retention",
    )(proj, proj, proj, proj, gn_g.reshape(1, dv))


def kernel(x, c, ada_w, ada_b, norm_g, ffn_w_in, ffn_w_out, da_w_qkv, da_lambda, da_subln_g,
           da_w_o, ret_w_qkvg, ret_gn_g, ret_w_o, final_g):
    B, S, D = x.shape
    depth = ada_w.shape[0]
    mods = _ada_mods(c, ada_w, ada_b)

    da_hd = D // (2 * DA_HEADS)
    da_scale = jnp.concatenate([jnp.full((2 * DA_HEADS * da_hd,), da_hd ** -0.5 * LOG2E, F32),
                                jnp.ones((da_w_qkv.shape[2] - 2 * DA_HEADS * da_hd,), F32)])
    ret_dk = D // RET_HEADS
    ret_scale = jnp.concatenate([jnp.ones((RET_HEADS * ret_dk,), F32),
                                 jnp.full((RET_HEADS * ret_dk,), ret_dk ** -0.5, F32),
                                 jnp.ones((ret_w_qkvg.shape[2] - 2 * RET_HEADS * ret_dk,), F32)])
    wb = {
        ("ffn_in", 0, 0): ffn_w_in[0, 0].astype(BF16),
        ("ffn_out", 0, 0): ffn_w_out[0, 0].astype(BF16),
        ("da_qkv", 0): (da_w_qkv[0] * da_scale).astype(BF16),
    }
    jobs = [(("da_o", 0), da_w_o, (0,), None),
            (("ffn_in", 0, 1), ffn_w_in, (0, 1), None),
            (("ffn_out", 0, 1), ffn_w_out, (0, 1), None)]
    for i in range(1, depth):
        jobs += [(("ffn_in", i, k), ffn_w_in, (i, k), None) for k in range(2)]
        jobs += [(("ffn_out", i, k), ffn_w_out, (i, k), None) for k in range(2)]
        j = i // N_MIXERS
        if i % N_MIXERS == 0:
            jobs += [(("da_qkv", j), da_w_qkv, (j,), da_scale), (("da_o", j), da_w_o, (j,), None)]
        else:
            jobs += [(("ret", j), ret_w_qkvg, (j,), ret_scale), (("ret_o", j), ret_w_o, (j,), None)]

    xf = x.reshape(B * S, D)
    for i in range(depth):
        m = mods[i]
        xf = _ffn(xf, m, 0, norm_g[i, 0], wb["ffn_in", i, 0], wb["ffn_out", i, 0], seq=S)
        j = i // N_MIXERS
        if i % N_MIXERS == 0:
            lambda_init = 0.8 - 0.6 * math.exp(-0.3 * i)
            qkv = _norm_mod_matmul(xf, m, 3, norm_g[i, 1], wb["da_qkv", j], seq=S)
            heads, cast = _diff_attention(qkv, da_lambda[j], da_subln_g[j], batch=B, seq=S,
                                          lambda_init=lambda_init,
                                          cast_jobs=[job[1:] for job in jobs])
            wb.update({job[0]: w for job, w in zip(jobs, cast)})
            jobs = []
            xf = _matmul_residual(heads, wb["da_o", j], xf, m, 5, seq=S)
        else:
            proj = _norm_mod_matmul(xf, m, 3, norm_g[i, 1], wb["ret", j], seq=S)
            heads = _retention(proj, ret_gn_g[j], batch=B, seq=S)
            xf = _matmul_residual(heads, wb["ret_o", j], xf, m, 5, seq=S)
        xf = _ffn(xf, m, 6, norm_g[i, 2], wb["ffn_in", i, 1], wb["ffn_out", i, 1],
                  final_g=final_g if i == depth - 1 else None, seq=S)
    return xf.reshape(B, S, D)
```

```python
import functools
import math

import jax
import jax.numpy as jnp
from jax import lax
from jax.experimental import pallas as pl
from jax.experimental.pallas import tpu as pltpu

F32 = jnp.float32
BF16 = jnp.bfloat16

EPS = 1e-5
LOG2E = math.log2(math.e)
CHUNK = 64
N_MOD = 9
N_MIXERS = 2
DA_HEADS = 8
HEADS_PER_STEP = 2
RET_HEADS = 8

BF16_SUBLANES = 16
V7X_VMEM_BYTES = 64 * 1024 * 1024
VMEM_LIMIT_BYTES = V7X_VMEM_BYTES - 4 * 1024 * 1024


def _params(*semantics):
    return pltpu.CompilerParams(dimension_semantics=semantics, vmem_limit_bytes=VMEM_LIMIT_BYTES)


def _dot(a, b):
    return jnp.dot(a, b, preferred_element_type=F32)


def _dot_nt(a, b):
    return lax.dot_general(a, b, (((1,), (1,)), ((), ())), preferred_element_type=F32)


def _dot_tn(a, b):
    return lax.dot_general(a, b, (((0,), (0,)), ((), ())), preferred_element_type=F32)


def _rms(x, g):
    return x * lax.rsqrt(jnp.mean(x * x, axis=-1, keepdims=True) + EPS) * g


NORM_ROWS = 32
FIRST_STEP_PARTS = 4


def _norm_mod_into(h_ref, x_ref, g_ref, shift_ref, scale_ref, rows=None, unrolled=False):
    gs = g_ref[...] * (1.0 + scale_ref[...])
    shift = shift_ref[...]
    start, stop = rows if rows is not None else (0, x_ref.shape[0])

    def chunk(r0):
        x = x_ref[pl.ds(r0, NORM_ROWS), :]
        rinv = lax.rsqrt(jnp.mean(x * x, axis=-1, keepdims=True) + EPS)
        h_ref[pl.ds(r0, NORM_ROWS), :] = (x * rinv * gs + shift).astype(BF16)

    if unrolled:
        for r0 in range(start, stop, NORM_ROWS):
            chunk(r0)
    else:
        def body(i, carry):
            chunk(pl.multiple_of(start + i * NORM_ROWS, NORM_ROWS))
            return carry

        lax.fori_loop(0, (stop - start) // NORM_ROWS, body, 0, unroll=4)


def _ada_kernel(c_ref, w_ref, b_ref, o_ref):
    cs = jax.nn.silu(c_ref[...]).astype(BF16)
    o_ref[...] = _dot(cs, w_ref[...].astype(BF16)) + b_ref[...]


def _ada_mods(c, ada_w, ada_b):
    B, D = c.shape
    L, _, W = ada_w.shape
    rows = 8
    tn = min(2048, W)
    c_pad = jnp.zeros((rows, D), F32).at[:B].set(c)
    out = pl.pallas_call(
        _ada_kernel,
        out_shape=jax.ShapeDtypeStruct((L, rows, W), F32),
        grid=(L, W // tn),
        in_specs=[
            pl.BlockSpec((rows, D), lambda l, n: (0, 0)),
            pl.BlockSpec((None, D, tn), lambda l, n: (l, 0, n)),
            pl.BlockSpec((None, 1, tn), lambda l, n: (l, 0, n)),
        ],
        out_specs=pl.BlockSpec((None, rows, tn), lambda l, n: (l, 0, n)),
        compiler_params=_params("parallel", "parallel"),
        name="ada_mods",
    )(c_pad, ada_w, ada_b.reshape(L, 1, W))
    return out[:, :B].reshape(L, B, N_MOD, D).transpose(0, 2, 1, 3)[:, :, :, None, :]


def _ffn_kernel(x_ref, sh_ref, sc_ref, gt_ref, ng_ref, wg_ref, wu_ref, wo_ref, *rest, final, nf):
    if final:
        fg_ref, o_ref, h_ref = rest
    else:
        o_ref, h_ref = rest
    j = pl.program_id(1)
    assert nf >= 2, "the first and last hidden-dimension steps are distinct code paths"

    def contribution(rows=slice(None)):
        h = h_ref[rows, :]
        a = _dot(h, wg_ref[...])
        b = _dot(h, wu_ref[...])
        act = (jax.nn.silu(a) * b).astype(BF16)
        return _dot(act, wo_ref[...])

    @pl.when(j == 0)
    def _():
        part = x_ref.shape[0] // FIRST_STEP_PARTS
        _norm_mod_into(h_ref, x_ref, ng_ref, sh_ref, sc_ref, rows=(0, part))
        for q in range(FIRST_STEP_PARTS):
            if q + 1 < FIRST_STEP_PARTS:
                _norm_mod_into(h_ref, x_ref, ng_ref, sh_ref, sc_ref,
                               rows=((q + 1) * part, (q + 2) * part), unrolled=True)
            o_ref[q * part:(q + 1) * part, :] = contribution(slice(q * part, (q + 1) * part))

    @pl.when((j > 0) & (j < nf - 1))
    def _():
        o_ref[...] += contribution()

    @pl.when(j == nf - 1)
    def _():
        o_ref[...] = x_ref[...] + (0.5 * gt_ref[...]) * (o_ref[...] + contribution())
        if final:
            for r0 in range(0, x_ref.shape[0], NORM_ROWS):
                o_ref[r0:r0 + NORM_ROWS, :] = _rms(o_ref[r0:r0 + NORM_ROWS, :], fg_ref[...])


def _ffn(x, mods, k0, norm_g, w_in, w_out, final_g=None, *, seq):
    M, D = x.shape
    F = w_out.shape[0]
    tm = min(1024, seq)
    tf = min(512, F)
    nf = F // tf
    bpb = seq // tm
    vec = lambda k: pl.BlockSpec((None, None, 1, D), lambda i, j: (k, i // bpb, 0, 0))
    in_specs = [
        pl.BlockSpec((tm, D), lambda i, j: (i, 0)),
        vec(k0), vec(k0 + 1), vec(k0 + 2),
        pl.BlockSpec((1, D), lambda i, j: (0, 0)),
        pl.BlockSpec((D, tf), lambda i, j: (0, j)),
        pl.BlockSpec((D, tf), lambda i, j: (0, nf + j)),
        pl.BlockSpec((tf, D), lambda i, j: (j, 0)),
    ]
    args = [x, mods, mods, mods, norm_g.reshape(1, D), w_in, w_in, w_out]
    if final_g is not None:
        in_specs.append(pl.BlockSpec((1, D), lambda i, j: (0, 0)))
        args.append(final_g.reshape(1, D))
    return pl.pallas_call(
        functools.partial(_ffn_kernel, final=final_g is not None, nf=nf),
        out_shape=jax.ShapeDtypeStruct((M, D), F32),
        grid=(M // tm, nf),
        in_specs=in_specs,
        out_specs=pl.BlockSpec((tm, D), lambda i, j: (i, 0)),
        scratch_shapes=[pltpu.VMEM((tm, D), BF16)],
        compiler_params=_params("parallel", "arbitrary"),
        name="ffn",
    )(*args)


def _nm_matmul_kernel(x_ref, sh_ref, sc_ref, ng_ref, w_ref, o_ref, h_ref):
    n = pl.program_id(1)

    @pl.when(n == 0)
    def _():
        part = x_ref.shape[0] // FIRST_STEP_PARTS
        _norm_mod_into(h_ref, x_ref, ng_ref, sh_ref, sc_ref, rows=(0, part))
        for q in range(FIRST_STEP_PARTS):
            if q + 1 < FIRST_STEP_PARTS:
                _norm_mod_into(h_ref, x_ref, ng_ref, sh_ref, sc_ref,
                               rows=((q + 1) * part, (q + 2) * part), unrolled=True)
            rows = slice(q * part, (q + 1) * part)
            o_ref[rows, :] = _dot(h_ref[rows, :], w_ref[...]).astype(o_ref.dtype)

    @pl.when(n > 0)
    def _():
        o_ref[...] = _dot(h_ref[...], w_ref[...]).astype(o_ref.dtype)


def _norm_mod_matmul(x, mods, k0, norm_g, w, *, seq):
    M, D = x.shape
    N = w.shape[1]
    tm = min(1024, seq)
    tn = 3072 if N % 3072 == 0 else min(2048, N)
    bpb = seq // tm
    vec = lambda k: pl.BlockSpec((None, None, 1, D), lambda i, n: (k, i // bpb, 0, 0))
    return pl.pallas_call(
        _nm_matmul_kernel,
        out_shape=jax.ShapeDtypeStruct((M, N), BF16),
        grid=(M // tm, N // tn),
        in_specs=[
            pl.BlockSpec((tm, D), lambda i, n: (i, 0)),
            vec(k0), vec(k0 + 1),
            pl.BlockSpec((1, D), lambda i, n: (0, 0)),
            pl.BlockSpec((D, tn), lambda i, n: (0, n)),
        ],
        out_specs=pl.BlockSpec((tm, tn), lambda i, n: (i, n)),
        scratch_shapes=[pltpu.VMEM((tm, D), BF16)],
        compiler_params=_params("parallel", "arbitrary"),
        name="norm_mod_matmul",
    )(x, mods, mods, norm_g.reshape(1, D), w)


def _mm_res_kernel(a_ref, w_ref, x_ref, gt_ref, o_ref):
    o_ref[...] = x_ref[...] + gt_ref[...] * _dot(a_ref[...], w_ref[...])


def _matmul_residual(a, w, x, mods, k, *, seq):
    M, K = a.shape
    N = w.shape[1]
    if 2 * K * N * w.dtype.itemsize <= V7X_VMEM_BYTES // 4:
        tm, tn = min(512, seq), N
    else:
        tm, tn = min(1024, seq), min(1024, N)
    bpb = seq // tm
    return pl.pallas_call(
        _mm_res_kernel,
        out_shape=jax.ShapeDtypeStruct((M, N), F32),
        grid=(M // tm, N // tn),
        in_specs=[
            pl.BlockSpec((tm, K), lambda i, n: (i, 0)),
            pl.BlockSpec((K, tn), lambda i, n: (0, n)),
            pl.BlockSpec((tm, tn), lambda i, n: (i, n)),
            pl.BlockSpec((None, None, 1, tn), lambda i, n: (k, i // bpb, 0, n)),
        ],
        out_specs=pl.BlockSpec((tm, tn), lambda i, n: (i, n)),
        compiler_params=_params("parallel", "parallel"),
        name="matmul_residual",
    )(a, w, x, mods)


def _da_attn_kernel(q_ref, k_ref, v_ref, lam_ref, g_ref, *rest, t, hd, n_heads, lambda_init,
                    cast_scaled, cast_chunks):
    n_cast_in = len(cast_scaled) + sum(cast_scaled)
    cast_in = rest[:n_cast_in]
    o_ref = rest[n_cast_in]
    cast_out = rest[n_cast_in + 1:n_cast_in + 1 + len(cast_scaled)]
    vt_ref, bias_ref, sa_ref, sb_ref, m_ref, acc_ref = rest[n_cast_in + 1 + len(cast_scaled):]

    qi = pl.program_id(2)
    nblk = vt_ref.shape[1]
    vd = 2 * hd
    hps = vt_ref.shape[0]

    step = (pl.program_id(0) * pl.num_programs(1) + pl.program_id(1)) * pl.num_programs(2) + qi
    n_steps = pl.num_programs(0) * pl.num_programs(1) * pl.num_programs(2)
    pos = 0
    for dst_ref, has_scale, n_chunks in zip(cast_out, cast_scaled, cast_chunks):
        src_ref = cast_in[pos]
        scale_ref = cast_in[pos + 1] if has_scale else None
        pos += 2 if has_scale else 1
        moved = (step == 0) | ((step * n_chunks) // n_steps != ((step - 1) * n_chunks) // n_steps)

        @pl.when(moved)
        def _(dst_ref=dst_ref, src_ref=src_ref, scale_ref=scale_ref):
            w = src_ref[...]
            if scale_ref is not None:
                w = w * scale_ref[...]
            dst_ref[...] = w.astype(BF16)

    lam = lam_ref[...]
    lam_val = (jnp.exp(jnp.sum(lam[0:1] * lam[1:2], axis=-1, keepdims=True))
               - jnp.exp(jnp.sum(lam[2:3] * lam[3:4], axis=-1, keepdims=True)) + lambda_init)

    for hh in range(hps):
        c0 = hh * vd
        head = pl.program_id(1) * hps + hh
        hv = jnp.full((1, 1), head + 1, jnp.int32).astype(F32)
        slope = jnp.exp2(hv * (-8.0 / n_heads)) * LOG2E

        @pl.when(qi == 0)
        def _(hh=hh, c0=c0, slope=slope):
            for b in range(nblk):
                vt_ref[hh, b, 0:vd, :] = v_ref[b * t:(b + 1) * t, c0:c0 + vd].astype(F32).T.astype(BF16)
                vt_ref[hh, b, vd:, :] = jnp.ones((vt_ref.shape[2] - vd, t), BF16)
            r = lax.broadcasted_iota(jnp.int32, (t, t), 0)
            c = lax.broadcasted_iota(jnp.int32, (t, t), 1)
            bias_ref[hh, 0] = slope * r.astype(F32)
            bias_ref[hh, 1] = jnp.where((r // CHUNK) <= (c // CHUNK),
                                        slope * (c - jnp.abs(c - r)).astype(F32), -jnp.inf)

        qts = [q_ref[:, c0 + n * hd:c0 + (n + 1) * hd].astype(F32).T.astype(BF16) for n in range(2)]
        for n in range(2):
            m_ref[n] = jnp.full((1, t), -jnp.inf, F32)
            acc_ref[n] = jnp.zeros(acc_ref.shape[1:], F32)

        def qk(kb, s_ref, hh=hh, c0=c0, qts=qts):
            kk = pl.multiple_of(kb * t, t)
            bias = bias_ref[hh, (kb == qi).astype(jnp.int32)]
            for n in range(2):
                s_ref[n] = _dot(k_ref[pl.ds(kk, t), c0 + n * hd:c0 + (n + 1) * hd], qts[n]) + bias

        def softmax_pv(kb, s_ref, hh=hh, slope=slope):
            off = slope * ((kb - qi) * t).astype(F32)
            vt_b = vt_ref[hh, kb]
            for n in range(2):
                s = s_ref[n]
                m_old = m_ref[n]
                m_new = jnp.maximum(m_old, jnp.max(s, axis=0, keepdims=True) + off)
                alpha = jnp.exp2(m_old - m_new)
                p = jnp.exp2(s - (m_new - off))
                m_ref[n] = m_new
                acc_ref[n] = alpha * acc_ref[n] + _dot(vt_b, p.astype(BF16))

        qk(0, sa_ref)

        def body(pair, carry, qk=qk, softmax_pv=softmax_pv):
            kb = 2 * pair
            qk(kb + 1, sb_ref)
            softmax_pv(kb, sa_ref)
            qk(jnp.minimum(kb + 2, qi), sa_ref)
            softmax_pv(kb + 1, sb_ref)
            return carry

        lax.fori_loop(0, (qi + 1) // 2, body, 0)

        @pl.when(qi % 2 == 0)
        def _(softmax_pv=softmax_pv):
            softmax_pv(qi, sa_ref)

        heads_t = [acc_ref[n, 0:vd, :] / acc_ref[n, vd:vd + 1, :] for n in range(2)]
        o_t = heads_t[0] - lam_val * heads_t[1]
        o_t = o_t * lax.rsqrt(jnp.mean(o_t * o_t, axis=0, keepdims=True) + EPS)
        o_ref[:, c0:c0 + vd] = (o_t.T * (g_ref[...] * (1.0 - lambda_init))).astype(o_ref.dtype)


def _diff_attention(qkv, lam, subln_g, *, batch, seq, lambda_init, cast_jobs=()):
    M = qkv.shape[0]
    H = DA_HEADS
    hd = qkv.shape[1] // (6 * H)
    vd = 2 * hd
    t = min(512, seq)
    nq = seq // t
    hps = HEADS_PER_STEP
    hg = H // hps
    n_steps = batch * hg * nq
    wd = hps * vd
    in_specs = [
        pl.BlockSpec((t, wd), lambda b, h, i: (b * nq + i, h)),
        pl.BlockSpec((seq, wd), lambda b, h, i: (b, hg + h)),
        pl.BlockSpec((seq, wd), lambda b, h, i: (b, 2 * hg + h)),
        pl.BlockSpec((4, hd), lambda b, h, i: (0, 0)),
        pl.BlockSpec((1, vd), lambda b, h, i: (0, 0)),
    ]
    args = [qkv, qkv, qkv, lam, subln_g.reshape(1, vd)]
    out_shape = [jax.ShapeDtypeStruct((M, H * vd), BF16)]
    out_specs = [pl.BlockSpec((t, wd), lambda b, h, i: (b * nq + i, h))]
    scaled, chunks = [], []
    for arr, lead, col_scale in cast_jobs:
        rows, cols = arr.shape[-2:]
        chunk = next(c for c in range(BF16_SUBLANES, rows + 1, BF16_SUBLANES)
                     if rows % c == 0 and rows // c <= n_steps)
        n_chunks = rows // chunk

        def chunk_of(b, h, i, n_chunks=n_chunks):
            return (((b * hg + h) * nq + i) * n_chunks) // n_steps

        in_specs.append(pl.BlockSpec((None,) * len(lead) + (chunk, cols),
                                     lambda b, h, i, lead=lead, f=chunk_of: (*lead, f(b, h, i), 0)))
        args.append(arr)
        if col_scale is not None:
            in_specs.append(pl.BlockSpec((1, cols), lambda b, h, i: (0, 0)))
            args.append(col_scale.reshape(1, cols))
        scaled.append(col_scale is not None)
        chunks.append(n_chunks)
        out_shape.append(jax.ShapeDtypeStruct((rows, cols), BF16))
        out_specs.append(pl.BlockSpec((chunk, cols), lambda b, h, i, f=chunk_of: (f(b, h, i), 0)))
    outs = pl.pallas_call(
        functools.partial(_da_attn_kernel, t=t, hd=hd, n_heads=H, lambda_init=lambda_init,
                          cast_scaled=tuple(scaled), cast_chunks=tuple(chunks)),
        out_shape=out_shape,
        grid=(batch, hg, nq),
        in_specs=in_specs,
        out_specs=out_specs,
        scratch_shapes=[pltpu.VMEM((hps, nq, vd + BF16_SUBLANES, t), BF16),
                        pltpu.VMEM((hps, 2, t, t), F32),
                        pltpu.VMEM((2, t, t), F32), pltpu.VMEM((2, t, t), F32),
                        pltpu.VMEM((2, 1, t), F32),
                        pltpu.VMEM((2, vd + BF16_SUBLANES, t), F32)],
        compiler_params=_params("arbitrary", "arbitrary", "arbitrary"),
        name="diff_attention",
    )(*args)
    return outs[0], outs[1:]


def _ret_log_gamma(h):
    return math.log(1.0 - 2.0 ** (-5.0 - h))


def _retention_kernel(q_ref, k_ref, v_ref, g_ref, gn_ref, o_ref, state_ref, dmat_ref, qdec_ref,
                      kdec_ref, *, tb, dk, dv, n_heads):
    t = pl.program_id(1)

    @pl.when((pl.program_id(0) == 0) & (t == 0))
    def _():
        r = lax.broadcasted_iota(jnp.int32, (tb, tb), 0)
        c = lax.broadcasted_iota(jnp.int32, (tb, tb), 1)
        dist = jnp.abs(r - c).astype(F32)
        visible = (c // CHUNK) <= (r // CHUNK)
        pos = lax.broadcasted_iota(jnp.int32, (tb, dk), 0).astype(F32)
        for h in range(n_heads):
            lg = _ret_log_gamma(h)
            dmat_ref[h] = jnp.where(visible, jnp.exp(lg * dist), 0.0)
            qdec_ref[h] = jnp.exp(lg * pos)
            kdec_ref[h] = jnp.exp(lg * (tb - pos))

    @pl.when(t == 0)
    def _():
        state_ref[...] = jnp.zeros_like(state_ref)

    for h in range(n_heads):
        qh = q_ref[:, h * dk:(h + 1) * dk]
        kh = k_ref[:, h * dk:(h + 1) * dk]
        vh = v_ref[:, h * dv:(h + 1) * dv]
        inner = (_dot_nt(qh, kh) * dmat_ref[h]).astype(BF16)
        qd = (qh.astype(F32) * qdec_ref[h]).astype(BF16)
        kd = (kh.astype(F32) * kdec_ref[h]).astype(BF16)
        state = state_ref[h]
        o = _dot(inner, vh) + _dot(qd, state.astype(BF16))
        state_ref[h] = state * math.exp(_ret_log_gamma(h) * tb) + _dot_tn(kd, vh)
        mu = jnp.mean(o, axis=-1, keepdims=True)
        d = o - mu
        y = d * lax.rsqrt(jnp.mean(d * d, axis=-1, keepdims=True) + EPS) * gn_ref[...]
        gate = g_ref[:, h * dv:(h + 1) * dv].astype(F32)
        o_ref[:, h * dv:(h + 1) * dv] = (jax.nn.silu(gate) * y).astype(o_ref.dtype)


def _retention(proj, gn_g, *, batch, seq):
    M, W = proj.shape
    H = RET_HEADS
    dk = W // (6 * H)
    dv = 2 * dk
    tb = min(256, seq)
    nt = seq // tb
    row = lambda b, t: b * nt + t
    return pl.pallas_call(
        functools.partial(_retention_kernel, tb=tb, dk=dk, dv=dv, n_heads=H),
        out_shape=jax.ShapeDtypeStruct((M, H * dv), BF16),
        grid=(batch, nt),
        in_specs=[
            pl.BlockSpec((tb, H * dk), lambda b, t: (row(b, t), 0)),
            pl.BlockSpec((tb, H * dk), lambda b, t: (row(b, t), 1)),
            pl.BlockSpec((tb, H * dv), lambda b, t: (row(b, t), 1)),
            pl.BlockSpec((tb, H * dv), lambda b, t: (row(b, t), 2)),
            pl.BlockSpec((1, dv), lambda b, t: (0, 0)),
        ],
        out_specs=pl.BlockSpec((tb, H * dv), lambda b, t: (row(b, t), 0)),
        scratch_shapes=[pltpu.VMEM((H, dk, dv), F32), pltpu.VMEM((H, tb, tb), F32),
                        pltpu.VMEM((H, tb, dk), F32), pltpu.VMEM((H, tb, dk), F32)],
        compiler_params=_params("arbitrary", "arbitrary"),
        name="retention",
    )(proj, proj, proj, proj, gn_g.reshape(1, dv))


def kernel(x, c, ada_w, ada_b, norm_g, ffn_w_in, ffn_w_out, da_w_qkv, da_lambda, da_subln_g,
           da_w_o, ret_w_qkvg, ret_gn_g, ret_w_o, final_g):
    B, S, D = x.shape
    depth = ada_w.shape[0]
    mods = _ada_mods(c, ada_w, ada_b)

    da_hd = D // (2 * DA_HEADS)
    da_scale = jnp.concatenate([jnp.full((2 * DA_HEADS * da_hd,), da_hd ** -0.5 * LOG2E, F32),
                                jnp.ones((da_w_qkv.shape[2] - 2 * DA_HEADS * da_hd,), F32)])
    ret_dk = D // RET_HEADS
    ret_scale = jnp.concatenate([jnp.ones((RET_HEADS * ret_dk,), F32),
                                 jnp.full((RET_HEADS * ret_dk,), ret_dk ** -0.5, F32),
                                 jnp.ones((ret_w_qkvg.shape[2] - 2 * RET_HEADS * ret_dk,), F32)])
    wb = {
        ("ffn_in", 0, 0): ffn_w_in[0, 0].astype(BF16),
        ("ffn_out", 0, 0): ffn_w_out[0, 0].astype(BF16),
        ("da_qkv", 0): (da_w_qkv[0] * da_scale).astype(BF16),
    }
    jobs = [(("da_o", 0), da_w_o, (0,), None),
            (("ffn_in", 0, 1), ffn_w_in, (0, 1), None),
            (("ffn_out", 0, 1), ffn_w_out, (0, 1), None)]
    for i in range(1, depth):
        jobs += [(("ffn_in", i, k), ffn_w_in, (i, k), None) for k in range(2)]
        jobs += [(("ffn_out", i, k), ffn_w_out, (i, k), None) for k in range(2)]
        j = i // N_MIXERS
        if i % N_MIXERS == 0:
            jobs += [(("da_qkv", j), da_w_qkv, (j,), da_scale), (("da_o", j), da_w_o, (j,), None)]
        else:
            jobs += [(("ret", j), ret_w_qkvg, (j,), ret_scale), (("ret_o", j), ret_w_o, (j,), None)]

    xf = x.reshape(B * S, D)
    for i in range(depth):
        m = mods[i]
        xf = _ffn(xf, m, 0, norm_g[i, 0], wb["ffn_in", i, 0], wb["ffn_out", i, 0], seq=S)
        j = i // N_MIXERS
        if i % N_MIXERS == 0:
            lambda_init = 0.8 - 0.6 * math.exp(-0.3 * i)
            qkv = _norm_mod_matmul(xf, m, 3, norm_g[i, 1], wb["da_qkv", j], seq=S)
            heads, cast = _diff_attention(qkv, da_lambda[j], da_subln_g[j], batch=B, seq=S,
                                          lambda_init=lambda_init,
                                          cast_jobs=[job[1:] for job in jobs])
            wb.update({job[0]: w for job, w in zip(jobs, cast)})
            jobs = []
            xf = _matmul_residual(heads, wb["da_o", j], xf, m, 5, seq=S)
        else:
            proj = _norm_mod_matmul(xf, m, 3, norm_g[i, 1], wb["ret", j], seq=S)
            heads = _retention(proj, ret_gn_g[j], batch=B, seq=S)
            xf = _matmul_residual(heads, wb["ret_o", j], xf, m, 5, seq=S)
        xf = _ffn(xf, m, 6, norm_g[i, 2], wb["ffn_in", i, 1], wb["ffn_out", i, 1],
                  final_g=final_g if i == depth - 1 else None, seq=S)
    return xf.reshape(B, S, D)
```

```python
import functools
import math

import jax
import jax.numpy as jnp
from jax import lax
from jax.experimental import pallas as pl
from jax.experimental.pallas import tpu as pltpu

F32 = jnp.float32
BF16 = jnp.bfloat16

EPS = 1e-5
LOG2E = math.log2(math.e)
CHUNK = 64
N_MOD = 9
N_MIXERS = 2
DA_HEADS = 8
RET_HEADS = 8

BF16_SUBLANES = 16
V7X_VMEM_BYTES = 64 * 1024 * 1024
VMEM_LIMIT_BYTES = V7X_VMEM_BYTES - 8 * 1024 * 1024


def _params(*semantics):
    return pltpu.CompilerParams(dimension_semantics=semantics, vmem_limit_bytes=VMEM_LIMIT_BYTES)


def _dot(a, b):
    return jnp.dot(a, b, preferred_element_type=F32)


def _dot_nt(a, b):
    return lax.dot_general(a, b, (((1,), (1,)), ((), ())), preferred_element_type=F32)


def _dot_tn(a, b):
    return lax.dot_general(a, b, (((0,), (0,)), ((), ())), preferred_element_type=F32)


def _rms(x, g):
    return x * lax.rsqrt(jnp.mean(x * x, axis=-1, keepdims=True) + EPS) * g


NORM_ROWS = 32
FIRST_STEP_PARTS = 4


def _norm_mod_into(h_ref, x_ref, g_ref, shift_ref, scale_ref, rows=None, unrolled=False):
    gs = g_ref[...] * (1.0 + scale_ref[...])
    shift = shift_ref[...]
    start, stop = rows if rows is not None else (0, x_ref.shape[0])

    def chunk(r0):
        x = x_ref[pl.ds(r0, NORM_ROWS), :]
        rinv = lax.rsqrt(jnp.mean(x * x, axis=-1, keepdims=True) + EPS)
        h_ref[pl.ds(r0, NORM_ROWS), :] = (x * rinv * gs + shift).astype(BF16)

    if unrolled:
        for r0 in range(start, stop, NORM_ROWS):
            chunk(r0)
    else:
        def body(i, carry):
            chunk(pl.multiple_of(start + i * NORM_ROWS, NORM_ROWS))
            return carry

        lax.fori_loop(0, (stop - start) // NORM_ROWS, body, 0, unroll=4)


def _ada_kernel(c_ref, w_ref, b_ref, o_ref):
    cs = jax.nn.silu(c_ref[...]).astype(BF16)
    o_ref[...] = _dot(cs, w_ref[...].astype(BF16)) + b_ref[...]


def _ada_mods(c, ada_w, ada_b):
    B, D = c.shape
    L, _, W = ada_w.shape
    rows = 8
    tn = min(1024, W)
    c_pad = jnp.zeros((rows, D), F32).at[:B].set(c)
    out = pl.pallas_call(
        _ada_kernel,
        out_shape=jax.ShapeDtypeStruct((L, rows, W), F32),
        grid=(L, W // tn),
        in_specs=[
            pl.BlockSpec((rows, D), lambda l, n: (0, 0)),
            pl.BlockSpec((None, D, tn), lambda l, n: (l, 0, n)),
            pl.BlockSpec((None, 1, tn), lambda l, n: (l, 0, n)),
        ],
        out_specs=pl.BlockSpec((None, rows, tn), lambda l, n: (l, 0, n)),
        compiler_params=_params("parallel", "parallel"),
        name="ada_mods",
    )(c_pad, ada_w, ada_b.reshape(L, 1, W))
    return out[:, :B].reshape(L, B, N_MOD, D).transpose(0, 2, 1, 3)[:, :, :, None, :]


def _ffn_kernel(x_ref, sh_ref, sc_ref, gt_ref, ng_ref, wg_ref, wu_ref, wo_ref, *rest, final, nf):
    if final:
        fg_ref, o_ref, h_ref = rest
    else:
        o_ref, h_ref = rest
    j = pl.program_id(1)
    assert nf >= 2, "the first and last hidden-dimension steps are distinct code paths"

    def contribution(rows=slice(None)):
        h = h_ref[rows, :]
        a = _dot(h, wg_ref[...])
        b = _dot(h, wu_ref[...])
        act = (jax.nn.silu(a) * b).astype(BF16)
        return _dot(act, wo_ref[...])

    @pl.when(j == 0)
    def _():
        part = x_ref.shape[0] // FIRST_STEP_PARTS
        _norm_mod_into(h_ref, x_ref, ng_ref, sh_ref, sc_ref, rows=(0, part))
        for q in range(FIRST_STEP_PARTS):
            if q + 1 < FIRST_STEP_PARTS:
                _norm_mod_into(h_ref, x_ref, ng_ref, sh_ref, sc_ref,
                               rows=((q + 1) * part, (q + 2) * part), unrolled=True)
            o_ref[q * part:(q + 1) * part, :] = contribution(slice(q * part, (q + 1) * part))

    @pl.when((j > 0) & (j < nf - 1))
    def _():
        o_ref[...] += contribution()

    @pl.when(j == nf - 1)
    def _():
        o_ref[...] = x_ref[...] + (0.5 * gt_ref[...]) * (o_ref[...] + contribution())
        if final:
            for r0 in range(0, x_ref.shape[0], NORM_ROWS):
                o_ref[r0:r0 + NORM_ROWS, :] = _rms(o_ref[r0:r0 + NORM_ROWS, :], fg_ref[...])


def _ffn(x, mods, k0, norm_g, w_in, w_out, final_g=None, *, seq):
    M, D = x.shape
    F = w_out.shape[0]
    tm = min(1024, seq)
    tf = min(512, F)
    nf = F // tf
    bpb = seq // tm
    vec = lambda k: pl.BlockSpec((None, None, 1, D), lambda i, j: (k, i // bpb, 0, 0))
    in_specs = [
        pl.BlockSpec((tm, D), lambda i, j: (i, 0)),
        vec(k0), vec(k0 + 1), vec(k0 + 2),
        pl.BlockSpec((1, D), lambda i, j: (0, 0)),
        pl.BlockSpec((D, tf), lambda i, j: (0, j)),
        pl.BlockSpec((D, tf), lambda i, j: (0, nf + j)),
        pl.BlockSpec((tf, D), lambda i, j: (j, 0)),
    ]
    args = [x, mods, mods, mods, norm_g.reshape(1, D), w_in, w_in, w_out]
    if final_g is not None:
        in_specs.append(pl.BlockSpec((1, D), lambda i, j: (0, 0)))
        args.append(final_g.reshape(1, D))
    return pl.pallas_call(
        functools.partial(_ffn_kernel, final=final_g is not None, nf=nf),
        out_shape=jax.ShapeDtypeStruct((M, D), F32),
        grid=(M // tm, nf),
        in_specs=in_specs,
        out_specs=pl.BlockSpec((tm, D), lambda i, j: (i, 0)),
        scratch_shapes=[pltpu.VMEM((tm, D), BF16)],
        compiler_params=_params("parallel", "arbitrary"),
        name="ffn",
    )(*args)


def _nm_matmul_kernel(x_ref, sh_ref, sc_ref, ng_ref, w_ref, o_ref, h_ref):
    n = pl.program_id(1)

    @pl.when(n == 0)
    def _():
        part = x_ref.shape[0] // FIRST_STEP_PARTS
        _norm_mod_into(h_ref, x_ref, ng_ref, sh_ref, sc_ref, rows=(0, part))
        for q in range(FIRST_STEP_PARTS):
            if q + 1 < FIRST_STEP_PARTS:
                _norm_mod_into(h_ref, x_ref, ng_ref, sh_ref, sc_ref,
                               rows=((q + 1) * part, (q + 2) * part), unrolled=True)
            rows = slice(q * part, (q + 1) * part)
            o_ref[rows, :] = _dot(h_ref[rows, :], w_ref[...]).astype(o_ref.dtype)

    @pl.when(n > 0)
    def _():
        o_ref[...] = _dot(h_ref[...], w_ref[...]).astype(o_ref.dtype)


def _norm_mod_matmul(x, mods, k0, norm_g, w, *, seq):
    M, D = x.shape
    N = w.shape[1]
    tm = min(1024, seq)
    tn = min(2048, N)
    bpb = seq // tm
    vec = lambda k: pl.BlockSpec((None, None, 1, D), lambda i, n: (k, i // bpb, 0, 0))
    return pl.pallas_call(
        _nm_matmul_kernel,
        out_shape=jax.ShapeDtypeStruct((M, N), BF16),
        grid=(M // tm, N // tn),
        in_specs=[
            pl.BlockSpec((tm, D), lambda i, n: (i, 0)),
            vec(k0), vec(k0 + 1),
            pl.BlockSpec((1, D), lambda i, n: (0, 0)),
            pl.BlockSpec((D, tn), lambda i, n: (0, n)),
        ],
        out_specs=pl.BlockSpec((tm, tn), lambda i, n: (i, n)),
        scratch_shapes=[pltpu.VMEM((tm, D), BF16)],
        compiler_params=_params("parallel", "arbitrary"),
        name="norm_mod_matmul",
    )(x, mods, mods, norm_g.reshape(1, D), w)


def _mm_res_kernel(a_ref, w_ref, x_ref, gt_ref, o_ref):
    o_ref[...] = x_ref[...] + gt_ref[...] * _dot(a_ref[...], w_ref[...])


def _matmul_residual(a, w, x, mods, k, *, seq):
    M, K = a.shape
    N = w.shape[1]
    if 2 * K * N * w.dtype.itemsize <= V7X_VMEM_BYTES // 4:
        tm, tn = min(512, seq), N
    else:
        tm, tn = min(1024, seq), min(1024, N)
    bpb = seq // tm
    return pl.pallas_call(
        _mm_res_kernel,
        out_shape=jax.ShapeDtypeStruct((M, N), F32),
        grid=(M // tm, N // tn),
        in_specs=[
            pl.BlockSpec((tm, K), lambda i, n: (i, 0)),
            pl.BlockSpec((K, tn), lambda i, n: (0, n)),
            pl.BlockSpec((tm, tn), lambda i, n: (i, n)),
            pl.BlockSpec((None, None, 1, tn), lambda i, n: (k, i // bpb, 0, n)),
        ],
        out_specs=pl.BlockSpec((tm, tn), lambda i, n: (i, n)),
        compiler_params=_params("parallel", "parallel"),
        name="matmul_residual",
    )(a, w, x, mods)


def _da_attn_kernel(q_ref, k_ref, v_ref, lam_ref, g_ref, *rest, t, hd, n_heads, lambda_init,
                    cast_scaled, cast_chunks):
    n_cast_in = len(cast_scaled) + sum(cast_scaled)
    cast_in = rest[:n_cast_in]
    o_ref = rest[n_cast_in]
    cast_out = rest[n_cast_in + 1:n_cast_in + 1 + len(cast_scaled)]
    vt_ref, bias_ref, sa_ref, sb_ref, m_ref, acc_ref = rest[n_cast_in + 1 + len(cast_scaled):]

    head = pl.program_id(1)
    qi = pl.program_id(2)
    nblk = vt_ref.shape[0]
    vd = 2 * hd
    hv = jnp.full((1, 1), head + 1, jnp.int32).astype(F32)
    slope = jnp.exp2(hv * (-8.0 / n_heads)) * LOG2E

    @pl.when(qi == 0)
    def _():
        for b in range(nblk):
            vt_ref[b, 0:vd, :] = v_ref[b * t:(b + 1) * t, :].astype(F32).T.astype(BF16)
            vt_ref[b, vd:, :] = jnp.ones((vt_ref.shape[1] - vd, t), BF16)
        r = lax.broadcasted_iota(jnp.int32, (t, t), 0)
        c = lax.broadcasted_iota(jnp.int32, (t, t), 1)
        bias_ref[0] = slope * r.astype(F32)
        bias_ref[1] = jnp.where((r // CHUNK) <= (c // CHUNK),
                                slope * (c - jnp.abs(c - r)).astype(F32), -jnp.inf)

    qts = [q_ref[:, n * hd:(n + 1) * hd].astype(F32).T.astype(BF16) for n in range(2)]
    for n in range(2):
        m_ref[n] = jnp.full((1, t), -jnp.inf, F32)
        acc_ref[n] = jnp.zeros(acc_ref.shape[1:], F32)

    def qk(kb, s_ref):
        kk = pl.multiple_of(kb * t, t)
        bias = bias_ref[(kb == qi).astype(jnp.int32)]
        for n in range(2):
            s_ref[n] = _dot(k_ref[pl.ds(kk, t), n * hd:(n + 1) * hd], qts[n]) + bias

    def softmax_pv(kb, s_ref):
        off = slope * ((kb - qi) * t).astype(F32)
        vt_b = vt_ref[kb]
        for n in range(2):
            s = s_ref[n]
            m_old = m_ref[n]
            m_new = jnp.maximum(m_old, jnp.max(s, axis=0, keepdims=True) + off)
            alpha = jnp.exp2(m_old - m_new)
            p = jnp.exp2(s - (m_new - off))
            m_ref[n] = m_new
            acc_ref[n] = alpha * acc_ref[n] + _dot(vt_b, p.astype(BF16))

    qk(0, sa_ref)

    step = (pl.program_id(0) * pl.num_programs(1) + head) * pl.num_programs(2) + qi
    n_steps = pl.num_programs(0) * pl.num_programs(1) * pl.num_programs(2)
    pos = 0
    for dst_ref, has_scale, n_chunks in zip(cast_out, cast_scaled, cast_chunks):
        src_ref = cast_in[pos]
        scale_ref = cast_in[pos + 1] if has_scale else None
        pos += 2 if has_scale else 1
        moved = (step == 0) | ((step * n_chunks) // n_steps != ((step - 1) * n_chunks) // n_steps)

        @pl.when(moved)
        def _(dst_ref=dst_ref, src_ref=src_ref, scale_ref=scale_ref):
            w = src_ref[...]
            if scale_ref is not None:
                w = w * scale_ref[...]
            dst_ref[...] = w.astype(BF16)

    def body(pair, carry):
        kb = 2 * pair
        qk(kb + 1, sb_ref)
        softmax_pv(kb, sa_ref)
        qk(jnp.minimum(kb + 2, qi), sa_ref)
        softmax_pv(kb + 1, sb_ref)
        return carry

    lax.fori_loop(0, (qi + 1) // 2, body, 0)

    @pl.when(qi % 2 == 0)
    def _():
        softmax_pv(qi, sa_ref)

    lam = lam_ref[...]
    lam_val = (jnp.exp(jnp.sum(lam[0:1] * lam[1:2], axis=-1, keepdims=True))
               - jnp.exp(jnp.sum(lam[2:3] * lam[3:4], axis=-1, keepdims=True)) + lambda_init)
    heads_t = [acc_ref[n, 0:vd, :] / acc_ref[n, vd:vd + 1, :] for n in range(2)]
    o_t = heads_t[0] - lam_val * heads_t[1]
    o_t = o_t * lax.rsqrt(jnp.mean(o_t * o_t, axis=0, keepdims=True) + EPS)
    o_ref[...] = (o_t.T * (g_ref[...] * (1.0 - lambda_init))).astype(o_ref.dtype)


def _diff_attention(qkv, lam, subln_g, *, batch, seq, lambda_init, cast_jobs=()):
    M = qkv.shape[0]
    H = DA_HEADS
    hd = qkv.shape[1] // (6 * H)
    vd = 2 * hd
    t = min(512, seq)
    nq = seq // t
    n_steps = batch * H * nq
    in_specs = [
        pl.BlockSpec((t, vd), lambda b, h, i: (b * nq + i, h)),
        pl.BlockSpec((seq, vd), lambda b, h, i: (b, H + h)),
        pl.BlockSpec((seq, vd), lambda b, h, i: (b, 2 * H + h)),
        pl.BlockSpec((4, hd), lambda b, h, i: (0, 0)),
        pl.BlockSpec((1, vd), lambda b, h, i: (0, 0)),
    ]
    args = [qkv, qkv, qkv, lam, subln_g.reshape(1, vd)]
    out_shape = [jax.ShapeDtypeStruct((M, H * vd), BF16)]
    out_specs = [pl.BlockSpec((t, vd), lambda b, h, i: (b * nq + i, h))]
    scaled, chunks = [], []
    for arr, lead, col_scale in cast_jobs:
        rows, cols = arr.shape[-2:]
        chunk = BF16_SUBLANES * pl.cdiv(rows, BF16_SUBLANES * n_steps)
        assert rows % chunk == 0, (rows, chunk)
        n_chunks = rows // chunk

        def chunk_of(b, h, i, n_chunks=n_chunks):
            return (((b * H + h) * nq + i) * n_chunks) // n_steps

        in_specs.append(pl.BlockSpec((None,) * len(lead) + (chunk, cols),
                                     lambda b, h, i, lead=lead, f=chunk_of: (*lead, f(b, h, i), 0)))
        args.append(arr)
        if col_scale is not None:
            in_specs.append(pl.BlockSpec((1, cols), lambda b, h, i: (0, 0)))
            args.append(col_scale.reshape(1, cols))
        scaled.append(col_scale is not None)
        chunks.append(n_chunks)
        out_shape.append(jax.ShapeDtypeStruct((rows, cols), BF16))
        out_specs.append(pl.BlockSpec((chunk, cols), lambda b, h, i, f=chunk_of: (f(b, h, i), 0)))
    outs = pl.pallas_call(
        functools.partial(_da_attn_kernel, t=t, hd=hd, n_heads=H, lambda_init=lambda_init,
                          cast_scaled=tuple(scaled), cast_chunks=tuple(chunks)),
        out_shape=out_shape,
        grid=(batch, H, nq),
        in_specs=in_specs,
        out_specs=out_specs,
        scratch_shapes=[pltpu.VMEM((nq, vd + BF16_SUBLANES, t), BF16),
                        pltpu.VMEM((2, t, t), F32),
                        pltpu.VMEM((2, t, t), F32), pltpu.VMEM((2, t, t), F32),
                        pltpu.VMEM((2, 1, t), F32),
                        pltpu.VMEM((2, vd + BF16_SUBLANES, t), F32)],
        compiler_params=_params("arbitrary", "arbitrary", "arbitrary"),
        name="diff_attention",
    )(*args)
    return outs[0], outs[1:]


def _ret_log_gamma(h):
    return math.log(1.0 - 2.0 ** (-5.0 - h))


def _retention_kernel(q_ref, k_ref, v_ref, g_ref, gn_ref, wo_ref, x_ref, gt_ref, o_ref, state_ref,
                      dmat_ref, qdec_ref, kdec_ref, oga_ref, ogb_ref, *, tb, dk, dv, n_heads, nt):
    s = pl.program_id(0)

    @pl.when(s == 0)
    def _():
        r = lax.broadcasted_iota(jnp.int32, (tb, tb), 0)
        c = lax.broadcasted_iota(jnp.int32, (tb, tb), 1)
        dist = jnp.abs(r - c).astype(F32)
        visible = (c // CHUNK) <= (r // CHUNK)
        pos = lax.broadcasted_iota(jnp.int32, (tb, dk), 0).astype(F32)
        for h in range(n_heads):
            lg = _ret_log_gamma(h)
            dmat_ref[h] = jnp.where(visible, jnp.exp(lg * dist), 0.0)
            qdec_ref[h] = jnp.exp(lg * pos)
            kdec_ref[h] = jnp.exp(lg * (tb - pos))
        ogb_ref[...] = jnp.zeros_like(ogb_ref)

    @pl.when(s % nt == 0)
    def _():
        state_ref[...] = jnp.zeros_like(state_ref)

    slab = o_ref.shape[1] // n_heads

    def step(og_ref, og_prev_ref):
        for h in range(n_heads):
            qh = q_ref[:, h * dk:(h + 1) * dk]
            kh = k_ref[:, h * dk:(h + 1) * dk]
            vh = v_ref[:, h * dv:(h + 1) * dv]
            inner = (_dot_nt(qh, kh) * dmat_ref[h]).astype(BF16)
            qd = (qh.astype(F32) * qdec_ref[h]).astype(BF16)
            kd = (kh.astype(F32) * kdec_ref[h]).astype(BF16)
            state = state_ref[h]
            o = _dot(inner, vh) + _dot(qd, state.astype(BF16))
            state_ref[h] = state * math.exp(_ret_log_gamma(h) * tb) + _dot_tn(kd, vh)
            mu = jnp.mean(o, axis=-1, keepdims=True)
            d = o - mu
            y = d * lax.rsqrt(jnp.mean(d * d, axis=-1, keepdims=True) + EPS) * gn_ref[...]
            gate = g_ref[:, h * dv:(h + 1) * dv].astype(F32)
            og_ref[:, h * dv:(h + 1) * dv] = (jax.nn.silu(gate) * y).astype(og_ref.dtype)
            cols = slice(h * slab, (h + 1) * slab)
            o_ref[:, cols] = x_ref[:, cols] + gt_ref[:, cols] * _dot(og_prev_ref[...], wo_ref[:, cols])

    @pl.when(s % 2 == 0)
    def _():
        step(oga_ref, ogb_ref)

    @pl.when(s % 2 == 1)
    def _():
        step(ogb_ref, oga_ref)


def _retention_residual(proj, gn_g, w_o, x, mods, k, *, batch, seq):
    M, W = proj.shape
    D = x.shape[1]
    H = RET_HEADS
    dk = W // (6 * H)
    dv = 2 * dk
    tb = min(256, seq)
    nt = seq // tb
    n_blocks = batch * nt
    cur = lambda s: jnp.minimum(s, n_blocks - 1)
    prev = lambda s: jnp.maximum(s - 1, 0)
    return pl.pallas_call(
        functools.partial(_retention_kernel, tb=tb, dk=dk, dv=dv, n_heads=H, nt=nt),
        out_shape=jax.ShapeDtypeStruct((M, D), F32),
        grid=(n_blocks + 1,),
        in_specs=[
            pl.BlockSpec((tb, H * dk), lambda s: (cur(s), 0)),
            pl.BlockSpec((tb, H * dk), lambda s: (cur(s), 1)),
            pl.BlockSpec((tb, H * dv), lambda s: (cur(s), 1)),
            pl.BlockSpec((tb, H * dv), lambda s: (cur(s), 2)),
            pl.BlockSpec((1, dv), lambda s: (0, 0)),
            pl.BlockSpec((H * dv, D), lambda s: (0, 0), pipeline_mode=pl.Buffered(1)),
            pl.BlockSpec((tb, D), lambda s: (prev(s), 0)),
            pl.BlockSpec((None, None, 1, D), lambda s: (k, prev(s) // nt, 0, 0)),
        ],
        out_specs=pl.BlockSpec((tb, D), lambda s: (prev(s), 0)),
        scratch_shapes=[pltpu.VMEM((H, dk, dv), F32), pltpu.VMEM((H, tb, tb), F32),
                        pltpu.VMEM((H, tb, dk), F32), pltpu.VMEM((H, tb, dk), F32),
                        pltpu.VMEM((tb, H * dv), BF16), pltpu.VMEM((tb, H * dv), BF16)],
        compiler_params=_params("arbitrary"),
        name="retention_residual",
    )(proj, proj, proj, proj, gn_g.reshape(1, dv), w_o, x, mods)


def kernel(x, c, ada_w, ada_b, norm_g, ffn_w_in, ffn_w_out, da_w_qkv, da_lambda, da_subln_g,
           da_w_o, ret_w_qkvg, ret_gn_g, ret_w_o, final_g):
    B, S, D = x.shape
    depth = ada_w.shape[0]
    mods = _ada_mods(c, ada_w, ada_b)

    da_hd = D // (2 * DA_HEADS)
    da_scale = jnp.concatenate([jnp.full((2 * DA_HEADS * da_hd,), da_hd ** -0.5 * LOG2E, F32),
                                jnp.ones((da_w_qkv.shape[2] - 2 * DA_HEADS * da_hd,), F32)])
    ret_dk = D // RET_HEADS
    ret_scale = jnp.concatenate([jnp.ones((RET_HEADS * ret_dk,), F32),
                                 jnp.full((RET_HEADS * ret_dk,), ret_dk ** -0.5, F32),
                                 jnp.ones((ret_w_qkvg.shape[2] - 2 * RET_HEADS * ret_dk,), F32)])
    wb = {
        ("ffn_in", 0, 0): ffn_w_in[0, 0].astype(BF16),
        ("ffn_out", 0, 0): ffn_w_out[0, 0].astype(BF16),
        ("da_qkv", 0): (da_w_qkv[0] * da_scale).astype(BF16),
    }
    jobs = [(("da_o", 0), da_w_o, (0,), None),
            (("ffn_in", 0, 1), ffn_w_in, (0, 1), None),
            (("ffn_out", 0, 1), ffn_w_out, (0, 1), None)]
    for i in range(1, depth):
        jobs += [(("ffn_in", i, k), ffn_w_in, (i, k), None) for k in range(2)]
        jobs += [(("ffn_out", i, k), ffn_w_out, (i, k), None) for k in range(2)]
        j = i // N_MIXERS
        if i % N_MIXERS == 0:
            jobs += [(("da_qkv", j), da_w_qkv, (j,), da_scale), (("da_o", j), da_w_o, (j,), None)]
        else:
            jobs += [(("ret", j), ret_w_qkvg, (j,), ret_scale), (("ret_o", j), ret_w_o, (j,), None)]

    xf = x.reshape(B * S, D)
    for i in range(depth):
        m = mods[i]
        xf = _ffn(xf, m, 0, norm_g[i, 0], wb["ffn_in", i, 0], wb["ffn_out", i, 0], seq=S)
        j = i // N_MIXERS
        if i % N_MIXERS == 0:
            lambda_init = 0.8 - 0.6 * math.exp(-0.3 * i)
            qkv = _norm_mod_matmul(xf, m, 3, norm_g[i, 1], wb["da_qkv", j], seq=S)
            heads, cast = _diff_attention(qkv, da_lambda[j], da_subln_g[j], batch=B, seq=S,
                                          lambda_init=lambda_init,
                                          cast_jobs=[job[1:] for job in jobs])
            wb.update({job[0]: w for job, w in zip(jobs, cast)})
            jobs = []
            xf = _matmul_residual(heads, wb["da_o", j], xf, m, 5, seq=S)
        else:
            proj = _norm_mod_matmul(xf, m, 3, norm_g[i, 1], wb["ret", j], seq=S)
            xf = _retention_residual(proj, ret_gn_g[j], wb["ret_o", j], xf, m, 5, batch=B, seq=S)
        xf = _ffn(xf, m, 6, norm_g[i, 2], wb["ffn_in", i, 1], wb["ffn_out", i, 1],
                  final_g=final_g if i == depth - 1 else None, seq=S)
    return xf.reshape(B, S, D)
```

```python
import functools
import math

import jax
import jax.numpy as jnp
from jax import lax
from jax.experimental import pallas as pl
from jax.experimental.pallas import tpu as pltpu

F32 = jnp.float32
BF16 = jnp.bfloat16

EPS = 1e-5
LOG2E = math.log2(math.e)
CHUNK = 64
N_MOD = 9
N_MIXERS = 2
DA_HEADS = 8
RET_HEADS = 8

BF16_SUBLANES = 16
V7X_VMEM_BYTES = 64 * 1024 * 1024
VMEM_LIMIT_BYTES = V7X_VMEM_BYTES - 8 * 1024 * 1024


def _params(*semantics):
    return pltpu.CompilerParams(dimension_semantics=semantics, vmem_limit_bytes=VMEM_LIMIT_BYTES)


def _dot(a, b):
    return jnp.dot(a, b, preferred_element_type=F32)


def _dot_nt(a, b):
    return lax.dot_general(a, b, (((1,), (1,)), ((), ())), preferred_element_type=F32)


def _dot_tn(a, b):
    return lax.dot_general(a, b, (((0,), (0,)), ((), ())), preferred_element_type=F32)


def _rms(x, g):
    return x * lax.rsqrt(jnp.mean(x * x, axis=-1, keepdims=True) + EPS) * g


NORM_ROWS = 32
FIRST_STEP_PARTS = 4


def _norm_mod_into(h_ref, x_ref, g_ref, shift_ref, scale_ref, rows=None, unrolled=False):
    gs = g_ref[...] * (1.0 + scale_ref[...])
    shift = shift_ref[...]
    start, stop = rows if rows is not None else (0, x_ref.shape[0])

    def chunk(r0):
        x = x_ref[pl.ds(r0, NORM_ROWS), :]
        rinv = lax.rsqrt(jnp.mean(x * x, axis=-1, keepdims=True) + EPS)
        h_ref[pl.ds(r0, NORM_ROWS), :] = (x * rinv * gs + shift).astype(BF16)

    if unrolled:
        for r0 in range(start, stop, NORM_ROWS):
            chunk(r0)
    else:
        def body(i, carry):
            chunk(pl.multiple_of(start + i * NORM_ROWS, NORM_ROWS))
            return carry

        lax.fori_loop(0, (stop - start) // NORM_ROWS, body, 0, unroll=4)


def _ada_kernel(c_ref, w_ref, b_ref, o_ref):
    cs = jax.nn.silu(c_ref[...]).astype(BF16)
    o_ref[...] = _dot(cs, w_ref[...].astype(BF16)) + b_ref[...]


def _ada_mods(c, ada_w, ada_b):
    B, D = c.shape
    L, _, W = ada_w.shape
    rows = 8
    tn = min(1024, W)
    c_pad = jnp.zeros((rows, D), F32).at[:B].set(c)
    out = pl.pallas_call(
        _ada_kernel,
        out_shape=jax.ShapeDtypeStruct((L, rows, W), F32),
        grid=(L, W // tn),
        in_specs=[
            pl.BlockSpec((rows, D), lambda l, n: (0, 0)),
            pl.BlockSpec((None, D, tn), lambda l, n: (l, 0, n)),
            pl.BlockSpec((None, 1, tn), lambda l, n: (l, 0, n)),
        ],
        out_specs=pl.BlockSpec((None, rows, tn), lambda l, n: (l, 0, n)),
        compiler_params=_params("parallel", "parallel"),
        name="ada_mods",
    )(c_pad, ada_w, ada_b.reshape(L, 1, W))
    return out[:, :B].reshape(L, B, N_MOD, D).transpose(0, 2, 1, 3)[:, :, :, None, :]


def _ffn_kernel(x_ref, sh_ref, sc_ref, gt_ref, ng_ref, wg_ref, wu_ref, wo_ref, *rest, final, nf):
    if final:
        fg_ref, o_ref, h_ref = rest
    else:
        o_ref, h_ref = rest
    j = pl.program_id(1)
    assert nf >= 2, "the first and last hidden-dimension steps are distinct code paths"

    def contribution(rows=slice(None)):
        h = h_ref[rows, :]
        a = _dot(h, wg_ref[...])
        b = _dot(h, wu_ref[...])
        act = (jax.nn.silu(a) * b).astype(BF16)
        return _dot(act, wo_ref[...])

    @pl.when(j == 0)
    def _():
        part = x_ref.shape[0] // FIRST_STEP_PARTS
        _norm_mod_into(h_ref, x_ref, ng_ref, sh_ref, sc_ref, rows=(0, part))
        for q in range(FIRST_STEP_PARTS):
            if q + 1 < FIRST_STEP_PARTS:
                _norm_mod_into(h_ref, x_ref, ng_ref, sh_ref, sc_ref,
                               rows=((q + 1) * part, (q + 2) * part), unrolled=True)
            o_ref[q * part:(q + 1) * part, :] = contribution(slice(q * part, (q + 1) * part))

    @pl.when((j > 0) & (j < nf - 1))
    def _():
        o_ref[...] += contribution()

    @pl.when(j == nf - 1)
    def _():
        o_ref[...] = x_ref[...] + (0.5 * gt_ref[...]) * (o_ref[...] + contribution())
        if final:
            for r0 in range(0, x_ref.shape[0], NORM_ROWS):
                o_ref[r0:r0 + NORM_ROWS, :] = _rms(o_ref[r0:r0 + NORM_ROWS, :], fg_ref[...])


def _ffn(x, mods, k0, norm_g, w_in, w_out, final_g=None, *, seq):
    M, D = x.shape
    F = w_out.shape[0]
    tm = min(1024, seq)
    tf = min(512, F)
    nf = F // tf
    bpb = seq // tm
    vec = lambda k: pl.BlockSpec((None, None, 1, D), lambda i, j: (k, i // bpb, 0, 0))
    in_specs = [
        pl.BlockSpec((tm, D), lambda i, j: (i, 0)),
        vec(k0), vec(k0 + 1), vec(k0 + 2),
        pl.BlockSpec((1, D), lambda i, j: (0, 0)),
        pl.BlockSpec((D, tf), lambda i, j: (0, j)),
        pl.BlockSpec((D, tf), lambda i, j: (0, nf + j)),
        pl.BlockSpec((tf, D), lambda i, j: (j, 0)),
    ]
    args = [x, mods, mods, mods, norm_g.reshape(1, D), w_in, w_in, w_out]
    if final_g is not None:
        in_specs.append(pl.BlockSpec((1, D), lambda i, j: (0, 0)))
        args.append(final_g.reshape(1, D))
    return pl.pallas_call(
        functools.partial(_ffn_kernel, final=final_g is not None, nf=nf),
        out_shape=jax.ShapeDtypeStruct((M, D), F32),
        grid=(M // tm, nf),
        in_specs=in_specs,
        out_specs=pl.BlockSpec((tm, D), lambda i, j: (i, 0)),
        scratch_shapes=[pltpu.VMEM((tm, D), BF16)],
        compiler_params=_params("parallel", "arbitrary"),
        name="ffn",
    )(*args)


def _nm_matmul_kernel(x_ref, sh_ref, sc_ref, ng_ref, w_ref, o_ref, h_ref):
    n = pl.program_id(1)

    @pl.when(n == 0)
    def _():
        part = x_ref.shape[0] // FIRST_STEP_PARTS
        _norm_mod_into(h_ref, x_ref, ng_ref, sh_ref, sc_ref, rows=(0, part))
        for q in range(FIRST_STEP_PARTS):
            if q + 1 < FIRST_STEP_PARTS:
                _norm_mod_into(h_ref, x_ref, ng_ref, sh_ref, sc_ref,
                               rows=((q + 1) * part, (q + 2) * part), unrolled=True)
            rows = slice(q * part, (q + 1) * part)
            o_ref[rows, :] = _dot(h_ref[rows, :], w_ref[...]).astype(o_ref.dtype)

    @pl.when(n > 0)
    def _():
        o_ref[...] = _dot(h_ref[...], w_ref[...]).astype(o_ref.dtype)


def _norm_mod_matmul(x, mods, k0, norm_g, w, *, seq):
    M, D = x.shape
    N = w.shape[1]
    tm = min(1024, seq)
    tn = min(2048, N)
    bpb = seq // tm
    vec = lambda k: pl.BlockSpec((None, None, 1, D), lambda i, n: (k, i // bpb, 0, 0))
    return pl.pallas_call(
        _nm_matmul_kernel,
        out_shape=jax.ShapeDtypeStruct((M, N), BF16),
        grid=(M // tm, N // tn),
        in_specs=[
            pl.BlockSpec((tm, D), lambda i, n: (i, 0)),
            vec(k0), vec(k0 + 1),
            pl.BlockSpec((1, D), lambda i, n: (0, 0)),
            pl.BlockSpec((D, tn), lambda i, n: (0, n)),
        ],
        out_specs=pl.BlockSpec((tm, tn), lambda i, n: (i, n)),
        scratch_shapes=[pltpu.VMEM((tm, D), BF16)],
        compiler_params=_params("parallel", "arbitrary"),
        name="norm_mod_matmul",
    )(x, mods, mods, norm_g.reshape(1, D), w)


def _mm_res_kernel(a_ref, w_ref, x_ref, gt_ref, o_ref):
    o_ref[...] = x_ref[...] + gt_ref[...] * _dot(a_ref[...], w_ref[...])


def _matmul_residual(a, w, x, mods, k, *, seq):
    M, K = a.shape
    N = w.shape[1]
    if 2 * K * N * w.dtype.itemsize <= V7X_VMEM_BYTES // 4:
        tm, tn = min(512, seq), N
    else:
        tm, tn = min(1024, seq), min(1024, N)
    bpb = seq // tm
    return pl.pallas_call(
        _mm_res_kernel,
        out_shape=jax.ShapeDtypeStruct((M, N), F32),
        grid=(M // tm, N // tn),
        in_specs=[
            pl.BlockSpec((tm, K), lambda i, n: (i, 0)),
            pl.BlockSpec((K, tn), lambda i, n: (0, n)),
            pl.BlockSpec((tm, tn), lambda i, n: (i, n)),
            pl.BlockSpec((None, None, 1, tn), lambda i, n: (k, i // bpb, 0, n)),
        ],
        out_specs=pl.BlockSpec((tm, tn), lambda i, n: (i, n)),
        compiler_params=_params("parallel", "parallel"),
        name="matmul_residual",
    )(a, w, x, mods)


def _da_attn_kernel(q_ref, k_ref, v_ref, lam_ref, g_ref, *rest, t, hd, n_heads, lambda_init,
                    cast_scaled, cast_chunks):
    n_cast_in = len(cast_scaled) + sum(cast_scaled)
    cast_in = rest[:n_cast_in]
    o_ref = rest[n_cast_in]
    cast_out = rest[n_cast_in + 1:n_cast_in + 1 + len(cast_scaled)]
    vt_ref, bias_ref, sa_ref, sb_ref, m_ref, acc_ref = rest[n_cast_in + 1 + len(cast_scaled):]

    head = pl.program_id(1)
    qi = pl.program_id(2)
    nblk = vt_ref.shape[0]
    vd = 2 * hd
    hv = jnp.full((1, 1), head + 1, jnp.int32).astype(F32)
    slope = jnp.exp2(hv * (-8.0 / n_heads)) * LOG2E

    @pl.when(qi == 0)
    def _():
        for b in range(nblk):
            vt_ref[b, 0:vd, :] = v_ref[b * t:(b + 1) * t, :].astype(F32).T.astype(BF16)
            vt_ref[b, vd:, :] = jnp.ones((vt_ref.shape[1] - vd, t), BF16)
        r = lax.broadcasted_iota(jnp.int32, (t, t), 0)
        c = lax.broadcasted_iota(jnp.int32, (t, t), 1)
        bias_ref[0] = slope * r.astype(F32)
        bias_ref[1] = jnp.where((r // CHUNK) <= (c // CHUNK),
                                slope * (c - jnp.abs(c - r)).astype(F32), -jnp.inf)

    qts = [q_ref[:, n * hd:(n + 1) * hd].astype(F32).T.astype(BF16) for n in range(2)]
    for n in range(2):
        m_ref[n] = jnp.full((1, t), -jnp.inf, F32)
        acc_ref[n] = jnp.zeros(acc_ref.shape[1:], F32)

    def qk(kb, s_ref):
        kk = pl.multiple_of(kb * t, t)
        bias = bias_ref[(kb == qi).astype(jnp.int32)]
        for n in range(2):
            s_ref[n] = _dot(k_ref[pl.ds(kk, t), n * hd:(n + 1) * hd], qts[n]) + bias

    def softmax_pv(kb, s_ref):
        off = slope * ((kb - qi) * t).astype(F32)
        vt_b = vt_ref[kb]
        for n in range(2):
            s = s_ref[n]
            m_old = m_ref[n]
            m_new = jnp.maximum(m_old, jnp.max(s, axis=0, keepdims=True) + off)
            alpha = jnp.exp2(m_old - m_new)
            p = jnp.exp2(s - (m_new - off))
            m_ref[n] = m_new
            acc_ref[n] = alpha * acc_ref[n] + _dot(vt_b, p.astype(BF16))

    qk(0, sa_ref)

    step = (pl.program_id(0) * pl.num_programs(1) + head) * pl.num_programs(2) + qi
    n_steps = pl.num_programs(0) * pl.num_programs(1) * pl.num_programs(2)
    pos = 0
    for dst_ref, has_scale, n_chunks in zip(cast_out, cast_scaled, cast_chunks):
        src_ref = cast_in[pos]
        scale_ref = cast_in[pos + 1] if has_scale else None
        pos += 2 if has_scale else 1
        moved = (step == 0) | ((step * n_chunks) // n_steps != ((step - 1) * n_chunks) // n_steps)

        @pl.when(moved)
        def _(dst_ref=dst_ref, src_ref=src_ref, scale_ref=scale_ref):
            w = src_ref[...]
            if scale_ref is not None:
                w = w * scale_ref[...]
            dst_ref[...] = w.astype(BF16)

    def pair(kb):
        qk(kb + 1, sb_ref)
        softmax_pv(kb, sa_ref)
        qk(jnp.minimum(kb + 2, qi), sa_ref)
        softmax_pv(kb + 1, sb_ref)

    n_blocks = qi + 1
    n_quads = n_blocks // 4

    def quad_body(i, carry):
        pair(4 * i)
        pair(4 * i + 2)
        return carry

    lax.fori_loop(0, n_quads, quad_body, 0)

    def pair_body(i, carry):
        pair(4 * n_quads + 2 * i)
        return carry

    lax.fori_loop(0, (n_blocks - 4 * n_quads) // 2, pair_body, 0)

    @pl.when(n_blocks % 2 == 1)
    def _():
        softmax_pv(qi, sa_ref)

    lam = lam_ref[...]
    lam_val = (jnp.exp(jnp.sum(lam[0:1] * lam[1:2], axis=-1, keepdims=True))
               - jnp.exp(jnp.sum(lam[2:3] * lam[3:4], axis=-1, keepdims=True)) + lambda_init)
    heads_t = [acc_ref[n, 0:vd, :] / acc_ref[n, vd:vd + 1, :] for n in range(2)]
    o_t = heads_t[0] - lam_val * heads_t[1]
    o_t = o_t * lax.rsqrt(jnp.mean(o_t * o_t, axis=0, keepdims=True) + EPS)
    o_ref[...] = (o_t.T * (g_ref[...] * (1.0 - lambda_init))).astype(o_ref.dtype)


def _diff_attention(qkv, lam, subln_g, *, batch, seq, lambda_init, cast_jobs=()):
    M = qkv.shape[0]
    H = DA_HEADS
    hd = qkv.shape[1] // (6 * H)
    vd = 2 * hd
    t = min(512, seq)
    nq = seq // t
    n_steps = batch * H * nq
    in_specs = [
        pl.BlockSpec((t, vd), lambda b, h, i: (b * nq + i, h)),
        pl.BlockSpec((seq, vd), lambda b, h, i: (b, H + h)),
        pl.BlockSpec((seq, vd), lambda b, h, i: (b, 2 * H + h)),
        pl.BlockSpec((4, hd), lambda b, h, i: (0, 0)),
        pl.BlockSpec((1, vd), lambda b, h, i: (0, 0)),
    ]
    args = [qkv, qkv, qkv, lam, subln_g.reshape(1, vd)]
    out_shape = [jax.ShapeDtypeStruct((M, H * vd), BF16)]
    out_specs = [pl.BlockSpec((t, vd), lambda b, h, i: (b * nq + i, h))]
    scaled, chunks = [], []
    for arr, lead, col_scale in cast_jobs:
        rows, cols = arr.shape[-2:]
        chunk = BF16_SUBLANES * pl.cdiv(rows, BF16_SUBLANES * n_steps)
        assert rows % chunk == 0, (rows, chunk)
        n_chunks = rows // chunk

        def chunk_of(b, h, i, n_chunks=n_chunks):
            return (((b * H + h) * nq + i) * n_chunks) // n_steps

        in_specs.append(pl.BlockSpec((None,) * len(lead) + (chunk, cols),
                                     lambda b, h, i, lead=lead, f=chunk_of: (*lead, f(b, h, i), 0)))
        args.append(arr)
        if col_scale is not None:
            in_specs.append(pl.BlockSpec((1, cols), lambda b, h, i: (0, 0)))
            args.append(col_scale.reshape(1, cols))
        scaled.append(col_scale is not None)
        chunks.append(n_chunks)
        out_shape.append(jax.ShapeDtypeStruct((rows, cols), BF16))
        out_specs.append(pl.BlockSpec((chunk, cols), lambda b, h, i, f=chunk_of: (f(b, h, i), 0)))
    outs = pl.pallas_call(
        functools.partial(_da_attn_kernel, t=t, hd=hd, n_heads=H, lambda_init=lambda_init,
                          cast_scaled=tuple(scaled), cast_chunks=tuple(chunks)),
        out_shape=out_shape,
        grid=(batch, H, nq),
        in_specs=in_specs,
        out_specs=out_specs,
        scratch_shapes=[pltpu.VMEM((nq, vd + BF16_SUBLANES, t), BF16),
                        pltpu.VMEM((2, t, t), F32),
                        pltpu.VMEM((2, t, t), F32), pltpu.VMEM((2, t, t), F32),
                        pltpu.VMEM((2, 1, t), F32),
                        pltpu.VMEM((2, vd + BF16_SUBLANES, t), F32)],
        compiler_params=_params("arbitrary", "arbitrary", "arbitrary"),
        name="diff_attention",
    )(*args)
    return outs[0], outs[1:]


def _ret_log_gamma(h):
    return math.log(1.0 - 2.0 ** (-5.0 - h))


def _retention_kernel(q_ref, k_ref, v_ref, g_ref, gn_ref, o_ref, state_ref, dmat_ref, qdec_ref,
                      kdec_ref, *, tb, dk, dv, n_heads):
    t = pl.program_id(1)

    @pl.when((pl.program_id(0) == 0) & (t == 0))
    def _():
        r = lax.broadcasted_iota(jnp.int32, (tb, tb), 0)
        c = lax.broadcasted_iota(jnp.int32, (tb, tb), 1)
        dist = jnp.abs(r - c).astype(F32)
        visible = (c // CHUNK) <= (r // CHUNK)
        pos = lax.broadcasted_iota(jnp.int32, (tb, dk), 0).astype(F32)
        for h in range(n_heads):
            lg = _ret_log_gamma(h)
            dmat_ref[h] = jnp.where(visible, jnp.exp(lg * dist), 0.0)
            qdec_ref[h] = jnp.exp(lg * pos)
            kdec_ref[h] = jnp.exp(lg * (tb - pos))

    @pl.when(t == 0)
    def _():
        state_ref[...] = jnp.zeros_like(state_ref)

    for h in range(n_heads):
        qh = q_ref[:, h * dk:(h + 1) * dk]
        kh = k_ref[:, h * dk:(h + 1) * dk]
        vh = v_ref[:, h * dv:(h + 1) * dv]
        inner = (_dot_nt(qh, kh) * dmat_ref[h]).astype(BF16)
        qd = (qh.astype(F32) * qdec_ref[h]).astype(BF16)
        kd = (kh.astype(F32) * kdec_ref[h]).astype(BF16)
        state = state_ref[h]
        o = _dot(inner, vh) + _dot(qd, state.astype(BF16))
        state_ref[h] = state * math.exp(_ret_log_gamma(h) * tb) + _dot_tn(kd, vh)
        mu = jnp.mean(o, axis=-1, keepdims=True)
        d = o - mu
        y = d * lax.rsqrt(jnp.mean(d * d, axis=-1, keepdims=True) + EPS) * gn_ref[...]
        gate = g_ref[:, h * dv:(h + 1) * dv].astype(F32)
        o_ref[:, h * dv:(h + 1) * dv] = (jax.nn.silu(gate) * y).astype(o_ref.dtype)


def _retention(proj, gn_g, *, batch, seq):
    M, W = proj.shape
    H = RET_HEADS
    dk = W // (6 * H)
    dv = 2 * dk
    tb = min(256, seq)
    nt = seq // tb
    row = lambda b, t: b * nt + t
    return pl.pallas_call(
        functools.partial(_retention_kernel, tb=tb, dk=dk, dv=dv, n_heads=H),
        out_shape=jax.ShapeDtypeStruct((M, H * dv), BF16),
        grid=(batch, nt),
        in_specs=[
            pl.BlockSpec((tb, H * dk), lambda b, t: (row(b, t), 0)),
            pl.BlockSpec((tb, H * dk), lambda b, t: (row(b, t), 1)),
            pl.BlockSpec((tb, H * dv), lambda b, t: (row(b, t), 1)),
            pl.BlockSpec((tb, H * dv), lambda b, t: (row(b, t), 2)),
            pl.BlockSpec((1, dv), lambda b, t: (0, 0)),
        ],
        out_specs=pl.BlockSpec((tb, H * dv), lambda b, t: (row(b, t), 0)),
        scratch_shapes=[pltpu.VMEM((H, dk, dv), F32), pltpu.VMEM((H, tb, tb), F32),
                        pltpu.VMEM((H, tb, dk), F32), pltpu.VMEM((H, tb, dk), F32)],
        compiler_params=_params("arbitrary", "arbitrary"),
        name="retention",
    )(proj, proj, proj, proj, gn_g.reshape(1, dv))


def kernel(x, c, ada_w, ada_b, norm_g, ffn_w_in, ffn_w_out, da_w_qkv, da_lambda, da_subln_g,
           da_w_o, ret_w_qkvg, ret_gn_g, ret_w_o, final_g):
    B, S, D = x.shape
    depth = ada_w.shape[0]
    mods = _ada_mods(c, ada_w, ada_b)

    da_hd = D // (2 * DA_HEADS)
    da_scale = jnp.concatenate([jnp.full((2 * DA_HEADS * da_hd,), da_hd ** -0.5 * LOG2E, F32),
                                jnp.ones((da_w_qkv.shape[2] - 2 * DA_HEADS * da_hd,), F32)])
    ret_dk = D // RET_HEADS
    ret_scale = jnp.concatenate([jnp.ones((RET_HEADS * ret_dk,), F32),
                                 jnp.full((RET_HEADS * ret_dk,), ret_dk ** -0.5, F32),
                                 jnp.ones((ret_w_qkvg.shape[2] - 2 * RET_HEADS * ret_dk,), F32)])
    wb = {
        ("ffn_in", 0, 0): ffn_w_in[0, 0].astype(BF16),
        ("ffn_out", 0, 0): ffn_w_out[0, 0].astype(BF16),
        ("da_qkv", 0): (da_w_qkv[0] * da_scale).astype(BF16),
    }
    jobs = [(("da_o", 0), da_w_o, (0,), None),
            (("ffn_in", 0, 1), ffn_w_in, (0, 1), None),
            (("ffn_out", 0, 1), ffn_w_out, (0, 1), None)]
    for i in range(1, depth):
        jobs += [(("ffn_in", i, k), ffn_w_in, (i, k), None) for k in range(2)]
        jobs += [(("ffn_out", i, k), ffn_w_out, (i, k), None) for k in range(2)]
        j = i // N_MIXERS
        if i % N_MIXERS == 0:
            jobs += [(("da_qkv", j), da_w_qkv, (j,), da_scale), (("da_o", j), da_w_o, (j,), None)]
        else:
            jobs += [(("ret", j), ret_w_qkvg, (j,), ret_scale), (("ret_o", j), ret_w_o, (j,), None)]

    xf = x.reshape(B * S, D)
    for i in range(depth):
        m = mods[i]
        xf = _ffn(xf, m, 0, norm_g[i, 0], wb["ffn_in", i, 0], wb["ffn_out", i, 0], seq=S)
        j = i // N_MIXERS
        if i % N_MIXERS == 0:
            lambda_init = 0.8 - 0.6 * math.exp(-0.3 * i)
            qkv = _norm_mod_matmul(xf, m, 3, norm_g[i, 1], wb["da_qkv", j], seq=S)
            heads, cast = _diff_attention(qkv, da_lambda[j], da_subln_g[j], batch=B, seq=S,
                                          lambda_init=lambda_init,
                                          cast_jobs=[job[1:] for job in jobs])
            wb.update({job[0]: w for job, w in zip(jobs, cast)})
            jobs = []
            xf = _matmul_residual(heads, wb["da_o", j], xf, m, 5, seq=S)
        else:
            proj = _norm_mod_matmul(xf, m, 3, norm_g[i, 1], wb["ret", j], seq=S)
            heads = _retention(proj, ret_gn_g[j], batch=B, seq=S)
            xf = _matmul_residual(heads, wb["ret_o", j], xf, m, 5, seq=S)
        xf = _ffn(xf, m, 6, norm_g[i, 2], wb["ffn_in", i, 1], wb["ffn_out", i, 1],
                  final_g=final_g if i == depth - 1 else None, seq=S)
    return xf.reshape(B, S, D)
```

```python
import functools
import math

import jax
import jax.numpy as jnp
from jax import lax
from jax.experimental import pallas as pl
from jax.experimental.pallas import tpu as pltpu

F32 = jnp.float32
BF16 = jnp.bfloat16

EPS = 1e-5
LOG2E = math.log2(math.e)
CHUNK = 64
N_MOD = 9
N_MIXERS = 2
DA_HEADS = 8
HEADS_PER_STEP = 2
RET_HEADS = 8

BF16_SUBLANES = 16
V7X_VMEM_BYTES = 64 * 1024 * 1024
VMEM_LIMIT_BYTES = V7X_VMEM_BYTES - 8 * 1024 * 1024


def _params(*semantics):
    return pltpu.CompilerParams(dimension_semantics=semantics, vmem_limit_bytes=VMEM_LIMIT_BYTES)


def _dot(a, b):
    return jnp.dot(a, b, preferred_element_type=F32)


def _dot_nt(a, b):
    return lax.dot_general(a, b, (((1,), (1,)), ((), ())), preferred_element_type=F32)


def _dot_tn(a, b):
    return lax.dot_general(a, b, (((0,), (0,)), ((), ())), preferred_element_type=F32)


def _rms(x, g):
    return x * lax.rsqrt(jnp.mean(x * x, axis=-1, keepdims=True) + EPS) * g


NORM_ROWS = 32
FIRST_STEP_PARTS = 4


def _norm_mod_into(h_ref, x_ref, g_ref, shift_ref, scale_ref, rows=None, unrolled=False):
    gs = g_ref[...] * (1.0 + scale_ref[...])
    shift = shift_ref[...]
    start, stop = rows if rows is not None else (0, x_ref.shape[0])

    def chunk(r0):
        x = x_ref[pl.ds(r0, NORM_ROWS), :]
        rinv = lax.rsqrt(jnp.mean(x * x, axis=-1, keepdims=True) + EPS)
        h_ref[pl.ds(r0, NORM_ROWS), :] = (x * rinv * gs + shift).astype(BF16)

    if unrolled:
        for r0 in range(start, stop, NORM_ROWS):
            chunk(r0)
    else:
        def body(i, carry):
            chunk(pl.multiple_of(start + i * NORM_ROWS, NORM_ROWS))
            return carry

        lax.fori_loop(0, (stop - start) // NORM_ROWS, body, 0, unroll=4)


def _ada_kernel(c_ref, w_ref, b_ref, o_ref):
    cs = jax.nn.silu(c_ref[...]).astype(BF16)
    o_ref[...] = _dot(cs, w_ref[...].astype(BF16)) + b_ref[...]


def _ada_mods(c, ada_w, ada_b):
    B, D = c.shape
    L, _, W = ada_w.shape
    rows = 8
    tn = min(1024, W)
    c_pad = jnp.zeros((rows, D), F32).at[:B].set(c)
    out = pl.pallas_call(
        _ada_kernel,
        out_shape=jax.ShapeDtypeStruct((L, rows, W), F32),
        grid=(L, W // tn),
        in_specs=[
            pl.BlockSpec((rows, D), lambda l, n: (0, 0)),
            pl.BlockSpec((None, D, tn), lambda l, n: (l, 0, n)),
            pl.BlockSpec((None, 1, tn), lambda l, n: (l, 0, n)),
        ],
        out_specs=pl.BlockSpec((None, rows, tn), lambda l, n: (l, 0, n)),
        compiler_params=_params("parallel", "parallel"),
        name="ada_mods",
    )(c_pad, ada_w, ada_b.reshape(L, 1, W))
    return out[:, :B].reshape(L, B, N_MOD, D).transpose(0, 2, 1, 3)[:, :, :, None, :]


def _ffn_kernel(x_ref, sh_ref, sc_ref, gt_ref, ng_ref, wg_ref, wu_ref, wo_ref, *rest, final, nf):
    if final:
        fg_ref, o_ref, h_ref = rest
    else:
        o_ref, h_ref = rest
    j = pl.program_id(1)
    assert nf >= 2, "the first and last hidden-dimension steps are distinct code paths"

    def contribution(rows=slice(None)):
        h = h_ref[rows, :]
        a = _dot(h, wg_ref[...])
        b = _dot(h, wu_ref[...])
        act = (jax.nn.silu(a) * b).astype(BF16)
        return _dot(act, wo_ref[...])

    @pl.when(j == 0)
    def _():
        part = x_ref.shape[0] // FIRST_STEP_PARTS
        _norm_mod_into(h_ref, x_ref, ng_ref, sh_ref, sc_ref, rows=(0, part))
        for q in range(FIRST_STEP_PARTS):
            if q + 1 < FIRST_STEP_PARTS:
                _norm_mod_into(h_ref, x_ref, ng_ref, sh_ref, sc_ref,
                               rows=((q + 1) * part, (q + 2) * part), unrolled=True)
            o_ref[q * part:(q + 1) * part, :] = contribution(slice(q * part, (q + 1) * part))

    @pl.when((j > 0) & (j < nf - 1))
    def _():
        o_ref[...] += contribution()

    @pl.when(j == nf - 1)
    def _():
        o_ref[...] = x_ref[...] + (0.5 * gt_ref[...]) * (o_ref[...] + contribution())
        if final:
            for r0 in range(0, x_ref.shape[0], NORM_ROWS):
                o_ref[r0:r0 + NORM_ROWS, :] = _rms(o_ref[r0:r0 + NORM_ROWS, :], fg_ref[...])


def _ffn(x, mods, k0, norm_g, w_in, w_out, final_g=None, *, seq):
    M, D = x.shape
    F = w_out.shape[0]
    tm = min(1024, seq)
    tf = min(512, F)
    nf = F // tf
    bpb = seq // tm
    vec = lambda k: pl.BlockSpec((None, None, 1, D), lambda i, j: (k, i // bpb, 0, 0))
    in_specs = [
        pl.BlockSpec((tm, D), lambda i, j: (i, 0)),
        vec(k0), vec(k0 + 1), vec(k0 + 2),
        pl.BlockSpec((1, D), lambda i, j: (0, 0)),
        pl.BlockSpec((D, tf), lambda i, j: (0, j)),
        pl.BlockSpec((D, tf), lambda i, j: (0, nf + j)),
        pl.BlockSpec((tf, D), lambda i, j: (j, 0)),
    ]
    args = [x, mods, mods, mods, norm_g.reshape(1, D), w_in, w_in, w_out]
    if final_g is not None:
        in_specs.append(pl.BlockSpec((1, D), lambda i, j: (0, 0)))
        args.append(final_g.reshape(1, D))
    return pl.pallas_call(
        functools.partial(_ffn_kernel, final=final_g is not None, nf=nf),
        out_shape=jax.ShapeDtypeStruct((M, D), F32),
        grid=(M // tm, nf),
        in_specs=in_specs,
        out_specs=pl.BlockSpec((tm, D), lambda i, j: (i, 0)),
        scratch_shapes=[pltpu.VMEM((tm, D), BF16)],
        compiler_params=_params("parallel", "arbitrary"),
        name="ffn",
    )(*args)


def _nm_matmul_kernel(x_ref, sh_ref, sc_ref, ng_ref, w_ref, o_ref, h_ref):
    n = pl.program_id(1)

    @pl.when(n == 0)
    def _():
        part = x_ref.shape[0] // FIRST_STEP_PARTS
        _norm_mod_into(h_ref, x_ref, ng_ref, sh_ref, sc_ref, rows=(0, part))
        for q in range(FIRST_STEP_PARTS):
            if q + 1 < FIRST_STEP_PARTS:
                _norm_mod_into(h_ref, x_ref, ng_ref, sh_ref, sc_ref,
                               rows=((q + 1) * part, (q + 2) * part), unrolled=True)
            rows = slice(q * part, (q + 1) * part)
            o_ref[rows, :] = _dot(h_ref[rows, :], w_ref[...]).astype(o_ref.dtype)

    @pl.when(n > 0)
    def _():
        o_ref[...] = _dot(h_ref[...], w_ref[...]).astype(o_ref.dtype)


def _norm_mod_matmul(x, mods, k0, norm_g, w, *, seq):
    M, D = x.shape
    N = w.shape[1]
    tm = min(1024, seq)
    tn = min(2048, N)
    bpb = seq // tm
    vec = lambda k: pl.BlockSpec((None, None, 1, D), lambda i, n: (k, i // bpb, 0, 0))
    return pl.pallas_call(
        _nm_matmul_kernel,
        out_shape=jax.ShapeDtypeStruct((M, N), BF16),
        grid=(M // tm, N // tn),
        in_specs=[
            pl.BlockSpec((tm, D), lambda i, n: (i, 0)),
            vec(k0), vec(k0 + 1),
            pl.BlockSpec((1, D), lambda i, n: (0, 0)),
            pl.BlockSpec((D, tn), lambda i, n: (0, n)),
        ],
        out_specs=pl.BlockSpec((tm, tn), lambda i, n: (i, n)),
        scratch_shapes=[pltpu.VMEM((tm, D), BF16)],
        compiler_params=_params("parallel", "arbitrary"),
        name="norm_mod_matmul",
    )(x, mods, mods, norm_g.reshape(1, D), w)


def _mm_res_kernel(a_ref, w_ref, x_ref, gt_ref, o_ref):
    o_ref[...] = x_ref[...] + gt_ref[...] * _dot(a_ref[...], w_ref[...])


def _matmul_residual(a, w, x, mods, k, *, seq):
    M, K = a.shape
    N = w.shape[1]
    if 2 * K * N * w.dtype.itemsize <= V7X_VMEM_BYTES // 4:
        tm, tn = min(512, seq), N
    else:
        tm, tn = min(1024, seq), min(1024, N)
    bpb = seq // tm
    return pl.pallas_call(
        _mm_res_kernel,
        out_shape=jax.ShapeDtypeStruct((M, N), F32),
        grid=(M // tm, N // tn),
        in_specs=[
            pl.BlockSpec((tm, K), lambda i, n: (i, 0)),
            pl.BlockSpec((K, tn), lambda i, n: (0, n)),
            pl.BlockSpec((tm, tn), lambda i, n: (i, n)),
            pl.BlockSpec((None, None, 1, tn), lambda i, n: (k, i // bpb, 0, n)),
        ],
        out_specs=pl.BlockSpec((tm, tn), lambda i, n: (i, n)),
        compiler_params=_params("parallel", "parallel"),
        name="matmul_residual",
    )(a, w, x, mods)


def _da_attn_kernel(q_ref, k_ref, v_ref, lam_ref, g_ref, *rest, t, hd, n_heads, lambda_init,
                    cast_scaled, cast_chunks):
    n_cast_in = len(cast_scaled) + sum(cast_scaled)
    cast_in = rest[:n_cast_in]
    o_ref = rest[n_cast_in]
    cast_out = rest[n_cast_in + 1:n_cast_in + 1 + len(cast_scaled)]
    vt_ref, bias_ref, sa_ref, sb_ref, m_ref, acc_ref = rest[n_cast_in + 1 + len(cast_scaled):]

    qi = pl.program_id(2)
    nblk = vt_ref.shape[1]
    vd = 2 * hd
    hps = vt_ref.shape[0]

    step = (pl.program_id(0) * pl.num_programs(1) + pl.program_id(1)) * pl.num_programs(2) + qi
    n_steps = pl.num_programs(0) * pl.num_programs(1) * pl.num_programs(2)
    pos = 0
    for dst_ref, has_scale, n_chunks in zip(cast_out, cast_scaled, cast_chunks):
        src_ref = cast_in[pos]
        scale_ref = cast_in[pos + 1] if has_scale else None
        pos += 2 if has_scale else 1
        moved = (step == 0) | ((step * n_chunks) // n_steps != ((step - 1) * n_chunks) // n_steps)

        @pl.when(moved)
        def _(dst_ref=dst_ref, src_ref=src_ref, scale_ref=scale_ref):
            w = src_ref[...]
            if scale_ref is not None:
                w = w * scale_ref[...]
            dst_ref[...] = w.astype(BF16)

    lam = lam_ref[...]
    lam_val = (jnp.exp(jnp.sum(lam[0:1] * lam[1:2], axis=-1, keepdims=True))
               - jnp.exp(jnp.sum(lam[2:3] * lam[3:4], axis=-1, keepdims=True)) + lambda_init)

    n_blocks = qi + 1
    n_quads = n_blocks // 4

    def slope_of(hh):
        hv = jnp.full((1, 1), pl.program_id(1) * hps + hh + 1, jnp.int32).astype(F32)
        return jnp.exp2(hv * (-8.0 / n_heads)) * LOG2E

    for hh in range(hps):
        c0 = hh * vd
        slope = slope_of(hh)

        @pl.when(qi == 0)
        def _(hh=hh, c0=c0, slope=slope):
            for b in range(nblk):
                vt_ref[hh, b, 0:vd, :] = v_ref[b * t:(b + 1) * t, c0:c0 + vd].astype(F32).T.astype(BF16)
                vt_ref[hh, b, vd:, :] = jnp.ones((vt_ref.shape[2] - vd, t), BF16)
            r = lax.broadcasted_iota(jnp.int32, (t, t), 0)
            c = lax.broadcasted_iota(jnp.int32, (t, t), 1)
            bias_ref[hh, 0] = slope * r.astype(F32)
            bias_ref[hh, 1] = jnp.where((r // CHUNK) <= (c // CHUNK),
                                        slope * (c - jnp.abs(c - r)).astype(F32), -jnp.inf)

    for hh in range(hps):
        c0 = hh * vd
        slope = slope_of(hh)
        qts = [q_ref[:, c0 + n * hd:c0 + (n + 1) * hd].astype(F32).T.astype(BF16) for n in range(2)]
        for n in range(2):
            m_ref[n] = jnp.full((1, t), -jnp.inf, F32)
            acc_ref[n] = jnp.zeros(acc_ref.shape[1:], F32)

        def qk(kb, s_ref, hh=hh, c0=c0, qts=qts):
            kk = pl.multiple_of(kb * t, t)
            bias = bias_ref[hh, (kb == qi).astype(jnp.int32)]
            for n in range(2):
                s_ref[n] = _dot(k_ref[pl.ds(kk, t), c0 + n * hd:c0 + (n + 1) * hd], qts[n]) + bias

        def softmax_pv(kb, s_ref, hh=hh, slope=slope):
            off = slope * ((kb - qi) * t).astype(F32)
            vt_b = vt_ref[hh, kb]
            for n in range(2):
                s = s_ref[n]
                m_old = m_ref[n]
                m_new = jnp.maximum(m_old, jnp.max(s, axis=0, keepdims=True) + off)
                alpha = jnp.exp2(m_old - m_new)
                p = jnp.exp2(s - (m_new - off))
                m_ref[n] = m_new
                acc_ref[n] = alpha * acc_ref[n] + _dot(vt_b, p.astype(BF16))

        qk(0, sa_ref)

        def pair(kb, qk=qk, softmax_pv=softmax_pv):
            qk(kb + 1, sb_ref)
            softmax_pv(kb, sa_ref)
            qk(jnp.minimum(kb + 2, qi), sa_ref)
            softmax_pv(kb + 1, sb_ref)

        def quad_body(i, carry, pair=pair):
            pair(4 * i)
            pair(4 * i + 2)
            return carry

        lax.fori_loop(0, n_quads, quad_body, 0)

        def pair_body(i, carry, pair=pair):
            pair(4 * n_quads + 2 * i)
            return carry

        lax.fori_loop(0, (n_blocks - 4 * n_quads) // 2, pair_body, 0)

        @pl.when(n_blocks % 2 == 1)
        def _(softmax_pv=softmax_pv):
            softmax_pv(qi, sa_ref)

        heads_t = [acc_ref[n, 0:vd, :] / acc_ref[n, vd:vd + 1, :] for n in range(2)]
        o_t = heads_t[0] - lam_val * heads_t[1]
        o_t = o_t * lax.rsqrt(jnp.mean(o_t * o_t, axis=0, keepdims=True) + EPS)
        o_ref[:, c0:c0 + vd] = (o_t.T * (g_ref[...] * (1.0 - lambda_init))).astype(o_ref.dtype)


def _diff_attention(qkv, lam, subln_g, *, batch, seq, lambda_init, cast_jobs=()):
    M = qkv.shape[0]
    H = DA_HEADS
    hd = qkv.shape[1] // (6 * H)
    vd = 2 * hd
    t = min(512, seq)
    nq = seq // t
    hps = HEADS_PER_STEP
    hg = H // hps
    n_steps = batch * hg * nq
    wd = hps * vd
    in_specs = [
        pl.BlockSpec((t, wd), lambda b, h, i: (b * nq + i, h)),
        pl.BlockSpec((seq, wd), lambda b, h, i: (b, hg + h)),
        pl.BlockSpec((seq, wd), lambda b, h, i: (b, 2 * hg + h)),
        pl.BlockSpec((4, hd), lambda b, h, i: (0, 0)),
        pl.BlockSpec((1, vd), lambda b, h, i: (0, 0)),
    ]
    args = [qkv, qkv, qkv, lam, subln_g.reshape(1, vd)]
    out_shape = [jax.ShapeDtypeStruct((M, H * vd), BF16)]
    out_specs = [pl.BlockSpec((t, wd), lambda b, h, i: (b * nq + i, h))]
    scaled, chunks = [], []
    for arr, lead, col_scale in cast_jobs:
        rows, cols = arr.shape[-2:]
        chunk = next(c for c in range(BF16_SUBLANES, rows + 1, BF16_SUBLANES)
                     if rows % c == 0 and rows // c <= n_steps)
        n_chunks = rows // chunk

        def chunk_of(b, h, i, n_chunks=n_chunks):
            return (((b * hg + h) * nq + i) * n_chunks) // n_steps

        in_specs.append(pl.BlockSpec((None,) * len(lead) + (chunk, cols),
                                     lambda b, h, i, lead=lead, f=chunk_of: (*lead, f(b, h, i), 0)))
        args.append(arr)
        if col_scale is not None:
            in_specs.append(pl.BlockSpec((1, cols), lambda b, h, i: (0, 0)))
            args.append(col_scale.reshape(1, cols))
        scaled.append(col_scale is not None)
        chunks.append(n_chunks)
        out_shape.append(jax.ShapeDtypeStruct((rows, cols), BF16))
        out_specs.append(pl.BlockSpec((chunk, cols), lambda b, h, i, f=chunk_of: (f(b, h, i), 0)))
    outs = pl.pallas_call(
        functools.partial(_da_attn_kernel, t=t, hd=hd, n_heads=H, lambda_init=lambda_init,
                          cast_scaled=tuple(scaled), cast_chunks=tuple(chunks)),
        out_shape=out_shape,
        grid=(batch, hg, nq),
        in_specs=in_specs,
        out_specs=out_specs,
        scratch_shapes=[pltpu.VMEM((hps, nq, vd + BF16_SUBLANES, t), BF16),
                        pltpu.VMEM((hps, 2, t, t), F32),
                        pltpu.VMEM((2, t, t), F32), pltpu.VMEM((2, t, t), F32),
                        pltpu.VMEM((2, 1, t), F32),
                        pltpu.VMEM((2, vd + BF16_SUBLANES, t), F32)],
        compiler_params=_params("arbitrary", "arbitrary", "arbitrary"),
        name="diff_attention",
    )(*args)
    return outs[0], outs[1:]


def _ret_log_gamma(h):
    return math.log(1.0 - 2.0 ** (-5.0 - h))


def _retention_kernel(q_ref, k_ref, v_ref, g_ref, gn_ref, o_ref, state_ref, dmat_ref, qdec_ref,
                      kdec_ref, *, tb, dk, dv, n_heads):
    t = pl.program_id(1)

    @pl.when((pl.program_id(0) == 0) & (t == 0))
    def _():
        r = lax.broadcasted_iota(jnp.int32, (tb, tb), 0)
        c = lax.broadcasted_iota(jnp.int32, (tb, tb), 1)
        dist = jnp.abs(r - c).astype(F32)
        visible = (c // CHUNK) <= (r // CHUNK)
        pos = lax.broadcasted_iota(jnp.int32, (tb, dk), 0).astype(F32)
        for h in range(n_heads):
            lg = _ret_log_gamma(h)
            dmat_ref[h] = jnp.where(visible, jnp.exp(lg * dist), 0.0)
            qdec_ref[h] = jnp.exp(lg * pos)
            kdec_ref[h] = jnp.exp(lg * (tb - pos))

    @pl.when(t == 0)
    def _():
        state_ref[...] = jnp.zeros_like(state_ref)

    for h in range(n_heads):
        qh = q_ref[:, h * dk:(h + 1) * dk]
        kh = k_ref[:, h * dk:(h + 1) * dk]
        vh = v_ref[:, h * dv:(h + 1) * dv]
        inner = (_dot_nt(qh, kh) * dmat_ref[h]).astype(BF16)
        qd = (qh.astype(F32) * qdec_ref[h]).astype(BF16)
        kd = (kh.astype(F32) * kdec_ref[h]).astype(BF16)
        state = state_ref[h]
        o = _dot(inner, vh) + _dot(qd, state.astype(BF16))
        state_ref[h] = state * math.exp(_ret_log_gamma(h) * tb) + _dot_tn(kd, vh)
        mu = jnp.mean(o, axis=-1, keepdims=True)
        d = o - mu
        y = d * lax.rsqrt(jnp.mean(d * d, axis=-1, keepdims=True) + EPS) * gn_ref[...]
        gate = g_ref[:, h * dv:(h + 1) * dv].astype(F32)
        o_ref[:, h * dv:(h + 1) * dv] = (jax.nn.silu(gate) * y).astype(o_ref.dtype)


def _retention(proj, gn_g, *, batch, seq):
    M, W = proj.shape
    H = RET_HEADS
    dk = W // (6 * H)
    dv = 2 * dk
    tb = min(256, seq)
    nt = seq // tb
    row = lambda b, t: b * nt + t
    return pl.pallas_call(
        functools.partial(_retention_kernel, tb=tb, dk=dk, dv=dv, n_heads=H),
        out_shape=jax.ShapeDtypeStruct((M, H * dv), BF16),
        grid=(batch, nt),
        in_specs=[
            pl.BlockSpec((tb, H * dk), lambda b, t: (row(b, t), 0)),
            pl.BlockSpec((tb, H * dk), lambda b, t: (row(b, t), 1)),
            pl.BlockSpec((tb, H * dv), lambda b, t: (row(b, t), 1)),
            pl.BlockSpec((tb, H * dv), lambda b, t: (row(b, t), 2)),
            pl.BlockSpec((1, dv), lambda b, t: (0, 0)),
        ],
        out_specs=pl.BlockSpec((tb, H * dv), lambda b, t: (row(b, t), 0)),
        scratch_shapes=[pltpu.VMEM((H, dk, dv), F32), pltpu.VMEM((H, tb, tb), F32),
                        pltpu.VMEM((H, tb, dk), F32), pltpu.VMEM((H, tb, dk), F32)],
        compiler_params=_params("arbitrary", "arbitrary"),
        name="retention",
    )(proj, proj, proj, proj, gn_g.reshape(1, dv))


def kernel(x, c, ada_w, ada_b, norm_g, ffn_w_in, ffn_w_out, da_w_qkv, da_lambda, da_subln_g,
           da_w_o, ret_w_qkvg, ret_gn_g, ret_w_o, final_g):
    B, S, D = x.shape
    depth = ada_w.shape[0]
    mods = _ada_mods(c, ada_w, ada_b)

    da_hd = D // (2 * DA_HEADS)
    da_scale = jnp.concatenate([jnp.full((2 * DA_HEADS * da_hd,), da_hd ** -0.5 * LOG2E, F32),
                                jnp.ones((da_w_qkv.shape[2] - 2 * DA_HEADS * da_hd,), F32)])
    ret_dk = D // RET_HEADS
    ret_scale = jnp.concatenate([jnp.ones((RET_HEADS * ret_dk,), F32),
                                 jnp.full((RET_HEADS * ret_dk,), ret_dk ** -0.5, F32),
                                 jnp.ones((ret_w_qkvg.shape[2] - 2 * RET_HEADS * ret_dk,), F32)])
    wb = {
        ("ffn_in", 0, 0): ffn_w_in[0, 0].astype(BF16),
        ("ffn_out", 0, 0): ffn_w_out[0, 0].astype(BF16),
        ("da_qkv", 0): (da_w_qkv[0] * da_scale).astype(BF16),
    }
    jobs = [(("da_o", 0), da_w_o, (0,), None),
            (("ffn_in", 0, 1), ffn_w_in, (0, 1), None),
            (("ffn_out", 0, 1), ffn_w_out, (0, 1), None)]
    for i in range(1, depth):
        jobs += [(("ffn_in", i, k), ffn_w_in, (i, k), None) for k in range(2)]
        jobs += [(("ffn_out", i, k), ffn_w_out, (i, k), None) for k in range(2)]
        j = i // N_MIXERS
        if i % N_MIXERS == 0:
            jobs += [(("da_qkv", j), da_w_qkv, (j,), da_scale), (("da_o", j), da_w_o, (j,), None)]
        else:
            jobs += [(("ret", j), ret_w_qkvg, (j,), ret_scale), (("ret_o", j), ret_w_o, (j,), None)]

    xf = x.reshape(B * S, D)
    for i in range(depth):
        m = mods[i]
        xf = _ffn(xf, m, 0, norm_g[i, 0], wb["ffn_in", i, 0], wb["ffn_out", i, 0], seq=S)
        j = i // N_MIXERS
        if i % N_MIXERS == 0:
            lambda_init = 0.8 - 0.6 * math.exp(-0.3 * i)
            qkv = _norm_mod_matmul(xf, m, 3, norm_g[i, 1], wb["da_qkv", j], seq=S)
            heads, cast = _diff_attention(qkv, da_lambda[j], da_subln_g[j], batch=B, seq=S,
                                          lambda_init=lambda_init,
                                          cast_jobs=[job[1:] for job in jobs])
            wb.update({job[0]: w for job, w in zip(jobs, cast)})
            jobs = []
            xf = _matmul_residual(heads, wb["da_o", j], xf, m, 5, seq=S)
        else:
            proj = _norm_mod_matmul(xf, m, 3, norm_g[i, 1], wb["ret", j], seq=S)
            heads = _retention(proj, ret_gn_g[j], batch=B, seq=S)
            xf = _matmul_residual(heads, wb["ret_o", j], xf, m, 5, seq=S)
        xf = _ffn(xf, m, 6, norm_g[i, 2], wb["ffn_in", i, 1], wb["ffn_out", i, 1],
                  final_g=final_g if i == depth - 1 else None, seq=S)
    return xf.reshape(B, S, D)
```

```python
import functools
import math

import jax
import jax.numpy as jnp
from jax import lax
from jax.experimental import pallas as pl
from jax.experimental.pallas import tpu as pltpu

F32 = jnp.float32
BF16 = jnp.bfloat16

EPS = 1e-5
LOG2E = math.log2(math.e)
CHUNK = 64
N_MOD = 9
N_MIXERS = 2
DA_HEADS = 8
HEADS_PER_STEP = 2
RET_HEADS = 8

BF16_SUBLANES = 16
V7X_VMEM_BYTES = 64 * 1024 * 1024
VMEM_LIMIT_BYTES = V7X_VMEM_BYTES - 8 * 1024 * 1024
ATTN_VMEM_LIMIT_BYTES = V7X_VMEM_BYTES - 4 * 1024 * 1024


def _params(*semantics, vmem_limit_bytes=VMEM_LIMIT_BYTES):
    return pltpu.CompilerParams(dimension_semantics=semantics, vmem_limit_bytes=vmem_limit_bytes)


def _dot(a, b):
    return jnp.dot(a, b, preferred_element_type=F32)


def _dot_nt(a, b):
    return lax.dot_general(a, b, (((1,), (1,)), ((), ())), preferred_element_type=F32)


def _dot_tn(a, b):
    return lax.dot_general(a, b, (((0,), (0,)), ((), ())), preferred_element_type=F32)


def _rms(x, g):
    return x * lax.rsqrt(jnp.mean(x * x, axis=-1, keepdims=True) + EPS) * g


NORM_ROWS = 32
FIRST_STEP_PARTS = 4


def _norm_mod_into(h_ref, x_ref, g_ref, shift_ref, scale_ref, rows=None, unrolled=False):
    gs = g_ref[...] * (1.0 + scale_ref[...])
    shift = shift_ref[...]
    start, stop = rows if rows is not None else (0, x_ref.shape[0])

    def chunk(r0):
        x = x_ref[pl.ds(r0, NORM_ROWS), :]
        rinv = lax.rsqrt(jnp.mean(x * x, axis=-1, keepdims=True) + EPS)
        h_ref[pl.ds(r0, NORM_ROWS), :] = (x * rinv * gs + shift).astype(BF16)

    if unrolled:
        for r0 in range(start, stop, NORM_ROWS):
            chunk(r0)
    else:
        def body(i, carry):
            chunk(pl.multiple_of(start + i * NORM_ROWS, NORM_ROWS))
            return carry

        lax.fori_loop(0, (stop - start) // NORM_ROWS, body, 0, unroll=4)


def _ada_kernel(c_ref, w_ref, b_ref, o_ref):
    cs = jax.nn.silu(c_ref[...]).astype(BF16)
    o_ref[...] = _dot(cs, w_ref[...].astype(BF16)) + b_ref[...]


def _ada_mods(c, ada_w, ada_b):
    B, D = c.shape
    L, _, W = ada_w.shape
    rows = 8
    tn = min(1024, W)
    c_pad = jnp.zeros((rows, D), F32).at[:B].set(c)
    out = pl.pallas_call(
        _ada_kernel,
        out_shape=jax.ShapeDtypeStruct((L, rows, W), F32),
        grid=(L, W // tn),
        in_specs=[
            pl.BlockSpec((rows, D), lambda l, n: (0, 0)),
            pl.BlockSpec((None, D, tn), lambda l, n: (l, 0, n)),
            pl.BlockSpec((None, 1, tn), lambda l, n: (l, 0, n)),
        ],
        out_specs=pl.BlockSpec((None, rows, tn), lambda l, n: (l, 0, n)),
        compiler_params=_params("parallel", "parallel"),
        name="ada_mods",
    )(c_pad, ada_w, ada_b.reshape(L, 1, W))
    return out[:, :B].reshape(L, B, N_MOD, D).transpose(0, 2, 1, 3)[:, :, :, None, :]


def _ffn_kernel(x_ref, sh_ref, sc_ref, gt_ref, ng_ref, wg_ref, wu_ref, wo_ref, *rest, final, nf):
    if final:
        fg_ref, o_ref, h_ref = rest
    else:
        o_ref, h_ref = rest
    j = pl.program_id(1)
    assert nf >= 2, "the first and last hidden-dimension steps are distinct code paths"

    def contribution(rows=slice(None)):
        h = h_ref[rows, :]
        a = _dot(h, wg_ref[...])
        b = _dot(h, wu_ref[...])
        act = (jax.nn.silu(a) * b).astype(BF16)
        return _dot(act, wo_ref[...])

    @pl.when(j == 0)
    def _():
        part = x_ref.shape[0] // FIRST_STEP_PARTS
        _norm_mod_into(h_ref, x_ref, ng_ref, sh_ref, sc_ref, rows=(0, part))
        for q in range(FIRST_STEP_PARTS):
            if q + 1 < FIRST_STEP_PARTS:
                _norm_mod_into(h_ref, x_ref, ng_ref, sh_ref, sc_ref,
                               rows=((q + 1) * part, (q + 2) * part), unrolled=True)
            o_ref[q * part:(q + 1) * part, :] = contribution(slice(q * part, (q + 1) * part))

    @pl.when((j > 0) & (j < nf - 1))
    def _():
        o_ref[...] += contribution()

    @pl.when(j == nf - 1)
    def _():
        o_ref[...] = x_ref[...] + (0.5 * gt_ref[...]) * (o_ref[...] + contribution())
        if final:
            for r0 in range(0, x_ref.shape[0], NORM_ROWS):
                o_ref[r0:r0 + NORM_ROWS, :] = _rms(o_ref[r0:r0 + NORM_ROWS, :], fg_ref[...])


def _ffn(x, mods, k0, norm_g, w_in, w_out, final_g=None, *, seq):
    M, D = x.shape
    F = w_out.shape[0]
    tm = min(1024, seq)
    tf = min(512, F)
    nf = F // tf
    bpb = seq // tm
    vec = lambda k: pl.BlockSpec((None, None, 1, D), lambda i, j: (k, i // bpb, 0, 0))
    in_specs = [
        pl.BlockSpec((tm, D), lambda i, j: (i, 0)),
        vec(k0), vec(k0 + 1), vec(k0 + 2),
        pl.BlockSpec((1, D), lambda i, j: (0, 0)),
        pl.BlockSpec((D, tf), lambda i, j: (0, j)),
        pl.BlockSpec((D, tf), lambda i, j: (0, nf + j)),
        pl.BlockSpec((tf, D), lambda i, j: (j, 0)),
    ]
    args = [x, mods, mods, mods, norm_g.reshape(1, D), w_in, w_in, w_out]
    if final_g is not None:
        in_specs.append(pl.BlockSpec((1, D), lambda i, j: (0, 0)))
        args.append(final_g.reshape(1, D))
    return pl.pallas_call(
        functools.partial(_ffn_kernel, final=final_g is not None, nf=nf),
        out_shape=jax.ShapeDtypeStruct((M, D), F32),
        grid=(M // tm, nf),
        in_specs=in_specs,
        out_specs=pl.BlockSpec((tm, D), lambda i, j: (i, 0)),
        scratch_shapes=[pltpu.VMEM((tm, D), BF16)],
        compiler_params=_params("parallel", "arbitrary"),
        name="ffn",
    )(*args)


def _nm_matmul_kernel(x_ref, sh_ref, sc_ref, ng_ref, w_ref, o_ref, h_ref):
    n = pl.program_id(1)

    @pl.when(n == 0)
    def _():
        part = x_ref.shape[0] // FIRST_STEP_PARTS
        _norm_mod_into(h_ref, x_ref, ng_ref, sh_ref, sc_ref, rows=(0, part))
        for q in range(FIRST_STEP_PARTS):
            if q + 1 < FIRST_STEP_PARTS:
                _norm_mod_into(h_ref, x_ref, ng_ref, sh_ref, sc_ref,
                               rows=((q + 1) * part, (q + 2) * part), unrolled=True)
            rows = slice(q * part, (q + 1) * part)
            o_ref[rows, :] = _dot(h_ref[rows, :], w_ref[...]).astype(o_ref.dtype)

    @pl.when(n > 0)
    def _():
        o_ref[...] = _dot(h_ref[...], w_ref[...]).astype(o_ref.dtype)


def _norm_mod_matmul(x, mods, k0, norm_g, w, *, seq):
    M, D = x.shape
    N = w.shape[1]
    tm = min(1024, seq)
    tn = min(2048, N)
    bpb = seq // tm
    vec = lambda k: pl.BlockSpec((None, None, 1, D), lambda i, n: (k, i // bpb, 0, 0))
    return pl.pallas_call(
        _nm_matmul_kernel,
        out_shape=jax.ShapeDtypeStruct((M, N), BF16),
        grid=(M // tm, N // tn),
        in_specs=[
            pl.BlockSpec((tm, D), lambda i, n: (i, 0)),
            vec(k0), vec(k0 + 1),
            pl.BlockSpec((1, D), lambda i, n: (0, 0)),
            pl.BlockSpec((D, tn), lambda i, n: (0, n)),
        ],
        out_specs=pl.BlockSpec((tm, tn), lambda i, n: (i, n)),
        scratch_shapes=[pltpu.VMEM((tm, D), BF16)],
        compiler_params=_params("parallel", "arbitrary"),
        name="norm_mod_matmul",
    )(x, mods, mods, norm_g.reshape(1, D), w)


def _mm_res_kernel(a_ref, w_ref, x_ref, gt_ref, o_ref):
    o_ref[...] = x_ref[...] + gt_ref[...] * _dot(a_ref[...], w_ref[...])


def _matmul_residual(a, w, x, mods, k, *, seq):
    M, K = a.shape
    N = w.shape[1]
    if 2 * K * N * w.dtype.itemsize <= V7X_VMEM_BYTES // 4:
        tm, tn = min(512, seq), N
    else:
        tm, tn = min(1024, seq), min(1024, N)
    bpb = seq // tm
    return pl.pallas_call(
        _mm_res_kernel,
        out_shape=jax.ShapeDtypeStruct((M, N), F32),
        grid=(M // tm, N // tn),
        in_specs=[
            pl.BlockSpec((tm, K), lambda i, n: (i, 0)),
            pl.BlockSpec((K, tn), lambda i, n: (0, n)),
            pl.BlockSpec((tm, tn), lambda i, n: (i, n)),
            pl.BlockSpec((None, None, 1, tn), lambda i, n: (k, i // bpb, 0, n)),
        ],
        out_specs=pl.BlockSpec((tm, tn), lambda i, n: (i, n)),
        compiler_params=_params("parallel", "parallel"),
        name="matmul_residual",
    )(a, w, x, mods)


def _da_attn_kernel(q_ref, k_ref, v_ref, lam_ref, g_ref, *rest, t, hd, n_heads, lambda_init,
                    cast_scaled, cast_chunks):
    n_cast_in = len(cast_scaled) + sum(cast_scaled)
    cast_in = rest[:n_cast_in]
    o_ref = rest[n_cast_in]
    cast_out = rest[n_cast_in + 1:n_cast_in + 1 + len(cast_scaled)]
    vt_ref, bias_ref, sa_ref, sb_ref, m_ref, acc_ref = rest[n_cast_in + 1 + len(cast_scaled):]

    qi = pl.program_id(2)
    nblk = vt_ref.shape[1]
    vd = 2 * hd
    hps = vt_ref.shape[0]

    step = (pl.program_id(0) * pl.num_programs(1) + pl.program_id(1)) * pl.num_programs(2) + qi
    n_steps = pl.num_programs(0) * pl.num_programs(1) * pl.num_programs(2)
    pos = 0
    for dst_ref, has_scale, n_chunks in zip(cast_out, cast_scaled, cast_chunks):
        src_ref = cast_in[pos]
        scale_ref = cast_in[pos + 1] if has_scale else None
        pos += 2 if has_scale else 1
        moved = (step == 0) | ((step * n_chunks) // n_steps != ((step - 1) * n_chunks) // n_steps)

        @pl.when(moved)
        def _(dst_ref=dst_ref, src_ref=src_ref, scale_ref=scale_ref):
            w = src_ref[...]
            if scale_ref is not None:
                w = w * scale_ref[...]
            dst_ref[...] = w.astype(BF16)

    lam = lam_ref[...]
    lam_val = (jnp.exp(jnp.sum(lam[0:1] * lam[1:2], axis=-1, keepdims=True))
               - jnp.exp(jnp.sum(lam[2:3] * lam[3:4], axis=-1, keepdims=True)) + lambda_init)

    n_blocks = qi + 1
    n_quads = n_blocks // 4

    def slope_of(hh):
        hv = jnp.full((1, 1), pl.program_id(1) * hps + hh + 1, jnp.int32).astype(F32)
        return jnp.exp2(hv * (-8.0 / n_heads)) * LOG2E

    for hh in range(hps):
        c0 = hh * vd
        slope = slope_of(hh)

        @pl.when(qi == 0)
        def _(hh=hh, c0=c0, slope=slope):
            for b in range(nblk):
                vt_ref[hh, b, 0:vd, :] = v_ref[b * t:(b + 1) * t, c0:c0 + vd].astype(F32).T.astype(BF16)
                vt_ref[hh, b, vd:, :] = jnp.ones((vt_ref.shape[2] - vd, t), BF16)
            r = lax.broadcasted_iota(jnp.int32, (t, t), 0)
            c = lax.broadcasted_iota(jnp.int32, (t, t), 1)
            bias_ref[hh, 0] = slope * r.astype(F32)
            bias_ref[hh, 1] = jnp.where((r // CHUNK) <= (c // CHUNK),
                                        slope * (c - jnp.abs(c - r)).astype(F32), -jnp.inf)

    slopes = [slope_of(hh) for hh in range(hps)]
    qts = [[q_ref[:, hh * vd + n * hd:hh * vd + (n + 1) * hd].astype(F32).T.astype(BF16)
            for n in range(2)] for hh in range(hps)]
    for hh in range(hps):
        for n in range(2):
            m_ref[hh, n] = jnp.full((1, t), -jnp.inf, F32)
            acc_ref[hh, n] = jnp.zeros(acc_ref.shape[2:], F32)

    def qk(kb, s_ref):
        kk = pl.multiple_of(kb * t, t)
        diag = (kb == qi).astype(jnp.int32)
        for hh in range(hps):
            bias = bias_ref[hh, diag]
            for n in range(2):
                c0 = hh * vd + n * hd
                s_ref[hh, n] = _dot(k_ref[pl.ds(kk, t), c0:c0 + hd], qts[hh][n]) + bias

    def softmax_pv(kb, s_ref):
        for hh in range(hps):
            off = slopes[hh] * ((kb - qi) * t).astype(F32)
            vt_b = vt_ref[hh, kb]
            for n in range(2):
                s = s_ref[hh, n]
                m_old = m_ref[hh, n]
                m_new = jnp.maximum(m_old, jnp.max(s, axis=0, keepdims=True) + off)
                alpha = jnp.exp2(m_old - m_new)
                p = jnp.exp2(s - (m_new - off))
                m_ref[hh, n] = m_new
                acc_ref[hh, n] = alpha * acc_ref[hh, n] + _dot(vt_b, p.astype(BF16))

    qk(0, sa_ref)

    def pair(kb):
        qk(kb + 1, sb_ref)
        softmax_pv(kb, sa_ref)
        qk(jnp.minimum(kb + 2, qi), sa_ref)
        softmax_pv(kb + 1, sb_ref)

    def quad_body(i, carry):
        pair(4 * i)
        pair(4 * i + 2)
        return carry

    lax.fori_loop(0, n_quads, quad_body, 0)

    def pair_body(i, carry):
        pair(4 * n_quads + 2 * i)
        return carry

    lax.fori_loop(0, (n_blocks - 4 * n_quads) // 2, pair_body, 0)

    @pl.when(n_blocks % 2 == 1)
    def _():
        softmax_pv(qi, sa_ref)

    for hh in range(hps):
        heads_t = [acc_ref[hh, n, 0:vd, :] / acc_ref[hh, n, vd:vd + 1, :] for n in range(2)]
        o_t = heads_t[0] - lam_val * heads_t[1]
        o_t = o_t * lax.rsqrt(jnp.mean(o_t * o_t, axis=0, keepdims=True) + EPS)
        o_ref[:, hh * vd:(hh + 1) * vd] = (o_t.T * (g_ref[...] * (1.0 - lambda_init))).astype(o_ref.dtype)


def _diff_attention(qkv, lam, subln_g, *, batch, seq, lambda_init, cast_jobs=()):
    M = qkv.shape[0]
    H = DA_HEADS
    hd = qkv.shape[1] // (6 * H)
    vd = 2 * hd
    t = min(512, seq)
    nq = seq // t
    hps = HEADS_PER_STEP
    hg = H // hps
    n_steps = batch * hg * nq
    wd = hps * vd
    in_specs = [
        pl.BlockSpec((t, wd), lambda b, h, i: (b * nq + i, h)),
        pl.BlockSpec((seq, wd), lambda b, h, i: (b, hg + h)),
        pl.BlockSpec((seq, wd), lambda b, h, i: (b, 2 * hg + h)),
        pl.BlockSpec((4, hd), lambda b, h, i: (0, 0)),
        pl.BlockSpec((1, vd), lambda b, h, i: (0, 0)),
    ]
    args = [qkv, qkv, qkv, lam, subln_g.reshape(1, vd)]
    out_shape = [jax.ShapeDtypeStruct((M, H * vd), BF16)]
    out_specs = [pl.BlockSpec((t, wd), lambda b, h, i: (b * nq + i, h))]
    scaled, chunks = [], []
    for arr, lead, col_scale in cast_jobs:
        rows, cols = arr.shape[-2:]
        chunk = next(c for c in range(BF16_SUBLANES, rows + 1, BF16_SUBLANES)
                     if rows % c == 0 and rows // c <= n_steps)
        n_chunks = rows // chunk

        def chunk_of(b, h, i, n_chunks=n_chunks):
            return (((b * hg + h) * nq + i) * n_chunks) // n_steps

        in_specs.append(pl.BlockSpec((None,) * len(lead) + (chunk, cols),
                                     lambda b, h, i, lead=lead, f=chunk_of: (*lead, f(b, h, i), 0)))
        args.append(arr)
        if col_scale is not None:
            in_specs.append(pl.BlockSpec((1, cols), lambda b, h, i: (0, 0)))
            args.append(col_scale.reshape(1, cols))
        scaled.append(col_scale is not None)
        chunks.append(n_chunks)
        out_shape.append(jax.ShapeDtypeStruct((rows, cols), BF16))
        out_specs.append(pl.BlockSpec((chunk, cols), lambda b, h, i, f=chunk_of: (f(b, h, i), 0)))
    outs = pl.pallas_call(
        functools.partial(_da_attn_kernel, t=t, hd=hd, n_heads=H, lambda_init=lambda_init,
                          cast_scaled=tuple(scaled), cast_chunks=tuple(chunks)),
        out_shape=out_shape,
        grid=(batch, hg, nq),
        in_specs=in_specs,
        out_specs=out_specs,
        scratch_shapes=[pltpu.VMEM((hps, nq, vd + BF16_SUBLANES, t), BF16),
                        pltpu.VMEM((hps, 2, t, t), F32),
                        pltpu.VMEM((hps, 2, t, t), F32), pltpu.VMEM((hps, 2, t, t), F32),
                        pltpu.VMEM((hps, 2, 1, t), F32),
                        pltpu.VMEM((hps, 2, vd + BF16_SUBLANES, t), F32)],
        compiler_params=_params("arbitrary", "arbitrary", "arbitrary",
                                vmem_limit_bytes=ATTN_VMEM_LIMIT_BYTES),
        name="diff_attention",
    )(*args)
    return outs[0], outs[1:]


def _ret_log_gamma(h):
    return math.log(1.0 - 2.0 ** (-5.0 - h))


def _retention_kernel(q_ref, k_ref, v_ref, g_ref, gn_ref, o_ref, state_ref, dmat_ref, qdec_ref,
                      kdec_ref, *, tb, dk, dv, n_heads):
    t = pl.program_id(1)

    @pl.when((pl.program_id(0) == 0) & (t == 0))
    def _():
        r = lax.broadcasted_iota(jnp.int32, (tb, tb), 0)
        c = lax.broadcasted_iota(jnp.int32, (tb, tb), 1)
        dist = jnp.abs(r - c).astype(F32)
        visible = (c // CHUNK) <= (r // CHUNK)
        pos = lax.broadcasted_iota(jnp.int32, (tb, dk), 0).astype(F32)
        for h in range(n_heads):
            lg = _ret_log_gamma(h)
            dmat_ref[h] = jnp.where(visible, jnp.exp(lg * dist), 0.0)
            qdec_ref[h] = jnp.exp(lg * pos)
            kdec_ref[h] = jnp.exp(lg * (tb - pos))

    @pl.when(t == 0)
    def _():
        state_ref[...] = jnp.zeros_like(state_ref)

    for h in range(n_heads):
        qh = q_ref[:, h * dk:(h + 1) * dk]
        kh = k_ref[:, h * dk:(h + 1) * dk]
        vh = v_ref[:, h * dv:(h + 1) * dv]
        inner = (_dot_nt(qh, kh) * dmat_ref[h]).astype(BF16)
        qd = (qh.astype(F32) * qdec_ref[h]).astype(BF16)
        kd = (kh.astype(F32) * kdec_ref[h]).astype(BF16)
        state = state_ref[h]
        o = _dot(inner, vh) + _dot(qd, state.astype(BF16))
        state_ref[h] = state * math.exp(_ret_log_gamma(h) * tb) + _dot_tn(kd, vh)
        mu = jnp.mean(o, axis=-1, keepdims=True)
        d = o - mu
        y = d * lax.rsqrt(jnp.mean(d * d, axis=-1, keepdims=True) + EPS) * gn_ref[...]
        gate = g_ref[:, h * dv:(h + 1) * dv].astype(F32)
        o_ref[:, h * dv:(h + 1) * dv] = (jax.nn.silu(gate) * y).astype(o_ref.dtype)


def _retention(proj, gn_g, *, batch, seq):
    M, W = proj.shape
    H = RET_HEADS
    dk = W // (6 * H)
    dv = 2 * dk
    tb = min(256, seq)
    nt = seq // tb
    row = lambda b, t: b * nt + t
    return pl.pallas_call(
        functools.partial(_retention_kernel, tb=tb, dk=dk, dv=dv, n_heads=H),
        out_shape=jax.ShapeDtypeStruct((M, H * dv), BF16),
        grid=(batch, nt),
        in_specs=[
            pl.BlockSpec((tb, H * dk), lambda b, t: (row(b, t), 0)),
            pl.BlockSpec((tb, H * dk), lambda b, t: (row(b, t), 1)),
            pl.BlockSpec((tb, H * dv), lambda b, t: (row(b, t), 1)),
            pl.BlockSpec((tb, H * dv), lambda b, t: (row(b, t), 2)),
            pl.BlockSpec((1, dv), lambda b, t: (0, 0)),
        ],
        out_specs=pl.BlockSpec((tb, H * dv), lambda b, t: (row(b, t), 0)),
        scratch_shapes=[pltpu.VMEM((H, dk, dv), F32), pltpu.VMEM((H, tb, tb), F32),
                        pltpu.VMEM((H, tb, dk), F32), pltpu.VMEM((H, tb, dk), F32)],
        compiler_params=_params("arbitrary", "arbitrary"),
        name="retention",
    )(proj, proj, proj, proj, gn_g.reshape(1, dv))


def kernel(x, c, ada_w, ada_b, norm_g, ffn_w_in, ffn_w_out, da_w_qkv, da_lambda, da_subln_g,
           da_w_o, ret_w_qkvg, ret_gn_g, ret_w_o, final_g):
    B, S, D = x.shape
    depth = ada_w.shape[0]
    mods = _ada_mods(c, ada_w, ada_b)

    da_hd = D // (2 * DA_HEADS)
    da_scale = jnp.concatenate([jnp.full((2 * DA_HEADS * da_hd,), da_hd ** -0.5 * LOG2E, F32),
                                jnp.ones((da_w_qkv.shape[2] - 2 * DA_HEADS * da_hd,), F32)])
    ret_dk = D // RET_HEADS
    ret_scale = jnp.concatenate([jnp.ones((RET_HEADS * ret_dk,), F32),
                                 jnp.full((RET_HEADS * ret_dk,), ret_dk ** -0.5, F32),
                                 jnp.ones((ret_w_qkvg.shape[2] - 2 * RET_HEADS * ret_dk,), F32)])
    wb = {
        ("ffn_in", 0, 0): ffn_w_in[0, 0].astype(BF16),
        ("ffn_out", 0, 0): ffn_w_out[0, 0].astype(BF16),
        ("da_qkv", 0): (da_w_qkv[0] * da_scale).astype(BF16),
    }
    jobs = [(("da_o", 0), da_w_o, (0,), None),
            (("ffn_in", 0, 1), ffn_w_in, (0, 1), None),
            (("ffn_out", 0, 1), ffn_w_out, (0, 1), None)]
    for i in range(1, depth):
        jobs += [(("ffn_in", i, k), ffn_w_in, (i, k), None) for k in range(2)]
        jobs += [(("ffn_out", i, k), ffn_w_out, (i, k), None) for k in range(2)]
        j = i // N_MIXERS
        if i % N_MIXERS == 0:
            jobs += [(("da_qkv", j), da_w_qkv, (j,), da_scale), (("da_o", j), da_w_o, (j,), None)]
        else:
            jobs += [(("ret", j), ret_w_qkvg, (j,), ret_scale), (("ret_o", j), ret_w_o, (j,), None)]

    xf = x.reshape(B * S, D)
    for i in range(depth):
        m = mods[i]
        xf = _ffn(xf, m, 0, norm_g[i, 0], wb["ffn_in", i, 0], wb["ffn_out", i, 0], seq=S)
        j = i // N_MIXERS
        if i % N_MIXERS == 0:
            lambda_init = 0.8 - 0.6 * math.exp(-0.3 * i)
            qkv = _norm_mod_matmul(xf, m, 3, norm_g[i, 1], wb["da_qkv", j], seq=S)
            heads, cast = _diff_attention(qkv, da_lambda[j], da_subln_g[j], batch=B, seq=S,
                                          lambda_init=lambda_init,
                                          cast_jobs=[job[1:] for job in jobs])
            wb.update({job[0]: w for job, w in zip(jobs, cast)})
            jobs = []
            xf = _matmul_residual(heads, wb["da_o", j], xf, m, 5, seq=S)
        else:
            proj = _norm_mod_matmul(xf, m, 3, norm_g[i, 1], wb["ret", j], seq=S)
            heads = _retention(proj, ret_gn_g[j], batch=B, seq=S)
            xf = _matmul_residual(heads, wb["ret_o", j], xf, m, 5, seq=S)
        xf = _ffn(xf, m, 6, norm_g[i, 2], wb["ffn_in", i, 1], wb["ffn_out", i, 1],
                  final_g=final_g if i == depth - 1 else None, seq=S)
    return xf.reshape(B, S, D)
```

```python
import functools
import math

import jax
import jax.numpy as jnp
from jax import lax
from jax.experimental import pallas as pl
from jax.experimental.pallas import tpu as pltpu

F32 = jnp.float32
BF16 = jnp.bfloat16

EPS = 1e-5
LOG2E = math.log2(math.e)
CHUNK = 64
N_MOD = 9
N_MIXERS = 2
DA_HEADS = 8
HEADS_PER_STEP = 2
RET_HEADS = 8

BF16_SUBLANES = 16
V7X_VMEM_BYTES = 64 * 1024 * 1024
VMEM_LIMIT_BYTES = V7X_VMEM_BYTES - 8 * 1024 * 1024
WIDE_VMEM_LIMIT_BYTES = V7X_VMEM_BYTES - 4 * 1024 * 1024


def _params(*semantics, vmem_limit_bytes=VMEM_LIMIT_BYTES):
    return pltpu.CompilerParams(dimension_semantics=semantics, vmem_limit_bytes=vmem_limit_bytes)


def _dot(a, b):
    return jnp.dot(a, b, preferred_element_type=F32)


def _dot_nt(a, b):
    return lax.dot_general(a, b, (((1,), (1,)), ((), ())), preferred_element_type=F32)


def _dot_tn(a, b):
    return lax.dot_general(a, b, (((0,), (0,)), ((), ())), preferred_element_type=F32)


def _rms(x, g):
    return x * lax.rsqrt(jnp.mean(x * x, axis=-1, keepdims=True) + EPS) * g


NORM_ROWS = 32
FIRST_STEP_PARTS = 4


def _norm_mod_into(h_ref, x_ref, g_ref, shift_ref, scale_ref, rows=None, unrolled=False):
    gs = g_ref[...] * (1.0 + scale_ref[...])
    shift = shift_ref[...]
    start, stop = rows if rows is not None else (0, x_ref.shape[0])

    def chunk(r0):
        x = x_ref[pl.ds(r0, NORM_ROWS), :]
        rinv = lax.rsqrt(jnp.mean(x * x, axis=-1, keepdims=True) + EPS)
        h_ref[pl.ds(r0, NORM_ROWS), :] = (x * rinv * gs + shift).astype(BF16)

    if unrolled:
        for r0 in range(start, stop, NORM_ROWS):
            chunk(r0)
    else:
        def body(i, carry):
            chunk(pl.multiple_of(start + i * NORM_ROWS, NORM_ROWS))
            return carry

        lax.fori_loop(0, (stop - start) // NORM_ROWS, body, 0, unroll=4)


def _ada_kernel(c_ref, w_ref, b_ref, o_ref):
    cs = jax.nn.silu(c_ref[...]).astype(BF16)
    o_ref[...] = _dot(cs, w_ref[...].astype(BF16)) + b_ref[...]


def _ada_mods(c, ada_w, ada_b):
    B, D = c.shape
    L, _, W = ada_w.shape
    rows = 8
    tn = min(1024, W)
    c_pad = jnp.zeros((rows, D), F32).at[:B].set(c)
    out = pl.pallas_call(
        _ada_kernel,
        out_shape=jax.ShapeDtypeStruct((L, rows, W), F32),
        grid=(L, W // tn),
        in_specs=[
            pl.BlockSpec((rows, D), lambda l, n: (0, 0)),
            pl.BlockSpec((None, D, tn), lambda l, n: (l, 0, n)),
            pl.BlockSpec((None, 1, tn), lambda l, n: (l, 0, n)),
        ],
        out_specs=pl.BlockSpec((None, rows, tn), lambda l, n: (l, 0, n)),
        compiler_params=_params("parallel", "parallel"),
        name="ada_mods",
    )(c_pad, ada_w, ada_b.reshape(L, 1, W))
    return out[:, :B].reshape(L, B, N_MOD, D).transpose(0, 2, 1, 3)[:, :, :, None, :]


def _ffn_kernel(x_ref, sh_ref, sc_ref, gt_ref, ng_ref, wg_ref, wu_ref, wo_ref, *rest, final, nf):
    if final:
        fg_ref, o_ref, h_ref = rest
    else:
        o_ref, h_ref = rest
    j = pl.program_id(1)
    assert nf >= 2, "the first and last hidden-dimension steps are distinct code paths"

    def contribution(rows=slice(None)):
        h = h_ref[rows, :]
        a = _dot(h, wg_ref[...])
        b = _dot(h, wu_ref[...])
        act = (jax.nn.silu(a) * b).astype(BF16)
        return _dot(act, wo_ref[...])

    @pl.when(j == 0)
    def _():
        part = x_ref.shape[0] // FIRST_STEP_PARTS
        _norm_mod_into(h_ref, x_ref, ng_ref, sh_ref, sc_ref, rows=(0, part))
        for q in range(FIRST_STEP_PARTS):
            if q + 1 < FIRST_STEP_PARTS:
                _norm_mod_into(h_ref, x_ref, ng_ref, sh_ref, sc_ref,
                               rows=((q + 1) * part, (q + 2) * part), unrolled=True)
            o_ref[q * part:(q + 1) * part, :] = contribution(slice(q * part, (q + 1) * part))

    @pl.when((j > 0) & (j < nf - 1))
    def _():
        o_ref[...] += contribution()

    @pl.when(j == nf - 1)
    def _():
        o_ref[...] = x_ref[...] + (0.5 * gt_ref[...]) * (o_ref[...] + contribution())
        if final:
            for r0 in range(0, x_ref.shape[0], NORM_ROWS):
                o_ref[r0:r0 + NORM_ROWS, :] = _rms(o_ref[r0:r0 + NORM_ROWS, :], fg_ref[...])


def _ffn(x, mods, k0, norm_g, w_in, w_out, final_g=None, *, seq):
    M, D = x.shape
    F = w_out.shape[0]
    tm = min(1024, seq)
    tf = min(512, F)
    nf = F // tf
    bpb = seq // tm
    vec = lambda k: pl.BlockSpec((None, None, 1, D), lambda i, j: (k, i // bpb, 0, 0))
    in_specs = [
        pl.BlockSpec((tm, D), lambda i, j: (i, 0)),
        vec(k0), vec(k0 + 1), vec(k0 + 2),
        pl.BlockSpec((1, D), lambda i, j: (0, 0)),
        pl.BlockSpec((D, tf), lambda i, j: (0, j)),
        pl.BlockSpec((D, tf), lambda i, j: (0, nf + j)),
        pl.BlockSpec((tf, D), lambda i, j: (j, 0)),
    ]
    args = [x, mods, mods, mods, norm_g.reshape(1, D), w_in, w_in, w_out]
    if final_g is not None:
        in_specs.append(pl.BlockSpec((1, D), lambda i, j: (0, 0)))
        args.append(final_g.reshape(1, D))
    return pl.pallas_call(
        functools.partial(_ffn_kernel, final=final_g is not None, nf=nf),
        out_shape=jax.ShapeDtypeStruct((M, D), F32),
        grid=(M // tm, nf),
        in_specs=in_specs,
        out_specs=pl.BlockSpec((tm, D), lambda i, j: (i, 0)),
        scratch_shapes=[pltpu.VMEM((tm, D), BF16)],
        compiler_params=_params("parallel", "arbitrary"),
        name="ffn",
    )(*args)


def _nm_matmul_kernel(x_ref, sh_ref, sc_ref, ng_ref, w_ref, o_ref, h_ref):
    n = pl.program_id(1)

    @pl.when(n == 0)
    def _():
        part = x_ref.shape[0] // FIRST_STEP_PARTS
        _norm_mod_into(h_ref, x_ref, ng_ref, sh_ref, sc_ref, rows=(0, part))
        for q in range(FIRST_STEP_PARTS):
            if q + 1 < FIRST_STEP_PARTS:
                _norm_mod_into(h_ref, x_ref, ng_ref, sh_ref, sc_ref,
                               rows=((q + 1) * part, (q + 2) * part), unrolled=True)
            rows = slice(q * part, (q + 1) * part)
            o_ref[rows, :] = _dot(h_ref[rows, :], w_ref[...]).astype(o_ref.dtype)

    @pl.when(n > 0)
    def _():
        o_ref[...] = _dot(h_ref[...], w_ref[...]).astype(o_ref.dtype)


def _norm_mod_matmul(x, mods, k0, norm_g, w, *, seq):
    M, D = x.shape
    N = w.shape[1]
    tm = min(1024, seq)
    tn = 3072 if N % 3072 == 0 else min(2048, N)
    bpb = seq // tm
    vec = lambda k: pl.BlockSpec((None, None, 1, D), lambda i, n: (k, i // bpb, 0, 0))
    return pl.pallas_call(
        _nm_matmul_kernel,
        out_shape=jax.ShapeDtypeStruct((M, N), BF16),
        grid=(M // tm, N // tn),
        in_specs=[
            pl.BlockSpec((tm, D), lambda i, n: (i, 0)),
            vec(k0), vec(k0 + 1),
            pl.BlockSpec((1, D), lambda i, n: (0, 0)),
            pl.BlockSpec((D, tn), lambda i, n: (0, n)),
        ],
        out_specs=pl.BlockSpec((tm, tn), lambda i, n: (i, n)),
        scratch_shapes=[pltpu.VMEM((tm, D), BF16)],
        compiler_params=_params("parallel", "arbitrary", vmem_limit_bytes=WIDE_VMEM_LIMIT_BYTES),
        name="norm_mod_matmul",
    )(x, mods, mods, norm_g.reshape(1, D), w)


def _mm_res_kernel(a_ref, w_ref, x_ref, gt_ref, o_ref):
    o_ref[...] = x_ref[...] + gt_ref[...] * _dot(a_ref[...], w_ref[...])


def _matmul_residual(a, w, x, mods, k, *, seq):
    M, K = a.shape
    N = w.shape[1]
    if 2 * K * N * w.dtype.itemsize <= V7X_VMEM_BYTES // 4:
        tm, tn = min(512, seq), N
    else:
        tm, tn = min(1024, seq), min(1024, N)
    bpb = seq // tm
    return pl.pallas_call(
        _mm_res_kernel,
        out_shape=jax.ShapeDtypeStruct((M, N), F32),
        grid=(M // tm, N // tn),
        in_specs=[
            pl.BlockSpec((tm, K), lambda i, n: (i, 0)),
            pl.BlockSpec((K, tn), lambda i, n: (0, n)),
            pl.BlockSpec((tm, tn), lambda i, n: (i, n)),
            pl.BlockSpec((None, None, 1, tn), lambda i, n: (k, i // bpb, 0, n)),
        ],
        out_specs=pl.BlockSpec((tm, tn), lambda i, n: (i, n)),
        compiler_params=_params("parallel", "parallel"),
        name="matmul_residual",
    )(a, w, x, mods)


def _da_attn_kernel(q_ref, k_ref, v_ref, lam_ref, g_ref, *rest, t, hd, n_heads, lambda_init,
                    cast_scaled, cast_chunks):
    n_cast_in = len(cast_scaled) + sum(cast_scaled)
    cast_in = rest[:n_cast_in]
    o_ref = rest[n_cast_in]
    cast_out = rest[n_cast_in + 1:n_cast_in + 1 + len(cast_scaled)]
    vt_ref, bias_ref, sa_ref, sb_ref, m_ref, acc_ref = rest[n_cast_in + 1 + len(cast_scaled):]

    qi = pl.program_id(2)
    nblk = vt_ref.shape[1]
    vd = 2 * hd
    hps = vt_ref.shape[0]

    step = (pl.program_id(0) * pl.num_programs(1) + pl.program_id(1)) * pl.num_programs(2) + qi
    n_steps = pl.num_programs(0) * pl.num_programs(1) * pl.num_programs(2)
    pos = 0
    for dst_ref, has_scale, n_chunks in zip(cast_out, cast_scaled, cast_chunks):
        src_ref = cast_in[pos]
        scale_ref = cast_in[pos + 1] if has_scale else None
        pos += 2 if has_scale else 1
        moved = (step == 0) | ((step * n_chunks) // n_steps != ((step - 1) * n_chunks) // n_steps)

        @pl.when(moved)
        def _(dst_ref=dst_ref, src_ref=src_ref, scale_ref=scale_ref):
            w = src_ref[...]
            if scale_ref is not None:
                w = w * scale_ref[...]
            dst_ref[...] = w.astype(BF16)

    lam = lam_ref[...]
    lam_val = (jnp.exp(jnp.sum(lam[0:1] * lam[1:2], axis=-1, keepdims=True))
               - jnp.exp(jnp.sum(lam[2:3] * lam[3:4], axis=-1, keepdims=True)) + lambda_init)

    n_blocks = qi + 1
    n_quads = n_blocks // 4

    def slope_of(hh):
        hv = jnp.full((1, 1), pl.program_id(1) * hps + hh + 1, jnp.int32).astype(F32)
        return jnp.exp2(hv * (-8.0 / n_heads)) * LOG2E

    for hh in range(hps):
        c0 = hh * vd
        slope = slope_of(hh)

        @pl.when(qi == 0)
        def _(hh=hh, c0=c0, slope=slope):
            for b in range(nblk):
                vt_ref[hh, b, 0:vd, :] = v_ref[b * t:(b + 1) * t, c0:c0 + vd].astype(F32).T.astype(BF16)
                vt_ref[hh, b, vd:, :] = jnp.ones((vt_ref.shape[2] - vd, t), BF16)
            r = lax.broadcasted_iota(jnp.int32, (t, t), 0)
            c = lax.broadcasted_iota(jnp.int32, (t, t), 1)
            bias_ref[hh, 0] = slope * r.astype(F32)
            bias_ref[hh, 1] = jnp.where((r // CHUNK) <= (c // CHUNK),
                                        slope * (c - jnp.abs(c - r)).astype(F32), -jnp.inf)

    slopes = [slope_of(hh) for hh in range(hps)]
    qts = [[q_ref[:, hh * vd + n * hd:hh * vd + (n + 1) * hd].astype(F32).T.astype(BF16)
            for n in range(2)] for hh in range(hps)]
    for hh in range(hps):
        for n in range(2):
            m_ref[hh, n] = jnp.full((1, t), -jnp.inf, F32)
            acc_ref[hh, n] = jnp.zeros(acc_ref.shape[2:], F32)

    def qk(kb, s_ref):
        kk = pl.multiple_of(kb * t, t)
        diag = (kb == qi).astype(jnp.int32)
        for hh in range(hps):
            bias = bias_ref[hh, diag]
            for n in range(2):
                c0 = hh * vd + n * hd
                s_ref[hh, n] = _dot(k_ref[pl.ds(kk, t), c0:c0 + hd], qts[hh][n]) + bias

    def softmax_pv(kb, s_ref):
        for hh in range(hps):
            off = slopes[hh] * ((kb - qi) * t).astype(F32)
            vt_b = vt_ref[hh, kb]
            for n in range(2):
                s = s_ref[hh, n]
                m_old = m_ref[hh, n]
                m_new = jnp.maximum(m_old, jnp.max(s, axis=0, keepdims=True) + off)
                alpha = jnp.exp2(m_old - m_new)
                p = jnp.exp2(s - (m_new - off))
                m_ref[hh, n] = m_new
                acc_ref[hh, n] = alpha * acc_ref[hh, n] + _dot(vt_b, p.astype(BF16))

    qk(0, sa_ref)

    def pair(kb):
        qk(kb + 1, sb_ref)
        softmax_pv(kb, sa_ref)
        qk(jnp.minimum(kb + 2, qi), sa_ref)
        softmax_pv(kb + 1, sb_ref)

    def quad_body(i, carry):
        pair(4 * i)
        pair(4 * i + 2)
        return carry

    lax.fori_loop(0, n_quads, quad_body, 0)

    def pair_body(i, carry):
        pair(4 * n_quads + 2 * i)
        return carry

    lax.fori_loop(0, (n_blocks - 4 * n_quads) // 2, pair_body, 0)

    @pl.when(n_blocks % 2 == 1)
    def _():
        softmax_pv(qi, sa_ref)

    for hh in range(hps):
        heads_t = [acc_ref[hh, n, 0:vd, :] / acc_ref[hh, n, vd:vd + 1, :] for n in range(2)]
        o_t = heads_t[0] - lam_val * heads_t[1]
        o_t = o_t * lax.rsqrt(jnp.mean(o_t * o_t, axis=0, keepdims=True) + EPS)
        o_ref[:, hh * vd:(hh + 1) * vd] = (o_t.T * (g_ref[...] * (1.0 - lambda_init))).astype(o_ref.dtype)


def _diff_attention(qkv, lam, subln_g, *, batch, seq, lambda_init, cast_jobs=()):
    M = qkv.shape[0]
    H = DA_HEADS
    hd = qkv.shape[1] // (6 * H)
    vd = 2 * hd
    t = min(512, seq)
    nq = seq // t
    hps = HEADS_PER_STEP
    hg = H // hps
    n_steps = batch * hg * nq
    wd = hps * vd
    in_specs = [
        pl.BlockSpec((t, wd), lambda b, h, i: (b * nq + i, h)),
        pl.BlockSpec((seq, wd), lambda b, h, i: (b, hg + h)),
        pl.BlockSpec((seq, wd), lambda b, h, i: (b, 2 * hg + h)),
        pl.BlockSpec((4, hd), lambda b, h, i: (0, 0)),
        pl.BlockSpec((1, vd), lambda b, h, i: (0, 0)),
    ]
    args = [qkv, qkv, qkv, lam, subln_g.reshape(1, vd)]
    out_shape = [jax.ShapeDtypeStruct((M, H * vd), BF16)]
    out_specs = [pl.BlockSpec((t, wd), lambda b, h, i: (b * nq + i, h))]
    scaled, chunks = [], []
    for arr, lead, col_scale in cast_jobs:
        rows, cols = arr.shape[-2:]
        chunk = next(c for c in range(BF16_SUBLANES, rows + 1, BF16_SUBLANES)
                     if rows % c == 0 and rows // c <= n_steps)
        n_chunks = rows // chunk

        def chunk_of(b, h, i, n_chunks=n_chunks):
            return (((b * hg + h) * nq + i) * n_chunks) // n_steps

        in_specs.append(pl.BlockSpec((None,) * len(lead) + (chunk, cols),
                                     lambda b, h, i, lead=lead, f=chunk_of: (*lead, f(b, h, i), 0)))
        args.append(arr)
        if col_scale is not None:
            in_specs.append(pl.BlockSpec((1, cols), lambda b, h, i: (0, 0)))
            args.append(col_scale.reshape(1, cols))
        scaled.append(col_scale is not None)
        chunks.append(n_chunks)
        out_shape.append(jax.ShapeDtypeStruct((rows, cols), BF16))
        out_specs.append(pl.BlockSpec((chunk, cols), lambda b, h, i, f=chunk_of: (f(b, h, i), 0)))
    outs = pl.pallas_call(
        functools.partial(_da_attn_kernel, t=t, hd=hd, n_heads=H, lambda_init=lambda_init,
                          cast_scaled=tuple(scaled), cast_chunks=tuple(chunks)),
        out_shape=out_shape,
        grid=(batch, hg, nq),
        in_specs=in_specs,
        out_specs=out_specs,
        scratch_shapes=[pltpu.VMEM((hps, nq, vd + BF16_SUBLANES, t), BF16),
                        pltpu.VMEM((hps, 2, t, t), F32),
                        pltpu.VMEM((hps, 2, t, t), F32), pltpu.VMEM((hps, 2, t, t), F32),
                        pltpu.VMEM((hps, 2, 1, t), F32),
                        pltpu.VMEM((hps, 2, vd + BF16_SUBLANES, t), F32)],
        compiler_params=_params("arbitrary", "arbitrary", "arbitrary",
                                vmem_limit_bytes=WIDE_VMEM_LIMIT_BYTES),
        name="diff_attention",
    )(*args)
    return outs[0], outs[1:]


def _ret_log_gamma(h):
    return math.log(1.0 - 2.0 ** (-5.0 - h))


def _retention_kernel(q_ref, k_ref, v_ref, g_ref, gn_ref, o_ref, state_ref, dmat_ref, qdec_ref,
                      kdec_ref, *, tb, dk, dv, n_heads):
    t = pl.program_id(1)

    @pl.when((pl.program_id(0) == 0) & (t == 0))
    def _():
        r = lax.broadcasted_iota(jnp.int32, (tb, tb), 0)
        c = lax.broadcasted_iota(jnp.int32, (tb, tb), 1)
        dist = jnp.abs(r - c).astype(F32)
        visible = (c // CHUNK) <= (r // CHUNK)
        pos = lax.broadcasted_iota(jnp.int32, (tb, dk), 0).astype(F32)
        for h in range(n_heads):
            lg = _ret_log_gamma(h)
            dmat_ref[h] = jnp.where(visible, jnp.exp(lg * dist), 0.0)
            qdec_ref[h] = jnp.exp(lg * pos)
            kdec_ref[h] = jnp.exp(lg * (tb - pos))

    @pl.when(t == 0)
    def _():
        state_ref[...] = jnp.zeros_like(state_ref)

    for h in range(n_heads):
        qh = q_ref[:, h * dk:(h + 1) * dk]
        kh = k_ref[:, h * dk:(h + 1) * dk]
        vh = v_ref[:, h * dv:(h + 1) * dv]
        inner = (_dot_nt(qh, kh) * dmat_ref[h]).astype(BF16)
        qd = (qh.astype(F32) * qdec_ref[h]).astype(BF16)
        kd = (kh.astype(F32) * kdec_ref[h]).astype(BF16)
        state = state_ref[h]
        o = _dot(inner, vh) + _dot(qd, state.astype(BF16))
        state_ref[h] = state * math.exp(_ret_log_gamma(h) * tb) + _dot_tn(kd, vh)
        mu = jnp.mean(o, axis=-1, keepdims=True)
        d = o - mu
        y = d * lax.rsqrt(jnp.mean(d * d, axis=-1, keepdims=True) + EPS) * gn_ref[...]
        gate = g_ref[:, h * dv:(h + 1) * dv].astype(F32)
        o_ref[:, h * dv:(h + 1) * dv] = (jax.nn.silu(gate) * y).astype(o_ref.dtype)


def _retention(proj, gn_g, *, batch, seq):
    M, W = proj.shape
    H = RET_HEADS
    dk = W // (6 * H)
    dv = 2 * dk
    tb = min(256, seq)
    nt = seq // tb
    row = lambda b, t: b * nt + t
    return pl.pallas_call(
        functools.partial(_retention_kernel, tb=tb, dk=dk, dv=dv, n_heads=H),
        out_shape=jax.ShapeDtypeStruct((M, H * dv), BF16),
        grid=(batch, nt),
        in_specs=[
            pl.BlockSpec((tb, H * dk), lambda b, t: (row(b, t), 0)),
            pl.BlockSpec((tb, H * dk), lambda b, t: (row(b, t), 1)),
            pl.BlockSpec((tb, H * dv), lambda b, t: (row(b, t), 1)),
            pl.BlockSpec((tb, H * dv), lambda b, t: (row(b, t), 2)),
            pl.BlockSpec((1, dv), lambda b, t: (0, 0)),
        ],
        out_specs=pl.BlockSpec((tb, H * dv), lambda b, t: (row(b, t), 0)),
        scratch_shapes=[pltpu.VMEM((H, dk, dv), F32), pltpu.VMEM((H, tb, tb), F32),
                        pltpu.VMEM((H, tb, dk), F32), pltpu.VMEM((H, tb, dk), F32)],
        compiler_params=_params("arbitrary", "arbitrary"),
        name="retention",
    )(proj, proj, proj, proj, gn_g.reshape(1, dv))


def kernel(x, c, ada_w, ada_b, norm_g, ffn_w_in, ffn_w_out, da_w_qkv, da_lambda, da_subln_g,
           da_w_o, ret_w_qkvg, ret_gn_g, ret_w_o, final_g):
    B, S, D = x.shape
    depth = ada_w.shape[0]
    mods = _ada_mods(c, ada_w, ada_b)

    da_hd = D // (2 * DA_HEADS)
    da_scale = jnp.concatenate([jnp.full((2 * DA_HEADS * da_hd,), da_hd ** -0.5 * LOG2E, F32),
                                jnp.ones((da_w_qkv.shape[2] - 2 * DA_HEADS * da_hd,), F32)])
    ret_dk = D // RET_HEADS
    ret_scale = jnp.concatenate([jnp.ones((RET_HEADS * ret_dk,), F32),
                                 jnp.full((RET_HEADS * ret_dk,), ret_dk ** -0.5, F32),
                                 jnp.ones((ret_w_qkvg.shape[2] - 2 * RET_HEADS * ret_dk,), F32)])
    wb = {
        ("ffn_in", 0, 0): ffn_w_in[0, 0].astype(BF16),
        ("ffn_out", 0, 0): ffn_w_out[0, 0].astype(BF16),
        ("da_qkv", 0): (da_w_qkv[0] * da_scale).astype(BF16),
    }
    jobs = [(("da_o", 0), da_w_o, (0,), None),
            (("ffn_in", 0, 1), ffn_w_in, (0, 1), None),
            (("ffn_out", 0, 1), ffn_w_out, (0, 1), None)]
    for i in range(1, depth):
        jobs += [(("ffn_in", i, k), ffn_w_in, (i, k), None) for k in range(2)]
        jobs += [(("ffn_out", i, k), ffn_w_out, (i, k), None) for k in range(2)]
        j = i // N_MIXERS
        if i % N_MIXERS == 0:
            jobs += [(("da_qkv", j), da_w_qkv, (j,), da_scale), (("da_o", j), da_w_o, (j,), None)]
        else:
            jobs += [(("ret", j), ret_w_qkvg, (j,), ret_scale), (("ret_o", j), ret_w_o, (j,), None)]

    xf = x.reshape(B * S, D)
    for i in range(depth):
        m = mods[i]
        xf = _ffn(xf, m, 0, norm_g[i, 0], wb["ffn_in", i, 0], wb["ffn_out", i, 0], seq=S)
        j = i // N_MIXERS
        if i % N_MIXERS == 0:
            lambda_init = 0.8 - 0.6 * math.exp(-0.3 * i)
            qkv = _norm_mod_matmul(xf, m, 3, norm_g[i, 1], wb["da_qkv", j], seq=S)
            heads, cast = _diff_attention(qkv, da_lambda[j], da_subln_g[j], batch=B, seq=S,
                                          lambda_init=lambda_init,
                                          cast_jobs=[job[1:] for job in jobs])
            wb.update({job[0]: w for job, w in zip(jobs, cast)})
            jobs = []
            xf = _matmul_residual(heads, wb["da_o", j], xf, m, 5, seq=S)
        else:
            proj = _norm_mod_matmul(xf, m, 3, norm_g[i, 1], wb["ret", j], seq=S)
            heads = _retention(proj, ret_gn_g[j], batch=B, seq=S)
            xf = _matmul_residual(heads, wb["ret_o", j], xf, m, 5, seq=S)
        xf = _ffn(xf, m, 6, norm_g[i, 2], wb["ffn_in", i, 1], wb["ffn_out", i, 1],
                  final_g=final_g if i == depth - 1 else None, seq=S)
    return xf.reshape(B, S, D)
```

```python
import functools
import math

import jax
import jax.numpy as jnp
from jax import lax
from jax.experimental import pallas as pl
from jax.experimental.pallas import tpu as pltpu

F32 = jnp.float32
BF16 = jnp.bfloat16

EPS = 1e-5
LOG2E = math.log2(math.e)
CHUNK = 64
N_MOD = 9
N_MIXERS = 2
DA_HEADS = 8
HEADS_PER_STEP = 2
RET_HEADS = 8

BF16_SUBLANES = 16
V7X_VMEM_BYTES = 64 * 1024 * 1024
VMEM_LIMIT_BYTES = V7X_VMEM_BYTES - 8 * 1024 * 1024
WIDE_VMEM_LIMIT_BYTES = V7X_VMEM_BYTES - 4 * 1024 * 1024


def _params(*semantics, vmem_limit_bytes=VMEM_LIMIT_BYTES):
    return pltpu.CompilerParams(dimension_semantics=semantics, vmem_limit_bytes=vmem_limit_bytes)


def _dot(a, b):
    return jnp.dot(a, b, preferred_element_type=F32)


def _dot_nt(a, b):
    return lax.dot_general(a, b, (((1,), (1,)), ((), ())), preferred_element_type=F32)


def _dot_tn(a, b):
    return lax.dot_general(a, b, (((0,), (0,)), ((), ())), preferred_element_type=F32)


def _rms(x, g):
    return x * lax.rsqrt(jnp.mean(x * x, axis=-1, keepdims=True) + EPS) * g


NORM_ROWS = 32
FIRST_STEP_PARTS = 4


def _norm_mod_into(h_ref, x_ref, g_ref, shift_ref, scale_ref, rows=None, unrolled=False):
    gs = g_ref[...] * (1.0 + scale_ref[...])
    shift = shift_ref[...]
    start, stop = rows if rows is not None else (0, x_ref.shape[0])

    def chunk(r0):
        x = x_ref[pl.ds(r0, NORM_ROWS), :]
        rinv = lax.rsqrt(jnp.mean(x * x, axis=-1, keepdims=True) + EPS)
        h_ref[pl.ds(r0, NORM_ROWS), :] = (x * rinv * gs + shift).astype(BF16)

    if unrolled:
        for r0 in range(start, stop, NORM_ROWS):
            chunk(r0)
    else:
        def body(i, carry):
            chunk(pl.multiple_of(start + i * NORM_ROWS, NORM_ROWS))
            return carry

        lax.fori_loop(0, (stop - start) // NORM_ROWS, body, 0, unroll=4)


def _ada_kernel(c_ref, w_ref, b_ref, o_ref):
    cs = jax.nn.silu(c_ref[...]).astype(BF16)
    o_ref[...] = _dot(cs, w_ref[...].astype(BF16)) + b_ref[...]


def _ada_mods(c, ada_w, ada_b):
    B, D = c.shape
    L, _, W = ada_w.shape
    rows = 8
    tn = min(1024, W)
    c_pad = jnp.zeros((rows, D), F32).at[:B].set(c)
    out = pl.pallas_call(
        _ada_kernel,
        out_shape=jax.ShapeDtypeStruct((L, rows, W), F32),
        grid=(L, W // tn),
        in_specs=[
            pl.BlockSpec((rows, D), lambda l, n: (0, 0)),
            pl.BlockSpec((None, D, tn), lambda l, n: (l, 0, n)),
            pl.BlockSpec((None, 1, tn), lambda l, n: (l, 0, n)),
        ],
        out_specs=pl.BlockSpec((None, rows, tn), lambda l, n: (l, 0, n)),
        compiler_params=_params("parallel", "parallel"),
        name="ada_mods",
    )(c_pad, ada_w, ada_b.reshape(L, 1, W))
    return out[:, :B].reshape(L, B, N_MOD, D).transpose(0, 2, 1, 3)[:, :, :, None, :]


def _ffn_kernel(x_ref, sh_ref, sc_ref, gt_ref, ng_ref, wg_ref, wu_ref, wo_ref, *rest, final, nf):
    if final:
        fg_ref, o_ref, h_ref = rest
    else:
        o_ref, h_ref = rest
    j = pl.program_id(1)
    assert nf >= 2, "the first and last hidden-dimension steps are distinct code paths"

    def contribution(rows=slice(None)):
        h = h_ref[rows, :]
        a = _dot(h, wg_ref[...])
        b = _dot(h, wu_ref[...])
        act = (jax.nn.silu(a) * b).astype(BF16)
        return _dot(act, wo_ref[...])

    @pl.when(j == 0)
    def _():
        part = x_ref.shape[0] // FIRST_STEP_PARTS
        _norm_mod_into(h_ref, x_ref, ng_ref, sh_ref, sc_ref, rows=(0, part))
        for q in range(FIRST_STEP_PARTS):
            if q + 1 < FIRST_STEP_PARTS:
                _norm_mod_into(h_ref, x_ref, ng_ref, sh_ref, sc_ref,
                               rows=((q + 1) * part, (q + 2) * part), unrolled=True)
            o_ref[q * part:(q + 1) * part, :] = contribution(slice(q * part, (q + 1) * part))

    @pl.when((j > 0) & (j < nf - 1))
    def _():
        o_ref[...] += contribution()

    @pl.when(j == nf - 1)
    def _():
        o_ref[...] = x_ref[...] + (0.5 * gt_ref[...]) * (o_ref[...] + contribution())
        if final:
            for r0 in range(0, x_ref.shape[0], NORM_ROWS):
                o_ref[r0:r0 + NORM_ROWS, :] = _rms(o_ref[r0:r0 + NORM_ROWS, :], fg_ref[...])


def _ffn(x, mods, k0, norm_g, w_in, w_out, final_g=None, *, seq):
    M, D = x.shape
    F = w_out.shape[0]
    tm = min(1024, seq)
    tf = min(512, F)
    nf = F // tf
    bpb = seq // tm
    vec = lambda k: pl.BlockSpec((None, None, 1, D), lambda i, j: (k, i // bpb, 0, 0))
    in_specs = [
        pl.BlockSpec((tm, D), lambda i, j: (i, 0)),
        vec(k0), vec(k0 + 1), vec(k0 + 2),
        pl.BlockSpec((1, D), lambda i, j: (0, 0)),
        pl.BlockSpec((D, tf), lambda i, j: (0, j)),
        pl.BlockSpec((D, tf), lambda i, j: (0, nf + j)),
        pl.BlockSpec((tf, D), lambda i, j: (j, 0)),
    ]
    args = [x, mods, mods, mods, norm_g.reshape(1, D), w_in, w_in, w_out]
    if final_g is not None:
        in_specs.append(pl.BlockSpec((1, D), lambda i, j: (0, 0)))
        args.append(final_g.reshape(1, D))
    return pl.pallas_call(
        functools.partial(_ffn_kernel, final=final_g is not None, nf=nf),
        out_shape=jax.ShapeDtypeStruct((M, D), F32),
        grid=(M // tm, nf),
        in_specs=in_specs,
        out_specs=pl.BlockSpec((tm, D), lambda i, j: (i, 0)),
        scratch_shapes=[pltpu.VMEM((tm, D), BF16)],
        compiler_params=_params("parallel", "arbitrary"),
        name="ffn",
    )(*args)


def _nm_matmul_kernel(x_ref, sh_ref, sc_ref, ng_ref, w_ref, o_ref, h_ref):
    n = pl.program_id(1)

    @pl.when(n == 0)
    def _():
        part = x_ref.shape[0] // FIRST_STEP_PARTS
        _norm_mod_into(h_ref, x_ref, ng_ref, sh_ref, sc_ref, rows=(0, part))
        for q in range(FIRST_STEP_PARTS):
            if q + 1 < FIRST_STEP_PARTS:
                _norm_mod_into(h_ref, x_ref, ng_ref, sh_ref, sc_ref,
                               rows=((q + 1) * part, (q + 2) * part), unrolled=True)
            rows = slice(q * part, (q + 1) * part)
            o_ref[rows, :] = _dot(h_ref[rows, :], w_ref[...]).astype(o_ref.dtype)

    @pl.when(n > 0)
    def _():
        o_ref[...] = _dot(h_ref[...], w_ref[...]).astype(o_ref.dtype)


def _norm_mod_matmul(x, mods, k0, norm_g, w, *, seq):
    M, D = x.shape
    N = w.shape[1]
    tm = min(1024, seq)
    tn = 3072 if N % 3072 == 0 else min(2048, N)
    bpb = seq // tm
    vec = lambda k: pl.BlockSpec((None, None, 1, D), lambda i, n: (k, i // bpb, 0, 0))
    return pl.pallas_call(
        _nm_matmul_kernel,
        out_shape=jax.ShapeDtypeStruct((M, N), BF16),
        grid=(M // tm, N // tn),
        in_specs=[
            pl.BlockSpec((tm, D), lambda i, n: (i, 0)),
            vec(k0), vec(k0 + 1),
            pl.BlockSpec((1, D), lambda i, n: (0, 0)),
            pl.BlockSpec((D, tn), lambda i, n: (0, n)),
        ],
        out_specs=pl.BlockSpec((tm, tn), lambda i, n: (i, n)),
        scratch_shapes=[pltpu.VMEM((tm, D), BF16)],
        compiler_params=_params("parallel", "arbitrary", vmem_limit_bytes=WIDE_VMEM_LIMIT_BYTES),
        name="norm_mod_matmul",
    )(x, mods, mods, norm_g.reshape(1, D), w)


def _mm_res_kernel(a_ref, w_ref, x_ref, gt_ref, o_ref):
    o_ref[...] = x_ref[...] + gt_ref[...] * _dot(a_ref[...], w_ref[...])


def _matmul_residual(a, w, x, mods, k, *, seq):
    M, K = a.shape
    N = w.shape[1]
    if 2 * K * N * w.dtype.itemsize <= V7X_VMEM_BYTES // 4:
        tm, tn = min(512, seq), N
    else:
        tm, tn = min(1024, seq), min(1024, N)
    bpb = seq // tm
    return pl.pallas_call(
        _mm_res_kernel,
        out_shape=jax.ShapeDtypeStruct((M, N), F32),
        grid=(M // tm, N // tn),
        in_specs=[
            pl.BlockSpec((tm, K), lambda i, n: (i, 0)),
            pl.BlockSpec((K, tn), lambda i, n: (0, n)),
            pl.BlockSpec((tm, tn), lambda i, n: (i, n)),
            pl.BlockSpec((None, None, 1, tn), lambda i, n: (k, i // bpb, 0, n)),
        ],
        out_specs=pl.BlockSpec((tm, tn), lambda i, n: (i, n)),
        compiler_params=_params("parallel", "parallel"),
        name="matmul_residual",
    )(a, w, x, mods)


def _da_attn_kernel(q_ref, k_ref, v_ref, lam_ref, g_ref, *rest, t, hd, n_heads, lambda_init,
                    cast_scaled, cast_chunks):
    n_cast_in = len(cast_scaled) + sum(cast_scaled)
    cast_in = rest[:n_cast_in]
    o_ref = rest[n_cast_in]
    cast_out = rest[n_cast_in + 1:n_cast_in + 1 + len(cast_scaled)]
    vt_ref, bias_ref, sa_ref, sb_ref, m_ref, acc_ref = rest[n_cast_in + 1 + len(cast_scaled):]

    qi = pl.program_id(2)
    nblk = vt_ref.shape[1]
    vd = 2 * hd
    hps = vt_ref.shape[0]

    step = (pl.program_id(0) * pl.num_programs(1) + pl.program_id(1)) * pl.num_programs(2) + qi
    n_steps = pl.num_programs(0) * pl.num_programs(1) * pl.num_programs(2)
    pos = 0
    for dst_ref, has_scale, n_chunks in zip(cast_out, cast_scaled, cast_chunks):
        src_ref = cast_in[pos]
        scale_ref = cast_in[pos + 1] if has_scale else None
        pos += 2 if has_scale else 1
        moved = (step == 0) | ((step * n_chunks) // n_steps != ((step - 1) * n_chunks) // n_steps)

        @pl.when(moved)
        def _(dst_ref=dst_ref, src_ref=src_ref, scale_ref=scale_ref):
            w = src_ref[...]
            if scale_ref is not None:
                w = w * scale_ref[...]
            dst_ref[...] = w.astype(BF16)

    lam = lam_ref[...]
    lam_val = (jnp.exp(jnp.sum(lam[0:1] * lam[1:2], axis=-1, keepdims=True))
               - jnp.exp(jnp.sum(lam[2:3] * lam[3:4], axis=-1, keepdims=True)) + lambda_init)

    def slope_of(hh):
        hv = jnp.full((1, 1), pl.program_id(1) * hps + hh + 1, jnp.int32).astype(F32)
        return jnp.exp2(hv * (-8.0 / n_heads)) * LOG2E

    for hh in range(hps):
        c0 = hh * vd
        slope = slope_of(hh)

        @pl.when(qi == 0)
        def _(hh=hh, c0=c0, slope=slope):
            for b in range(nblk):
                vt_ref[hh, b, 0:vd, :] = v_ref[b * t:(b + 1) * t, c0:c0 + vd].astype(F32).T.astype(BF16)
                vt_ref[hh, b, vd:, :] = jnp.ones((vt_ref.shape[2] - vd, t), BF16)
            r = lax.broadcasted_iota(jnp.int32, (t, t), 0)
            c = lax.broadcasted_iota(jnp.int32, (t, t), 1)
            bias_ref[hh, 0] = slope * r.astype(F32)
            bias_ref[hh, 1] = jnp.where((r // CHUNK) <= (c // CHUNK),
                                        slope * (c - jnp.abs(c - r)).astype(F32), -jnp.inf)

    slopes = [slope_of(hh) for hh in range(hps)]
    qts = [[q_ref[:, hh * vd + n * hd:hh * vd + (n + 1) * hd].astype(F32).T.astype(BF16)
            for n in range(2)] for hh in range(hps)]
    for hh in range(hps):
        for n in range(2):
            m_ref[hh, n] = jnp.full((1, t), -jnp.inf, F32)
            acc_ref[hh, n] = jnp.zeros(acc_ref.shape[2:], F32)

    def qk(kb, s_ref):
        kk = pl.multiple_of(kb * t, t)
        diag = (kb == qi).astype(jnp.int32)
        for hh in range(hps):
            bias = bias_ref[hh, diag]
            for n in range(2):
                c0 = hh * vd + n * hd
                s_ref[hh, n] = _dot(k_ref[pl.ds(kk, t), c0:c0 + hd], qts[hh][n]) + bias

    def softmax_pv(kb, s_ref):
        for hh in range(hps):
            off = slopes[hh] * ((kb - qi) * t).astype(F32)
            vt_b = vt_ref[hh, kb]
            for n in range(2):
                s = s_ref[hh, n]
                m_old = m_ref[hh, n]
                m_new = jnp.maximum(m_old, jnp.max(s, axis=0, keepdims=True) + off)
                alpha = jnp.exp2(m_old - m_new)
                p = jnp.exp2(s - (m_new - off))
                m_ref[hh, n] = m_new
                acc_ref[hh, n] = alpha * acc_ref[hh, n] + _dot(vt_b, p.astype(BF16))

    qk(0, sa_ref)

    def pair(kb):
        qk(kb + 1, sb_ref)
        softmax_pv(kb, sa_ref)
        qk(kb + 2, sa_ref)
        softmax_pv(kb + 1, sb_ref)

    n_quads = qi // 4

    def quad_body(i, carry):
        pair(4 * i)
        pair(4 * i + 2)
        return carry

    lax.fori_loop(0, n_quads, quad_body, 0)

    def pair_body(i, carry):
        pair(4 * n_quads + 2 * i)
        return carry

    lax.fori_loop(0, (qi - 4 * n_quads) // 2, pair_body, 0)

    @pl.when(qi % 2 == 0)
    def _():
        softmax_pv(qi, sa_ref)

    @pl.when(qi % 2 == 1)
    def _():
        qk(qi, sb_ref)
        softmax_pv(qi - 1, sa_ref)
        softmax_pv(qi, sb_ref)

    for hh in range(hps):
        heads_t = [acc_ref[hh, n, 0:vd, :] / acc_ref[hh, n, vd:vd + 1, :] for n in range(2)]
        o_t = heads_t[0] - lam_val * heads_t[1]
        o_t = o_t * lax.rsqrt(jnp.mean(o_t * o_t, axis=0, keepdims=True) + EPS)
        o_ref[:, hh * vd:(hh + 1) * vd] = (o_t.T * (g_ref[...] * (1.0 - lambda_init))).astype(o_ref.dtype)


def _diff_attention(qkv, lam, subln_g, *, batch, seq, lambda_init, cast_jobs=()):
    M = qkv.shape[0]
    H = DA_HEADS
    hd = qkv.shape[1] // (6 * H)
    vd = 2 * hd
    t = min(512, seq)
    nq = seq // t
    hps = HEADS_PER_STEP
    hg = H // hps
    n_steps = batch * hg * nq
    wd = hps * vd
    in_specs = [
        pl.BlockSpec((t, wd), lambda b, h, i: (b * nq + i, h)),
        pl.BlockSpec((seq, wd), lambda b, h, i: (b, hg + h)),
        pl.BlockSpec((seq, wd), lambda b, h, i: (b, 2 * hg + h)),
        pl.BlockSpec((4, hd), lambda b, h, i: (0, 0)),
        pl.BlockSpec((1, vd), lambda b, h, i: (0, 0)),
    ]
    args = [qkv, qkv, qkv, lam, subln_g.reshape(1, vd)]
    out_shape = [jax.ShapeDtypeStruct((M, H * vd), BF16)]
    out_specs = [pl.BlockSpec((t, wd), lambda b, h, i: (b * nq + i, h))]
    scaled, chunks = [], []
    for arr, lead, col_scale in cast_jobs:
        rows, cols = arr.shape[-2:]
        chunk = next(c for c in range(BF16_SUBLANES, rows + 1, BF16_SUBLANES)
                     if rows % c == 0 and rows // c <= n_steps)
        n_chunks = rows // chunk

        def chunk_of(b, h, i, n_chunks=n_chunks):
            return (((b * hg + h) * nq + i) * n_chunks) // n_steps

        in_specs.append(pl.BlockSpec((None,) * len(lead) + (chunk, cols),
                                     lambda b, h, i, lead=lead, f=chunk_of: (*lead, f(b, h, i), 0)))
        args.append(arr)
        if col_scale is not None:
            in_specs.append(pl.BlockSpec((1, cols), lambda b, h, i: (0, 0)))
            args.append(col_scale.reshape(1, cols))
        scaled.append(col_scale is not None)
        chunks.append(n_chunks)
        out_shape.append(jax.ShapeDtypeStruct((rows, cols), BF16))
        out_specs.append(pl.BlockSpec((chunk, cols), lambda b, h, i, f=chunk_of: (f(b, h, i), 0)))
    outs = pl.pallas_call(
        functools.partial(_da_attn_kernel, t=t, hd=hd, n_heads=H, lambda_init=lambda_init,
                          cast_scaled=tuple(scaled), cast_chunks=tuple(chunks)),
        out_shape=out_shape,
        grid=(batch, hg, nq),
        in_specs=in_specs,
        out_specs=out_specs,
        scratch_shapes=[pltpu.VMEM((hps, nq, vd + BF16_SUBLANES, t), BF16),
                        pltpu.VMEM((hps, 2, t, t), F32),
                        pltpu.VMEM((hps, 2, t, t), F32), pltpu.VMEM((hps, 2, t, t), F32),
                        pltpu.VMEM((hps, 2, 1, t), F32),
                        pltpu.VMEM((hps, 2, vd + BF16_SUBLANES, t), F32)],
        compiler_params=_params("arbitrary", "arbitrary", "arbitrary",
                                vmem_limit_bytes=WIDE_VMEM_LIMIT_BYTES),
        name="diff_attention",
    )(*args)
    return outs[0], outs[1:]


def _ret_log_gamma(h):
    return math.log(1.0 - 2.0 ** (-5.0 - h))


def _retention_kernel(q_ref, k_ref, v_ref, g_ref, gn_ref, o_ref, state_ref, dmat_ref, qdec_ref,
                      kdec_ref, *, tb, dk, dv, n_heads):
    t = pl.program_id(1)

    @pl.when((pl.program_id(0) == 0) & (t == 0))
    def _():
        r = lax.broadcasted_iota(jnp.int32, (tb, tb), 0)
        c = lax.broadcasted_iota(jnp.int32, (tb, tb), 1)
        dist = jnp.abs(r - c).astype(F32)
        visible = (c // CHUNK) <= (r // CHUNK)
        pos = lax.broadcasted_iota(jnp.int32, (tb, dk), 0).astype(F32)
        for h in range(n_heads):
            lg = _ret_log_gamma(h)
            dmat_ref[h] = jnp.where(visible, jnp.exp(lg * dist), 0.0)
            qdec_ref[h] = jnp.exp(lg * pos)
            kdec_ref[h] = jnp.exp(lg * (tb - pos))

    @pl.when(t == 0)
    def _():
        state_ref[...] = jnp.zeros_like(state_ref)

    for h in range(n_heads):
        qh = q_ref[:, h * dk:(h + 1) * dk]
        kh = k_ref[:, h * dk:(h + 1) * dk]
        vh = v_ref[:, h * dv:(h + 1) * dv]
        inner = (_dot_nt(qh, kh) * dmat_ref[h]).astype(BF16)
        qd = (qh.astype(F32) * qdec_ref[h]).astype(BF16)
        kd = (kh.astype(F32) * kdec_ref[h]).astype(BF16)
        state = state_ref[h]
        o = _dot(inner, vh) + _dot(qd, state.astype(BF16))
        state_ref[h] = state * math.exp(_ret_log_gamma(h) * tb) + _dot_tn(kd, vh)
        mu = jnp.mean(o, axis=-1, keepdims=True)
        d = o - mu
        y = d * lax.rsqrt(jnp.mean(d * d, axis=-1, keepdims=True) + EPS) * gn_ref[...]
        gate = g_ref[:, h * dv:(h + 1) * dv].astype(F32)
        o_ref[:, h * dv:(h + 1) * dv] = (jax.nn.silu(gate) * y).astype(o_ref.dtype)


def _retention(proj, gn_g, *, batch, seq):
    M, W = proj.shape
    H = RET_HEADS
    dk = W // (6 * H)
    dv = 2 * dk
    tb = min(256, seq)
    nt = seq // tb
    row = lambda b, t: b * nt + t
    return pl.pallas_call(
        functools.partial(_retention_kernel, tb=tb, dk=dk, dv=dv, n_heads=H),
        out_shape=jax.ShapeDtypeStruct((M, H * dv), BF16),
        grid=(batch, nt),
        in_specs=[
            pl.BlockSpec((tb, H * dk), lambda b, t: (row(b, t), 0)),
            pl.BlockSpec((tb, H * dk), lambda b, t: (row(b, t), 1)),
            pl.BlockSpec((tb, H * dv), lambda b, t: (row(b, t), 1)),
            pl.BlockSpec((tb, H * dv), lambda b, t: (row(b, t), 2)),
            pl.BlockSpec((1, dv), lambda b, t: (0, 0)),
        ],
        out_specs=pl.BlockSpec((tb, H * dv), lambda b, t: (row(b, t), 0)),
        scratch_shapes=[pltpu.VMEM((H, dk, dv), F32), pltpu.VMEM((H, tb, tb), F32),
                        pltpu.VMEM((H, tb, dk), F32), pltpu.VMEM((H, tb, dk), F32)],
        compiler_params=_params("arbitrary", "arbitrary"),
        name="retention",
    )(proj, proj, proj, proj, gn_g.reshape(1, dv))


def kernel(x, c, ada_w, ada_b, norm_g, ffn_w_in, ffn_w_out, da_w_qkv, da_lambda, da_subln_g,
           da_w_o, ret_w_qkvg, ret_gn_g, ret_w_o, final_g):
    B, S, D = x.shape
    depth = ada_w.shape[0]
    mods = _ada_mods(c, ada_w, ada_b)

    da_hd = D // (2 * DA_HEADS)
    da_scale = jnp.concatenate([jnp.full((2 * DA_HEADS * da_hd,), da_hd ** -0.5 * LOG2E, F32),
                                jnp.ones((da_w_qkv.shape[2] - 2 * DA_HEADS * da_hd,), F32)])
    ret_dk = D // RET_HEADS
    ret_scale = jnp.concatenate([jnp.ones((RET_HEADS * ret_dk,), F32),
                                 jnp.full((RET_HEADS * ret_dk,), ret_dk ** -0.5, F32),
                                 jnp.ones((ret_w_qkvg.shape[2] - 2 * RET_HEADS * ret_dk,), F32)])
    wb = {
        ("ffn_in", 0, 0): ffn_w_in[0, 0].astype(BF16),
        ("ffn_out", 0, 0): ffn_w_out[0, 0].astype(BF16),
        ("da_qkv", 0): (da_w_qkv[0] * da_scale).astype(BF16),
    }
    jobs = [(("da_o", 0), da_w_o, (0,), None),
            (("ffn_in", 0, 1), ffn_w_in, (0, 1), None),
            (("ffn_out", 0, 1), ffn_w_out, (0, 1), None)]
    for i in range(1, depth):
        jobs += [(("ffn_in", i, k), ffn_w_in, (i, k), None) for k in range(2)]
        jobs += [(("ffn_out", i, k), ffn_w_out, (i, k), None) for k in range(2)]
        j = i // N_MIXERS
        if i % N_MIXERS == 0:
            jobs += [(("da_qkv", j), da_w_qkv, (j,), da_scale), (("da_o", j), da_w_o, (j,), None)]
        else:
            jobs += [(("ret", j), ret_w_qkvg, (j,), ret_scale), (("ret_o", j), ret_w_o, (j,), None)]

    xf = x.reshape(B * S, D)
    for i in range(depth):
        m = mods[i]
        xf = _ffn(xf, m, 0, norm_g[i, 0], wb["ffn_in", i, 0], wb["ffn_out", i, 0], seq=S)
        j = i // N_MIXERS
        if i % N_MIXERS == 0:
            lambda_init = 0.8 - 0.6 * math.exp(-0.3 * i)
            qkv = _norm_mod_matmul(xf, m, 3, norm_g[i, 1], wb["da_qkv", j], seq=S)
            heads, cast = _diff_attention(qkv, da_lambda[j], da_subln_g[j], batch=B, seq=S,
                                          lambda_init=lambda_init,
                                          cast_jobs=[job[1:] for job in jobs])
            wb.update({job[0]: w for job, w in zip(jobs, cast)})
            jobs = []
            xf = _matmul_residual(heads, wb["da_o", j], xf, m, 5, seq=S)
        else:
            proj = _norm_mod_matmul(xf, m, 3, norm_g[i, 1], wb["ret", j], seq=S)
            heads = _retention(proj, ret_gn_g[j], batch=B, seq=S)
            xf = _matmul_residual(heads, wb["ret_o", j], xf, m, 5, seq=S)
        xf = _ffn(xf, m, 6, norm_g[i, 2], wb["ffn_in", i, 1], wb["ffn_out", i, 1],
                  final_g=final_g if i == depth - 1 else None, seq=S)
    return xf.reshape(B, S, D)
```
